```python
import jax, jax.numpy as jnp
from jax import lax
import numpy as np

D_MODEL = 1024
BATCH = 8
SEQ = 4096
DEPTH = 1
DEC_BATCH = 8
DEC_SEQ = 16
PAST_LEN = 4096

CHUNK = 64
Q_BLOCK = 128
HG_HEADS = 4
HG_DK = 128
HG_DV = 128
MLA_HEADS = 4
MLA_NOPE = 128
MLA_ROPE = 64
MLA_V = 128
Q_LORA = 384
KV_LORA = 256
ROPE_THETA = 10000.0
MLA_SCALE = (MLA_NOPE + MLA_ROPE) ** -0.5
MIX_WIDTH = HG_HEADS * HG_DV + MLA_HEADS * MLA_V
IN_WIDTH = 2 * HG_HEADS * HG_DK + 2 * HG_HEADS * HG_DV + Q_LORA + KV_LORA + MLA_ROPE
D_FF = 2816
CONV_W = 3
EPS = 1e-6

kernel_name = 'hymba_hgrn2_mla_convglu_adaln_stream_step'


def rmsnorm(x, gain):
    xf = x.astype(jnp.float32)
    y = xf * lax.rsqrt(jnp.mean(xf * xf, axis=-1, keepdims=True) + EPS)
    return (y * gain.astype(jnp.float32)).astype(x.dtype)


def rope(x, pos):
    half = x.shape[-1] // 2
    inv_freq = ROPE_THETA ** (-jnp.arange(half, dtype=jnp.float32) / half)
    ang = pos.astype(jnp.float32)[:, None] * inv_freq[None, :]
    cos = jnp.cos(ang)[None, :, None, :]
    sin = jnp.sin(ang)[None, :, None, :]
    xf = x.astype(jnp.float32)
    x1, x2 = xf[..., :half], xf[..., half:]
    return jnp.concatenate([x1 * cos - x2 * sin, x2 * cos + x1 * sin], axis=-1).astype(x.dtype)


def hgrn_chunk(s0, q, k, v, logf):
    L = q.shape[1]
    b = jnp.cumsum(logf, axis=1)
    causal = jnp.tril(jnp.ones((L, L), dtype=bool))
    diff = b[:, :, None] - b[:, None, :]
    decay = jnp.exp(jnp.where(causal[None, :, :, None, None], diff, -jnp.inf))
    scores = jnp.einsum('bthk,bshk,btshk->bhts', q, k, decay)
    o = (jnp.einsum('bthk,bhkv->bthv', q * jnp.exp(b), s0)
         + jnp.einsum('bhts,bshv->bthv', scores, v))
    b_last = b[:, -1]
    s_new = (jnp.exp(b_last)[..., None] * s0
             + jnp.einsum('bshk,bshv->bhkv', k * jnp.exp(b_last[:, None] - b), v))
    return s_new, o


def hgrn_prompt(q, k, v, logf):
    B, S = q.shape[:2]
    nc = S // CHUNK

    def to_chunks(t):
        return t.reshape(B, nc, CHUNK, *t.shape[2:]).swapaxes(0, 1)

    s0 = jnp.zeros((B, HG_HEADS, HG_DK, HG_DV), jnp.float32)
    s_final, o = lax.scan(lambda s, xs: hgrn_chunk(s, *xs), s0,
                          (to_chunks(q), to_chunks(k), to_chunks(v), to_chunks(logf)))
    return s_final, o.swapaxes(0, 1).reshape(B, S, HG_HEADS, HG_DV)


def hgrn_mixer(hq, hf, hi, hg, lb, gain, state):
    B, L = hq.shape[:2]
    dt = hq.dtype
    f32 = jnp.float32
    q = hq.astype(f32).reshape(B, L, HG_HEADS, HG_DK) * HG_DK ** -0.5
    f = lb + (1.0 - lb) * jax.nn.sigmoid(hf.astype(f32))
    k = (1.0 - f).reshape(B, L, HG_HEADS, HG_DK)
    logf = jnp.log(f).reshape(B, L, HG_HEADS, HG_DK)
    v = hi.astype(f32).reshape(B, L, HG_HEADS, HG_DV)
    if state is None:
        s_new, o = hgrn_prompt(q, k, v, logf)
    else:
        s_new, o = hgrn_chunk(state.astype(f32), q, k, v, logf)
    o = o * lax.rsqrt(jnp.mean(o * o, axis=-1, keepdims=True) + EPS) * gain.astype(f32)
    o = o.reshape(B, L, HG_HEADS * HG_DV) * jax.nn.silu(hg.astype(f32))
    return o.astype(dt), s_new.astype(dt)


def attend(q_nope, q_pe, k_nope, k_pe, v, mask):
    s = (jnp.einsum('bqhd,bkhd->bhqk', q_nope, k_nope)
         + jnp.einsum('bqhr,bkr->bhqk', q_pe, k_pe)).astype(jnp.float32) * MLA_SCALE
    if mask is not None:
        s = jnp.where(mask, s, -jnp.inf)
    p = jax.nn.softmax(s, axis=-1).astype(v.dtype)
    return jnp.einsum('bhqk,bkhd->bqhd', p, v)


def chunk_causal_attention(q_nope, q_pe, k_nope, k_pe, v):
    B, L = q_nope.shape[:2]
    nb = L // Q_BLOCK
    key_chunk = jnp.arange(L) // CHUNK

    def blocks(t):
        return t.reshape(B, nb, Q_BLOCK, *t.shape[2:]).swapaxes(0, 1)

    def one_block(xs):
        qn, qp, bi = xs
        q_chunk = (bi * Q_BLOCK + jnp.arange(Q_BLOCK)) // CHUNK
        mask = key_chunk[None, :] <= q_chunk[:, None]
        return attend(qn, qp, k_nope, k_pe, v, mask[None, None])

    o = lax.map(one_block, (blocks(q_nope), blocks(q_pe), jnp.arange(nb)))
    return o.swapaxes(0, 1).reshape(B, L, MLA_HEADS, MLA_V)


def mla_mixer(hcq, hckv, hkpe, pos, qn_g, kvn_g, w_uq, w_uk, w_uv, cache_lat, cache_kpe):
    B, L = hcq.shape[:2]
    q = (rmsnorm(hcq, qn_g) @ w_uq).reshape(B, L, MLA_HEADS, MLA_NOPE + MLA_ROPE)
    q_nope = q[..., :MLA_NOPE]
    q_pe = rope(q[..., MLA_NOPE:], pos)
    lat = rmsnorm(hckv, kvn_g)
    kpe = rope(hkpe[:, :, None, :], pos)[:, :, 0, :]
    if cache_lat is None:
        k_lat, k_pe_all = lat, kpe
    else:
        k_lat = jnp.concatenate([cache_lat.astype(lat.dtype), lat], axis=1)
        k_pe_all = jnp.concatenate([cache_kpe.astype(kpe.dtype), kpe], axis=1)
    k_nope = jnp.einsum('bkc,chd->bkhd', k_lat, w_uk)
    v = jnp.einsum('bkc,chd->bkhd', k_lat, w_uv)
    if cache_lat is None:
        o = chunk_causal_attention(q_nope, q_pe, k_nope, k_pe_all, v)
    else:
        o = attend(q_nope, q_pe, k_nope, k_pe_all, v, None)
    return o.reshape(B, L, MLA_HEADS * MLA_V), lat, kpe


def conv_glu(h, w_up, conv_w, conv_b, w_down, buf):
    B, L = h.shape[:2]
    a, v = jnp.split(h @ w_up, 2, axis=-1)
    if buf is None:
        buf = jnp.zeros((B, CONV_W - 1, D_FF), a.dtype)
    ext = jnp.concatenate([buf.astype(a.dtype), a], axis=1)
    conv = conv_b + sum(conv_w[j] * ext[:, j:j + L] for j in range(CONV_W))
    out = (jax.nn.gelu(conv, approximate=False) * v) @ w_down
    return out, ext[:, L:]


def trunk_layer(x, c, pos, w, cache_lat, cache_kpe, st_hg, st_conv):
    (w_ada, b_ada, g_mix, w_in, lb, hg_g, qn_g, kvn_g, w_uq, w_uk, w_uv, w_out,
     g_ffn, w_up, conv_w, conv_b, w_down) = w
    mod = (jax.nn.silu(c) @ w_ada + b_ada)[:, None, :]
    sh1, sc1, g1, sh2, sc2, g2 = jnp.split(mod, 6, axis=-1)
    h = rmsnorm(x, g_mix) * (1.0 + sc1) + sh1
    proj = h @ w_in
    qk = HG_HEADS * HG_DK
    vw = HG_HEADS * HG_DV
    points = [qk, 2 * qk, 2 * qk + vw, 2 * qk + 2 * vw,
              2 * qk + 2 * vw + Q_LORA, 2 * qk + 2 * vw + Q_LORA + KV_LORA]
    hq, hf, hi, hg, hcq, hckv, hkpe = jnp.split(proj, points, axis=-1)
    o_hg, st_hg_new = hgrn_mixer(hq, hf, hi, hg, lb, hg_g, st_hg)
    o_mla, lat_new, kpe_new = mla_mixer(hcq, hckv, hkpe, pos, qn_g, kvn_g, w_uq, w_uk, w_uv,
                                        cache_lat, cache_kpe)
    x = x + g1 * (jnp.concatenate([o_hg, o_mla], axis=-1) @ w_out)
    h2 = rmsnorm(x, g_ffn) * (1.0 + sc2) + sh2
    f, conv_new = conv_glu(h2, w_up, conv_w, conv_b, w_down, st_conv)
    x = x + g2 * f
    return x, lat_new, kpe_new, st_hg_new, conv_new


def setup_inputs(seed: int = 0) -> dict:
    key = jax.random.key(seed)
    ks = jax.random.split(key, 26)
    D = D_MODEL
    L = DEPTH

    def nrm(k, shape, scale):
        return jax.random.normal(k, shape, jnp.float32) * scale

    return {
        'x_prompt': nrm(ks[0], (BATCH, SEQ, D), 1.0),
        'x_sample': nrm(ks[1], (DEC_BATCH, DEC_SEQ, D), 1.0),
        'c_prompt': nrm(ks[2], (BATCH, D), 1.0),
        'c_sample': nrm(ks[3], (DEC_BATCH, D), 1.0),
        'cache_kv_latent': nrm(ks[4], (L, DEC_BATCH, PAST_LEN, KV_LORA), 1.0),
        'cache_k_rope': nrm(ks[5], (L, DEC_BATCH, PAST_LEN, MLA_ROPE), 1.0),
        'state_hgrn': nrm(ks[6], (L, DEC_BATCH, HG_HEADS, HG_DK, HG_DV), 0.5),
        'state_ffn_conv': nrm(ks[7], (L, DEC_BATCH, CONV_W - 1, D_FF), 1.0),
        'w_ada': nrm(ks[8], (L, D, 6 * D), 0.5 * D ** -0.5),
        'b_ada': nrm(ks[9], (L, 6 * D), 0.01),
        'norm_mix_gain': 1.0 + nrm(ks[10], (L, D), 0.02),
        'w_in': nrm(ks[11], (L, D, IN_WIDTH), D ** -0.5),
        'hg_lb_logits': nrm(ks[12], (L + 1, HG_HEADS * HG_DK), 0.1),
        'hg_norm_gain': 1.0 + nrm(ks[13], (L, HG_DV), 0.02),
        'mla_q_norm_gain': 1.0 + nrm(ks[14], (L, Q_LORA), 0.02),
        'mla_kv_norm_gain': 1.0 + nrm(ks[15], (L, KV_LORA), 0.02),
        'w_uq': nrm(ks[16], (L, Q_LORA, MLA_HEADS * (MLA_NOPE + MLA_ROPE)), Q_LORA ** -0.5),
        'w_uk': nrm(ks[17], (L, KV_LORA, MLA_HEADS, MLA_NOPE), KV_LORA ** -0.5),
        'w_uv': nrm(ks[18], (L, KV_LORA, MLA_HEADS, MLA_V), KV_LORA ** -0.5),
        'w_out': nrm(ks[19], (L, MIX_WIDTH, D), MIX_WIDTH ** -0.5),
        'norm_ffn_gain': 1.0 + nrm(ks[20], (L, D), 0.02),
        'w_up': nrm(ks[21], (L, D, 2 * D_FF), D ** -0.5),
        'conv_w': nrm(ks[22], (L, CONV_W, D_FF), CONV_W ** -0.5),
        'conv_b': nrm(ks[23], (L, D_FF), 0.01),
        'w_down': nrm(ks[24], (L, D_FF, D), D_FF ** -0.5),
        'final_norm_gain': 1.0 + nrm(ks[25], (D,), 0.02),
    }


def reference(x_prompt, x_sample, c_prompt, c_sample, cache_kv_latent, cache_k_rope, state_hgrn,
              state_ffn_conv, w_ada, b_ada, norm_mix_gain, w_in, hg_lb_logits, hg_norm_gain,
              mla_q_norm_gain, mla_kv_norm_gain, w_uq, w_uk, w_uv, w_out, norm_ffn_gain, w_up,
              conv_w, conv_b, w_down, final_norm_gain):
    lower_bounds = jnp.cumsum(jax.nn.softmax(hg_lb_logits.astype(jnp.float32), axis=0), axis=0)
    pos_p = jnp.arange(x_prompt.shape[1])
    pos_s = PAST_LEN + jnp.arange(x_sample.shape[1])
    h_p, h_s = x_prompt, x_sample
    lat_p, kpe_p, hg_p, cv_p = [], [], [], []
    lat_s, kpe_s, hg_s, cv_s = [], [], [], []
    for l in range(DEPTH):
        w = (w_ada[l], b_ada[l], norm_mix_gain[l], w_in[l], lower_bounds[l], hg_norm_gain[l],
             mla_q_norm_gain[l], mla_kv_norm_gain[l], w_uq[l], w_uk[l], w_uv[l], w_out[l],
             norm_ffn_gain[l], w_up[l], conv_w[l], conv_b[l], w_down[l])
        h_p, a, b, cst, d = trunk_layer(h_p, c_prompt, pos_p, w, None, None, None, None)
        lat_p.append(a); kpe_p.append(b); hg_p.append(cst); cv_p.append(d)
        h_s, a, b, cst, d = trunk_layer(h_s, c_sample, pos_s, w, cache_kv_latent[l], cache_k_rope[l],
                                        state_hgrn[l], state_ffn_conv[l])
        lat_s.append(a); kpe_s.append(b); hg_s.append(cst); cv_s.append(d)
    y_prompt = rmsnorm(h_p, final_norm_gain)
    y_sample = rmsnorm(h_s, final_norm_gain)
    return (y_prompt, y_sample,
            jnp.stack(lat_p), jnp.stack(kpe_p), jnp.stack(hg_p), jnp.stack(cv_p),
            jnp.stack(lat_s), jnp.stack(kpe_s), jnp.stack(hg_s), jnp.stack(cv_s))
```

```python
import functools

import numpy as np
import jax
import jax.numpy as jnp
from jax import lax
from jax.experimental import pallas as pl
from jax.experimental.pallas import tpu as pltpu

F32 = jnp.float32
BF16 = jnp.bfloat16

D_MODEL = 1024
CHUNK = 64
HG_HEADS = 4
HG_DK = 128
HG_DV = 128
HG_W = HG_HEADS * HG_DK
MLA_HEADS = 4
MLA_NOPE = 128
MLA_ROPE = 64
ROPE_HALF = MLA_ROPE // 2
MLA_V = 128
Q_LORA = 384
KV_LORA = 256
ROPE_THETA = 10000.0
MLA_SCALE = (MLA_NOPE + MLA_ROPE) ** -0.5
QK_W = 2 * MLA_NOPE
D_FF = 2816
CONV_W = 3
EPS = 1e-6

SUB = 16
FF_TILE = 256
CARRY_ROWS = 8
VMEM_LIMIT = 56 * 1024 * 1024


def _dot(a, b):
    return jnp.dot(a, b, preferred_element_type=F32)


def _dot_nt(a, b):
    return lax.dot_general(a, b, (((1,), (1,)), ((), ())), preferred_element_type=F32)


def _dot_tn(a, b):
    return lax.dot_general(a, b, (((0,), (0,)), ((), ())), preferred_element_type=F32)


def _split3(x):
    x1 = x.astype(BF16)
    r1 = x - x1.astype(F32)
    x2 = r1.astype(BF16)
    r2 = r1 - x2.astype(F32)
    return x1, x2, r2.astype(BF16)


def _rms(x, g):
    return x * lax.rsqrt(jnp.mean(x * x, axis=-1, keepdims=True) + EPS) * g


def _const_spec(shape):
    n = len(shape)
    return pl.BlockSpec(shape, lambda *_: (0,) * n)


def _ada_body(c_ref, w_ref, b_ref, o_ref):
    c = c_ref[...]
    s = c * jax.nn.sigmoid(c)
    s1, s2, _ = _split3(s)
    w = w_ref[...]
    w1 = w.astype(BF16)
    w2 = (w - w1.astype(F32)).astype(BF16)
    o_ref[...] = _dot(s1, w1) + _dot(s1, w2) + _dot(s2, w1) + b_ref[...]


def _ada(c, w_ada, b_ada):
    n, d = c.shape
    nout = w_ada.shape[1]
    tn = 1024
    return pl.pallas_call(
        _ada_body,
        out_shape=jax.ShapeDtypeStruct((n, nout), F32),
        grid=(nout // tn,),
        in_specs=[pl.BlockSpec((n, d), lambda j: (0, 0)),
                  pl.BlockSpec((d, tn), lambda j: (0, j)),
                  pl.BlockSpec((1, tn), lambda j: (0, j))],
        out_specs=pl.BlockSpec((n, tn), lambda j: (0, j)),
        compiler_params=pltpu.CompilerParams(dimension_semantics=("arbitrary",)),
        name="adaln_mod",
    )(c, w_ada, b_ada)


def _rope_slots(r1, r2, nope, out_ref, slot):
    for h in range(MLA_HEADS):
        xh = jnp.where(slot == h, r1, jnp.where(slot == (h + 1) % MLA_HEADS, r2, 0.0))
        out_ref[0, :, QK_W * h:QK_W * h + MLA_NOPE] = nope[:, MLA_NOPE * h:MLA_NOPE * (h + 1)].astype(BF16)
        out_ref[0, :, QK_W * h + MLA_NOPE:QK_W * (h + 1)] = xh.astype(BF16)


def _inproj_body(x_ref, mod_ref, gmix_ref, whg_ref, wcq_ref, wckv_ref, wkx_ref, qng_ref, kvng_ref,
                 wuqn_ref, wuqr_ref, wuk_ref, wuv_ref, cos_ref, sin_ref,
                 hg_out, q_out, k_out, v_out, lat_out, kpe_out):
    x = x_ref[0]
    mod = mod_ref[0]
    sh1 = mod[0:1]
    sc1 = mod[1:2]
    h = _rms(x, gmix_ref[...]) * (1.0 + sc1) + sh1
    hb = h.astype(BF16)
    hg_out[0] = _dot(hb, whg_ref[...])

    cos = cos_ref[...]
    sin = sin_ref[...]
    lane = lax.broadcasted_iota(jnp.int32, cos.shape, 1)
    slot = lane // ROPE_HALF

    cq = _rms(_dot(hb, wcq_ref[...]), qng_ref[...]).astype(BF16)
    qn = _dot(cq, wuqn_ref[...]) * MLA_SCALE
    qr = _dot(cq, wuqr_ref[...]) * MLA_SCALE
    qr1 = qr[:, 0:128] * cos - qr[:, 128:256] * sin
    qr2 = qr[:, 384:512] * cos + qr[:, 256:384] * sin
    _rope_slots(qr1, qr2, qn, q_out, slot)

    lat = _rms(_dot(hb, wckv_ref[...]), kvng_ref[...])
    lat_out[0] = lat
    latb = lat.astype(BF16)
    kn = _dot(latb, wuk_ref[...])
    v_out[0] = _dot(latb, wuv_ref[...]).astype(BF16)
    kx = _dot(hb, wkx_ref[...])
    kr1 = kx[:, 0:128] * cos - kx[:, 128:256] * sin
    kr2 = kx[:, 128:256] * cos + kx[:, 0:128] * sin
    kpe_out[0] = jnp.where(lane < ROPE_HALF, kr1, kr2)[:, 0:MLA_ROPE]
    _rope_slots(kr1, kr2, kn, k_out, slot)


def _inproj(x, mod, gmix, wts, cos, sin, tm):
    B, L, D = x.shape
    (whg, wcq, wckv, wkx, qng, kvng, wuqn, wuqr, wuk, wuv) = wts
    row = lambda w: pl.BlockSpec((1, tm, w), lambda b, i: (b, i, 0))
    in_specs = [row(D), pl.BlockSpec((1, 6, D), lambda b, i: (b, 0, 0)), _const_spec(gmix.shape)]
    in_specs += [_const_spec(w.shape) for w in (whg, wcq, wckv, wkx, qng, kvng, wuqn, wuqr, wuk, wuv)]
    in_specs += [pl.BlockSpec((tm, 128), lambda b, i: (i, 0))] * 2
    widths = (4 * HG_W, MLA_HEADS * QK_W, MLA_HEADS * QK_W, MLA_HEADS * MLA_V, KV_LORA, MLA_ROPE)
    dtypes = (F32, BF16, BF16, BF16, F32, F32)
    return pl.pallas_call(
        _inproj_body,
        out_shape=[jax.ShapeDtypeStruct((B, L, w), dt) for w, dt in zip(widths, dtypes)],
        grid=(B, L // tm),
        in_specs=in_specs,
        out_specs=[row(w) for w in widths],
        compiler_params=pltpu.CompilerParams(dimension_semantics=("arbitrary", "arbitrary"),
                                             vmem_limit_bytes=VMEM_LIMIT),
        name="in_proj",
    )(x, mod, gmix, whg, wcq, wckv, wkx, qng, kvng, wuqn, wuqr, wuk, wuv, cos, sin)


def _hgrn_body(raw_ref, lbl_ref, gain_ref, s0_ref, tin_ref, tup_ref, ones_ref,
               o_out, s_out,
               st_scr, b_scr, q_scr, kk_scr, v_scr, eb_scr, qt_scr, kt_scr, vb_scr, p_scr, o_scr, *, th):
    i = pl.program_id(1)

    @pl.when(i == 0)
    def _():
        for h in range(HG_HEADS):
            st_scr[h] = s0_ref[0, h].T

    lbl = lbl_ref[...]
    e = jnp.exp(lbl - jnp.max(lbl, axis=0, keepdims=True))
    lb = (e / jnp.sum(e, axis=0, keepdims=True))[0:1]

    q = raw_ref[0, :, 0:HG_W] * HG_DK ** -0.5
    f = lb + (1.0 - lb) * jax.nn.sigmoid(raw_ref[0, :, HG_W:2 * HG_W])
    kk = 1.0 - f
    v = raw_ref[0, :, 2 * HG_W:3 * HG_W]
    l1, l2, l3 = _split3(jnp.log(f))
    tin = tin_ref[...]
    tup = tup_ref[...]
    b = _dot(tin, l1) + _dot(tin, l2) + _dot(tin, l3)
    c = _dot(tup, l1) + _dot(tup, l2) + _dot(tup, l3)
    eb = jnp.exp(b)
    b_scr[...] = b
    q_scr[...] = q
    kk_scr[...] = kk
    v_scr[...] = v
    eb_scr[...] = eb
    qt_scr[...] = (q * eb).astype(BF16)
    kt_scr[...] = (kk * jnp.exp(c)).astype(BF16)
    vb_scr[...] = v.astype(BF16)

    rowid = lax.broadcasted_iota(jnp.int32, (SUB, HG_W), 0)

    def step(j, carry):
        r0 = pl.multiple_of(j * SUB, SUB)
        bj = b_scr[pl.ds(r0, SUB), :]
        qj = q_scr[pl.ds(r0, SUB), :]
        for s in range(SUB):
            bs = b_scr[pl.ds(r0 + s, 1), :]
            ks = kk_scr[pl.ds(r0 + s, 1), :]
            dec = jnp.exp(jnp.where(rowid >= s, bj - bs, -jnp.inf))
            p_scr[s * SUB:(s + 1) * SUB, :] = (qj * dec * ks).astype(BF16)
        r = _dot(p_scr[...], ones_ref[...])
        od = jnp.zeros((SUB, HG_W), F32)
        for s in range(SUB):
            od = od + r[s * SUB:(s + 1) * SUB, :] * v_scr[pl.ds(r0 + s, 1), :]
        qt = qt_scr[pl.ds(r0, SUB), :]
        kt = kt_scr[pl.ds(r0, SUB), :]
        vb = vb_scr[pl.ds(r0, SUB), :]
        dj = eb_scr[pl.ds(r0 + SUB - 1, 1), :]
        for h in range(HG_HEADS):
            hs = slice(h * HG_DK, (h + 1) * HG_DK)
            st = st_scr[h]
            oi = _dot_nt(qt[:, hs], st.astype(BF16))
            st_scr[h] = st * dj[:, hs] + _dot_tn(vb[:, hs], kt[:, hs])
            o_scr[pl.ds(r0, SUB), hs] = oi + od[:, hs]
        return carry

    lax.fori_loop(0, th // SUB, step, 0)

    gate = raw_ref[0, :, 3 * HG_W:4 * HG_W]
    gate = gate * jax.nn.sigmoid(gate)
    gain = gain_ref[...]
    for h in range(HG_HEADS):
        hs = slice(h * HG_DV, (h + 1) * HG_DV)
        o_out[0, :, hs] = (_rms(o_scr[:, hs], gain) * gate[:, hs]).astype(BF16)

    @pl.when(i == pl.num_programs(1) - 1)
    def _():
        for h in range(HG_HEADS):
            s_out[0, h] = st_scr[h].T


def _step_matrices(th):
    r = np.arange(th)
    same = (r[:, None] // SUB) == (r[None, :] // SUB)
    tin = (same & (r[None, :] <= r[:, None])).astype(np.float32)
    tup = (same & (r[None, :] > r[:, None])).astype(np.float32)
    hd = np.arange(HG_W) // HG_DK
    ones = (hd[:, None] == hd[None, :]).astype(np.float32)
    return jnp.asarray(tin, BF16), jnp.asarray(tup, BF16), jnp.asarray(ones, BF16)


def _hgrn(raw, lb_logits, gain, s0, th):
    B, L, _ = raw.shape
    tin, tup, ones = _step_matrices(th)
    f32s = lambda: pltpu.VMEM((th, HG_W), F32)
    b16s = lambda: pltpu.VMEM((th, HG_W), BF16)
    return pl.pallas_call(
        functools.partial(_hgrn_body, th=th),
        out_shape=[jax.ShapeDtypeStruct((B, L, HG_W), BF16),
                   jax.ShapeDtypeStruct((B, HG_HEADS, HG_DK, HG_DV), F32)],
        grid=(B, L // th),
        in_specs=[pl.BlockSpec((1, th, 4 * HG_W), lambda b, i: (b, i, 0)),
                  _const_spec(lb_logits.shape), _const_spec(gain.shape),
                  pl.BlockSpec((1, HG_HEADS, HG_DK, HG_DV), lambda b, i: (b, 0, 0, 0)),
                  _const_spec(tin.shape), _const_spec(tup.shape), _const_spec(ones.shape)],
        out_specs=[pl.BlockSpec((1, th, HG_W), lambda b, i: (b, i, 0)),
                   pl.BlockSpec((1, HG_HEADS, HG_DK, HG_DV), lambda b, i: (b, 0, 0, 0))],
        scratch_shapes=[pltpu.VMEM((HG_HEADS, HG_DV, HG_DK), F32),
                        f32s(), f32s(), f32s(), f32s(), f32s(),
                        b16s(), b16s(), b16s(),
                        pltpu.VMEM((SUB * SUB, HG_W), BF16), f32s()],
        compiler_params=pltpu.CompilerParams(dimension_semantics=("arbitrary", "arbitrary"),
                                             vmem_limit_bytes=VMEM_LIMIT),
        name="hgrn_scan",
    )(raw, lb_logits, gain, s0, tin, tup, ones)


def _attn_body(q_ref, k_ref, v_ref, o_ref, m_scr, l_scr, acc_scr, *, tq):
    qi = pl.program_id(2)
    q = q_ref[0]
    m_scr[...] = jnp.full(m_scr.shape, -jnp.inf, F32)
    l_scr[...] = jnp.zeros(l_scr.shape, F32)
    acc_scr[...] = jnp.zeros(acc_scr.shape, F32)

    def block(ki, masked):
        r0 = pl.multiple_of(ki * tq, tq)
        s = _dot_nt(q, k_ref[0, pl.ds(r0, tq), :])
        if masked:
            rc = lax.broadcasted_iota(jnp.int32, s.shape, 0) // CHUNK
            cc = lax.broadcasted_iota(jnp.int32, s.shape, 1) // CHUNK
            s = jnp.where(cc <= rc, s, -jnp.inf)
        m_prev = m_scr[...]
        m_new = jnp.maximum(m_prev, jnp.max(s, axis=-1, keepdims=True))
        alpha = jnp.exp(m_prev - m_new)
        p = jnp.exp(s - m_new)
        l_scr[...] = alpha * l_scr[...] + jnp.sum(p, axis=-1, keepdims=True)
        acc_scr[...] = alpha * acc_scr[...] + _dot(p.astype(BF16), v_ref[0, pl.ds(r0, tq), :])
        m_scr[...] = m_new

    def body(ki, carry):
        block(ki, False)
        return carry

    lax.fori_loop(0, qi, body, 0)
    block(qi, True)
    o_ref[0] = (acc_scr[...] / l_scr[...]).astype(BF16)


def _attn(q, k, v, tq):
    B, L, _ = q.shape
    return pl.pallas_call(
        functools.partial(_attn_body, tq=tq),
        out_shape=jax.ShapeDtypeStruct((B, L, MLA_HEADS * MLA_V), BF16),
        grid=(B, MLA_HEADS, L // tq),
        in_specs=[pl.BlockSpec((1, tq, QK_W), lambda b, h, i: (b, i, h)),
                  pl.BlockSpec((1, L, QK_W), lambda b, h, i: (b, 0, h)),
                  pl.BlockSpec((1, L, MLA_V), lambda b, h, i: (b, 0, h))],
        out_specs=pl.BlockSpec((1, tq, MLA_V), lambda b, h, i: (b, i, h)),
        scratch_shapes=[pltpu.VMEM((tq, 1), F32), pltpu.VMEM((tq, 1), F32), pltpu.VMEM((tq, MLA_V), F32)],
        compiler_params=pltpu.CompilerParams(dimension_semantics=("arbitrary",) * 3,
                                             vmem_limit_bytes=VMEM_LIMIT),
        name="mla_attn_prompt",
    )(q, k, v)


def _attn_cache_body(q_ref, kn_ref, vn_ref, clat_ref, ckpe_ref, wuk_ref, wuv_ref, place_ref, o_ref, *, tk, nk):
    latc = clat_ref[0].astype(BF16)
    kpec = ckpe_ref[0].astype(BF16)
    for h in range(MLA_HEADS):
        q = q_ref[0, :, QK_W * h:QK_W * (h + 1)]
        ws = slice(MLA_NOPE * h, MLA_NOPE * (h + 1))
        s_new = _dot_nt(q, kn_ref[0, :, QK_W * h:QK_W * (h + 1)])
        s_c = []
        for c in range(nk):
            rs = slice(c * tk, (c + 1) * tk)
            kc = _dot(latc[rs], wuk_ref[:, ws]).astype(BF16)
            kxc = _dot(kpec[rs], place_ref[:, ws]).astype(BF16)
            s_c.append(_dot_nt(q[:, 0:MLA_NOPE], kc) + _dot_nt(q[:, MLA_NOPE:QK_W], kxc))
        m = jnp.max(s_new, axis=-1, keepdims=True)
        for s in s_c:
            m = jnp.maximum(m, jnp.max(s, axis=-1, keepdims=True))
        p_new = jnp.exp(s_new - m)
        l = jnp.sum(p_new, axis=-1, keepdims=True)
        acc = _dot(p_new.astype(BF16), vn_ref[0, :, ws])
        for c in range(nk):
            rs = slice(c * tk, (c + 1) * tk)
            p = jnp.exp(s_c[c] - m)
            l = l + jnp.sum(p, axis=-1, keepdims=True)
            vc = _dot(latc[rs], wuv_ref[:, ws]).astype(BF16)
            acc = acc + _dot(p.astype(BF16), vc)
        o_ref[0, :, ws] = (acc / l).astype(BF16)


def _rope_place():
    p = np.zeros((MLA_ROPE, MLA_HEADS * MLA_NOPE), np.float32)
    for h in range(MLA_HEADS):
        for i in range(ROPE_HALF):
            p[i, MLA_NOPE * h + ROPE_HALF * h + i] = 1.0
            p[ROPE_HALF + i, MLA_NOPE * h + ROPE_HALF * ((h + 1) % MLA_HEADS) + i] = 1.0
    return jnp.asarray(p, BF16)


def _attn_cache(q, kn, vn, clat, ckpe, wuk, wuv):
    B, L, _ = q.shape
    P = clat.shape[1]
    tk = min(P, 1024)
    place = _rope_place()
    bspec = lambda n, w: pl.BlockSpec((1, n, w), lambda b: (b, 0, 0))
    return pl.pallas_call(
        functools.partial(_attn_cache_body, tk=tk, nk=P // tk),
        out_shape=jax.ShapeDtypeStruct((B, L, MLA_HEADS * MLA_V), BF16),
        grid=(B,),
        in_specs=[bspec(L, MLA_HEADS * QK_W), bspec(L, MLA_HEADS * QK_W), bspec(L, MLA_HEADS * MLA_V),
                  bspec(P, KV_LORA), bspec(P, MLA_ROPE),
                  _const_spec(wuk.shape), _const_spec(wuv.shape), _const_spec(place.shape)],
        out_specs=bspec(L, MLA_HEADS * MLA_V),
        compiler_params=pltpu.CompilerParams(dimension_semantics=("arbitrary",),
                                             vmem_limit_bytes=VMEM_LIMIT),
        name="mla_attn_cache",
    )(q, kn, vn, clat, ckpe, wuk, wuv, place)


def _ffn_body(x_ref, ohg_ref, omla_ref, mod_ref, conv0_ref, wout_ref, gffn_ref, wup_ref, cw_ref, cb_ref,
              wdn_ref, gfin_ref, y_ref, conv_out, carry_scr, a_scr, *, tm):
    i = pl.program_id(1)
    last = pl.num_programs(1) - 1

    @pl.when(i == 0)
    def _():
        carry_scr[...] = conv0_ref[0]

    mod = mod_ref[0]
    g1 = mod[2:3]
    sh2 = mod[3:4]
    sc2 = mod[4:5]
    g2 = mod[5:6]
    o = _dot(ohg_ref[0], wout_ref[0:HG_W, :]) + _dot(omla_ref[0], wout_ref[HG_W:2 * HG_W, :])
    x1 = x_ref[0] + g1 * o
    h2 = (_rms(x1, gffn_ref[...]) * (1.0 + sc2) + sh2).astype(BF16)
    acc = jnp.zeros((tm, D_MODEL), F32)
    for j in range(D_FF // FF_TILE):
        cs = slice(j * FF_TILE, (j + 1) * FF_TILE)
        a = _dot(h2, wup_ref[:, cs])
        v = _dot(h2, wup_ref[:, D_FF + j * FF_TILE:D_FF + (j + 1) * FF_TILE])
        a_scr[0:CARRY_ROWS, :] = carry_scr[:, cs]
        a_scr[CARRY_ROWS:CARRY_ROWS + tm, :] = a
        carry_scr[:, cs] = a[tm - CARRY_ROWS:tm]
        cw = cw_ref[:, cs]
        conv = (cb_ref[:, cs] + cw[0:1] * a_scr[CARRY_ROWS - 2:CARRY_ROWS - 2 + tm, :]
                + cw[1:2] * a_scr[CARRY_ROWS - 1:CARRY_ROWS - 1 + tm, :] + cw[2:3] * a)
        g = 0.5 * conv * (1.0 + lax.erf(conv * np.float32(np.sqrt(0.5))))
        acc = acc + _dot((g * v).astype(BF16), wdn_ref[cs, :])

        @pl.when(i == last)
        def _():
            conv_out[0, :, cs] = a[tm - (CONV_W - 1):tm]

    x2 = x1 + g2 * acc
    y_ref[0] = _rms(x2, gfin_ref[...])


def _ffn(x, ohg, omla, mod, conv0, wout, gffn, wup, cw, cb, wdn, gfin, tm):
    B, L, D = x.shape
    row = lambda w: pl.BlockSpec((1, tm, w), lambda b, i: (b, i, 0))
    once = lambda a: pl.BlockSpec(a.shape, lambda b, i: (0,) * a.ndim, pipeline_mode=pl.Buffered(1))
    return pl.pallas_call(
        functools.partial(_ffn_body, tm=tm),
        out_shape=[jax.ShapeDtypeStruct((B, L, D), F32),
                   jax.ShapeDtypeStruct((B, CONV_W - 1, D_FF), F32)],
        grid=(B, L // tm),
        in_specs=[row(D), row(HG_W), row(MLA_HEADS * MLA_V),
                  pl.BlockSpec((1, 6, D), lambda b, i: (b, 0, 0)),
                  pl.BlockSpec((1, CARRY_ROWS, D_FF), lambda b, i: (b, 0, 0)),
                  once(wout), once(gffn), once(wup), once(cw), once(cb), once(wdn), once(gfin)],
        out_specs=[row(D), pl.BlockSpec((1, CONV_W - 1, D_FF), lambda b, i: (b, 0, 0))],
        scratch_shapes=[pltpu.VMEM((CARRY_ROWS, D_FF), F32), pltpu.VMEM((CARRY_ROWS + tm, FF_TILE), F32)],
        compiler_params=pltpu.CompilerParams(dimension_semantics=("arbitrary", "arbitrary"),
                                             vmem_limit_bytes=VMEM_LIMIT),
        name="out_ffn",
    )(x, ohg, omla, mod, conv0, wout, gffn, wup, cw, cb, wdn, gfin)


def _prep_weights(w_in, mla_q_norm_gain, mla_kv_norm_gain, w_uq, w_uk, w_uv):
    hgw = 4 * HG_W
    whg = w_in[:, :hgw].astype(BF16)
    wcq = w_in[:, hgw:hgw + Q_LORA].astype(BF16)
    wckv = w_in[:, hgw + Q_LORA:hgw + Q_LORA + KV_LORA].astype(BF16)
    kp = w_in[:, hgw + Q_LORA + KV_LORA:]
    wkx = jnp.concatenate([jnp.tile(kp[:, :ROPE_HALF], (1, MLA_HEADS)),
                           jnp.tile(kp[:, ROPE_HALF:], (1, MLA_HEADS))], axis=1).astype(BF16)
    uq = w_uq.reshape(Q_LORA, MLA_HEADS, MLA_NOPE + MLA_ROPE)
    wuqn = uq[:, :, :MLA_NOPE].reshape(Q_LORA, MLA_HEADS * MLA_NOPE).astype(BF16)
    x1a = uq[:, :, MLA_NOPE:MLA_NOPE + ROPE_HALF].reshape(Q_LORA, MLA_HEADS * ROPE_HALF)
    x2a = uq[:, :, MLA_NOPE + ROPE_HALF:].reshape(Q_LORA, MLA_HEADS * ROPE_HALF)
    wuqr = jnp.concatenate([x1a, x2a, jnp.roll(x1a, ROPE_HALF, axis=1), jnp.roll(x2a, ROPE_HALF, axis=1)],
                           axis=1).astype(BF16)
    wuk = w_uk.reshape(KV_LORA, MLA_HEADS * MLA_NOPE).astype(BF16)
    wuv = w_uv.reshape(KV_LORA, MLA_HEADS * MLA_V).astype(BF16)
    return (whg, wcq, wckv, wkx, mla_q_norm_gain.reshape(1, -1), mla_kv_norm_gain.reshape(1, -1),
            wuqn, wuqr, wuk, wuv)


def _rope_tables(pos):
    inv_freq = ROPE_THETA ** (-jnp.arange(ROPE_HALF, dtype=F32) / ROPE_HALF)
    ang = pos.astype(F32)[:, None] * inv_freq[None, :]
    return jnp.tile(jnp.cos(ang), (1, MLA_HEADS)), jnp.tile(jnp.sin(ang), (1, MLA_HEADS))


def _tile(n, pref):
    return pref if n % pref == 0 else n


def kernel(x_prompt, x_sample, c_prompt, c_sample, cache_kv_latent, cache_k_rope, state_hgrn, state_ffn_conv, w_ada, b_ada, norm_mix_gain, w_in, hg_lb_logits, hg_norm_gain, mla_q_norm_gain, mla_kv_norm_gain, w_uq, w_uk, w_uv, w_out, norm_ffn_gain, w_up, conv_w, conv_b, w_down, final_norm_gain):
    assert w_ada.shape[0] == 1, "single-layer trunk"
    B, L, D = x_prompt.shape
    Bs, Ls, _ = x_sample.shape
    past = cache_kv_latent.shape[2]

    mod = _ada(jnp.concatenate([c_prompt, c_sample], axis=0), w_ada[0], b_ada)
    mod = mod.reshape(B + Bs, 6, D)
    wts = _prep_weights(w_in[0], mla_q_norm_gain[0], mla_kv_norm_gain[0], w_uq[0], w_uk[0], w_uv[0])
    gmix = norm_mix_gain
    wout = w_out[0].astype(BF16)
    wup = w_up[0].astype(BF16)
    wdn = w_down[0].astype(BF16)
    gfin = final_norm_gain.reshape(1, D)
    ffn_w = (wout, norm_ffn_gain, wup, conv_w[0], conv_b, wdn, gfin)

    def layer(x, mod_x, pos, s0, conv0, cache):
        n, l, _ = x.shape
        cos, sin = _rope_tables(pos)
        raw, q, k, v, lat, kpe = _inproj(x, mod_x, gmix, wts, cos, sin, _tile(l, 256))
        ohg, s_new = _hgrn(raw, hg_lb_logits, hg_norm_gain, s0, _tile(l, 256))
        if cache is None:
            omla = _attn(q, k, v, _tile(l, 512))
        else:
            omla = _attn_cache(q, k, v, cache[0], cache[1], wts[8], wts[9])
        y, conv_new = _ffn(x, ohg, omla, mod_x, conv0, *ffn_w, _tile(l, 512))
        return y, lat[None], kpe[None], s_new[None], conv_new[None]

    zeros_state = jnp.zeros((B, HG_HEADS, HG_DK, HG_DV), F32)
    zeros_conv = jnp.zeros((B, CARRY_ROWS, D_FF), F32)
    conv0_s = jnp.pad(state_ffn_conv[0], ((0, 0), (CARRY_ROWS - (CONV_W - 1), 0), (0, 0)))
    yp, latp, kpep, hgp, cvp = layer(x_prompt, mod[:B], jnp.arange(L), zeros_state, zeros_conv, None)
    ys, lats, kpes, hgs, cvs = layer(x_sample, mod[B:], past + jnp.arange(Ls), state_hgrn[0], conv0_s,
                                     (cache_kv_latent[0], cache_k_rope[0]))
    return (yp, ys, latp, kpep, hgp, cvp, lats, kpes, hgs, cvs)
```

```python
import functools

import numpy as np
import jax
import jax.numpy as jnp
from jax import lax
from jax.experimental import pallas as pl
from jax.experimental.pallas import tpu as pltpu

F32 = jnp.float32
BF16 = jnp.bfloat16

D_MODEL = 1024
CHUNK = 64
HG_HEADS = 4
HG_DK = 128
HG_DV = 128
HG_W = HG_HEADS * HG_DK
MLA_HEADS = 4
MLA_NOPE = 128
MLA_ROPE = 64
ROPE_HALF = MLA_ROPE // 2
MLA_V = 128
Q_LORA = 384
KV_LORA = 256
ROPE_THETA = 10000.0
MLA_SCALE = (MLA_NOPE + MLA_ROPE) ** -0.5
Q_SCALE = MLA_SCALE * float(np.log2(np.e))
QK_W = 2 * MLA_NOPE
D_FF = 2816
CONV_W = 3
EPS = 1e-6

SUB = 16
FF_TILE = D_FF
ATTN_LOOKAHEAD = 2
CARRY_ROWS = 8
VMEM_LIMIT = 56 * 1024 * 1024


def _dot(a, b):
    return jnp.dot(a, b, preferred_element_type=F32)


def _dot_nt(a, b):
    return lax.dot_general(a, b, (((1,), (1,)), ((), ())), preferred_element_type=F32)


def _dot_tn(a, b):
    return lax.dot_general(a, b, (((0,), (0,)), ((), ())), preferred_element_type=F32)


def _split3(x):
    x1 = x.astype(BF16)
    r1 = x - x1.astype(F32)
    x2 = r1.astype(BF16)
    r2 = r1 - x2.astype(F32)
    return x1, x2, r2.astype(BF16)


def _rms(x, g):
    return x * lax.rsqrt(jnp.mean(x * x, axis=-1, keepdims=True) + EPS) * g


def _const_spec(shape):
    n = len(shape)
    return pl.BlockSpec(shape, lambda *_: (0,) * n)


def _ada_body(c_ref, w_ref, b_ref, o_ref):
    c = c_ref[...]
    s = c * jax.nn.sigmoid(c)
    s1, s2, _ = _split3(s)
    w = w_ref[...]
    w1 = w.astype(BF16)
    w2 = (w - w1.astype(F32)).astype(BF16)
    o_ref[...] = _dot(s1, w1) + _dot(s1, w2) + _dot(s2, w1) + b_ref[...]


def _ada(c, w_ada, b_ada):
    n, d = c.shape
    nout = w_ada.shape[1]
    tn = 1024
    return pl.pallas_call(
        _ada_body,
        out_shape=jax.ShapeDtypeStruct((n, nout), F32),
        grid=(nout // tn,),
        in_specs=[pl.BlockSpec((n, d), lambda j: (0, 0)),
                  pl.BlockSpec((d, tn), lambda j: (0, j)),
                  pl.BlockSpec((1, tn), lambda j: (0, j))],
        out_specs=pl.BlockSpec((n, tn), lambda j: (0, j)),
        compiler_params=pltpu.CompilerParams(dimension_semantics=("arbitrary",)),
        name="adaln_mod",
    )(c, w_ada, b_ada)


def _rope_slots(r1, r2, nope, out_ref, slot):
    for h in range(MLA_HEADS):
        xh = jnp.where(slot == h, r1, jnp.where(slot == (h + 1) % MLA_HEADS, r2, 0.0))
        out_ref[0, :, QK_W * h:QK_W * h + MLA_NOPE] = nope[:, MLA_NOPE * h:MLA_NOPE * (h + 1)].astype(BF16)
        out_ref[0, :, QK_W * h + MLA_NOPE:QK_W * (h + 1)] = xh.astype(BF16)


def _inproj_body(x_ref, mod_ref, gmix_ref, whg_ref, wcq_ref, wckv_ref, wkx_ref, qng_ref, kvng_ref,
                 wuqn_ref, wuqr_ref, wuk_ref, wuvt_ref, cos_ref, sin_ref,
                 hg_out, q_out, k_out, v_out, lat_out, kpe_out):
    x = x_ref[0]
    mod = mod_ref[0]
    sh1 = mod[0:1]
    sc1 = mod[1:2]
    h = _rms(x, gmix_ref[...]) * (1.0 + sc1) + sh1
    hb = h.astype(BF16)
    hg_out[0] = _dot(hb, whg_ref[...])

    cos = cos_ref[...]
    sin = sin_ref[...]
    lane = lax.broadcasted_iota(jnp.int32, cos.shape, 1)
    slot = lane // ROPE_HALF

    cq = _rms(_dot(hb, wcq_ref[...]), qng_ref[...]).astype(BF16)
    qn = _dot(cq, wuqn_ref[...]) * Q_SCALE
    qr = _dot(cq, wuqr_ref[...]) * Q_SCALE
    qr1 = qr[:, 0:128] * cos - qr[:, 128:256] * sin
    qr2 = qr[:, 384:512] * cos + qr[:, 256:384] * sin
    _rope_slots(qr1, qr2, qn, q_out, slot)

    lat = _rms(_dot(hb, wckv_ref[...]), kvng_ref[...])
    lat_out[0] = lat
    latb = lat.astype(BF16)
    kn = _dot(latb, wuk_ref[...])
    v_out[0] = _dot_nt(wuvt_ref[...], latb).astype(BF16)
    kx = _dot(hb, wkx_ref[...])
    kr1 = kx[:, 0:128] * cos - kx[:, 128:256] * sin
    kr2 = kx[:, 128:256] * cos + kx[:, 0:128] * sin
    kpe_out[0] = jnp.where(lane < ROPE_HALF, kr1, kr2)[:, 0:MLA_ROPE]
    _rope_slots(kr1, kr2, kn, k_out, slot)


def _inproj(x, mod, gmix, wts, cos, sin, tm):
    B, L, D = x.shape
    (whg, wcq, wckv, wkx, qng, kvng, wuqn, wuqr, wuk, wuv) = wts
    wuvt = wuv.T
    row = lambda w: pl.BlockSpec((1, tm, w), lambda b, i: (b, i, 0))
    in_specs = [row(D), pl.BlockSpec((1, 6, D), lambda b, i: (b, 0, 0)), _const_spec(gmix.shape)]
    in_specs += [_const_spec(w.shape) for w in (whg, wcq, wckv, wkx, qng, kvng, wuqn, wuqr, wuk, wuvt)]
    in_specs += [pl.BlockSpec((tm, 128), lambda b, i: (i, 0))] * 2
    vw = MLA_HEADS * MLA_V
    shapes = [((B, L, 4 * HG_W), F32), ((B, L, MLA_HEADS * QK_W), BF16), ((B, L, MLA_HEADS * QK_W), BF16),
              ((B, vw, L), BF16), ((B, L, KV_LORA), F32), ((B, L, MLA_ROPE), F32)]
    out_specs = [row(s[2]) for s, _ in shapes]
    out_specs[3] = pl.BlockSpec((1, vw, tm), lambda b, i: (b, 0, i))
    return pl.pallas_call(
        _inproj_body,
        out_shape=[jax.ShapeDtypeStruct(s, dt) for s, dt in shapes],
        grid=(B, L // tm),
        in_specs=in_specs,
        out_specs=out_specs,
        compiler_params=pltpu.CompilerParams(dimension_semantics=("arbitrary", "arbitrary"),
                                             vmem_limit_bytes=VMEM_LIMIT),
        name="in_proj",
    )(x, mod, gmix, whg, wcq, wckv, wkx, qng, kvng, wuqn, wuqr, wuk, wuvt, cos, sin)


def _hgrn_body(raw_ref, lbl_ref, gain_ref, s0_ref, tin_ref, tup_ref, ones_ref,
               o_out, s_out,
               st_scr, b_scr, q_scr, kk_scr, v_scr, eb_scr, qt_scr, kt_scr, vb_scr, p_scr, o_scr, *, th):
    i = pl.program_id(1)

    @pl.when(i == 0)
    def _():
        for h in range(HG_HEADS):
            st_scr[h] = s0_ref[0, h].T

    lbl = lbl_ref[...]
    e = jnp.exp(lbl - jnp.max(lbl, axis=0, keepdims=True))
    lb = (e / jnp.sum(e, axis=0, keepdims=True))[0:1]

    q = raw_ref[0, :, 0:HG_W] * HG_DK ** -0.5
    f = lb + (1.0 - lb) * jax.nn.sigmoid(raw_ref[0, :, HG_W:2 * HG_W])
    kk = 1.0 - f
    v = raw_ref[0, :, 2 * HG_W:3 * HG_W]
    l1, l2, l3 = _split3(jnp.log(f))
    tin = tin_ref[...]
    tup = tup_ref[...]
    b = _dot(tin, l1) + _dot(tin, l2) + _dot(tin, l3)
    c = _dot(tup, l1) + _dot(tup, l2) + _dot(tup, l3)
    eb = jnp.exp(b)
    b_scr[...] = b
    q_scr[...] = q
    kk_scr[...] = kk
    v_scr[...] = v
    eb_scr[...] = eb
    qt_scr[...] = (q * eb).astype(BF16)
    kt_scr[...] = (kk * jnp.exp(c)).astype(BF16)
    vb_scr[...] = v.astype(BF16)

    rowid = lax.broadcasted_iota(jnp.int32, (SUB, HG_W), 0)

    def step(j, carry):
        r0 = pl.multiple_of(j * SUB, SUB)
        bj = b_scr[pl.ds(r0, SUB), :]
        qj = q_scr[pl.ds(r0, SUB), :]
        for s in range(SUB):
            bs = b_scr[pl.ds(r0 + s, 1), :]
            ks = kk_scr[pl.ds(r0 + s, 1), :]
            dec = jnp.exp(jnp.where(rowid >= s, bj - bs, -jnp.inf))
            p_scr[s * SUB:(s + 1) * SUB, :] = (qj * dec * ks).astype(BF16)
        r = _dot(p_scr[...], ones_ref[...])
        od = jnp.zeros((SUB, HG_W), F32)
        for s in range(SUB):
            od = od + r[s * SUB:(s + 1) * SUB, :] * v_scr[pl.ds(r0 + s, 1), :]
        qt = qt_scr[pl.ds(r0, SUB), :]
        kt = kt_scr[pl.ds(r0, SUB), :]
        vb = vb_scr[pl.ds(r0, SUB), :]
        dj = eb_scr[pl.ds(r0 + SUB - 1, 1), :]
        for h in range(HG_HEADS):
            hs = slice(h * HG_DK, (h + 1) * HG_DK)
            st = st_scr[h]
            oi = _dot_nt(qt[:, hs], st.astype(BF16))
            st_scr[h] = st * dj[:, hs] + _dot_tn(vb[:, hs], kt[:, hs])
            o_scr[pl.ds(r0, SUB), hs] = oi + od[:, hs]
        return carry

    lax.fori_loop(0, th // SUB, step, 0)

    gate = raw_ref[0, :, 3 * HG_W:4 * HG_W]
    gate = gate * jax.nn.sigmoid(gate)
    gain = gain_ref[...]
    for h in range(HG_HEADS):
        hs = slice(h * HG_DV, (h + 1) * HG_DV)
        o_out[0, :, hs] = (_rms(o_scr[:, hs], gain) * gate[:, hs]).astype(BF16)

    @pl.when(i == pl.num_programs(1) - 1)
    def _():
        for h in range(HG_HEADS):
            s_out[0, h] = st_scr[h].T


def _step_matrices(th):
    r = np.arange(th)
    same = (r[:, None] // SUB) == (r[None, :] // SUB)
    tin = (same & (r[None, :] <= r[:, None])).astype(np.float32)
    tup = (same & (r[None, :] > r[:, None])).astype(np.float32)
    hd = np.arange(HG_W) // HG_DK
    ones = (hd[:, None] == hd[None, :]).astype(np.float32)
    return jnp.asarray(tin, BF16), jnp.asarray(tup, BF16), jnp.asarray(ones, BF16)


def _hgrn(raw, lb_logits, gain, s0, th):
    B, L, _ = raw.shape
    tin, tup, ones = _step_matrices(th)
    f32s = lambda: pltpu.VMEM((th, HG_W), F32)
    b16s = lambda: pltpu.VMEM((th, HG_W), BF16)
    return pl.pallas_call(
        functools.partial(_hgrn_body, th=th),
        out_shape=[jax.ShapeDtypeStruct((B, L, HG_W), BF16),
                   jax.ShapeDtypeStruct((B, HG_HEADS, HG_DK, HG_DV), F32)],
        grid=(B, L // th),
        in_specs=[pl.BlockSpec((1, th, 4 * HG_W), lambda b, i: (b, i, 0)),
                  _const_spec(lb_logits.shape), _const_spec(gain.shape),
                  pl.BlockSpec((1, HG_HEADS, HG_DK, HG_DV), lambda b, i: (b, 0, 0, 0)),
                  _const_spec(tin.shape), _const_spec(tup.shape), _const_spec(ones.shape)],
        out_specs=[pl.BlockSpec((1, th, HG_W), lambda b, i: (b, i, 0)),
                   pl.BlockSpec((1, HG_HEADS, HG_DK, HG_DV), lambda b, i: (b, 0, 0, 0))],
        scratch_shapes=[pltpu.VMEM((HG_HEADS, HG_DV, HG_DK), F32),
                        f32s(), f32s(), f32s(), f32s(), f32s(),
                        b16s(), b16s(), b16s(),
                        pltpu.VMEM((SUB * SUB, HG_W), BF16), f32s()],
        compiler_params=pltpu.CompilerParams(dimension_semantics=("arbitrary", "arbitrary"),
                                             vmem_limit_bytes=VMEM_LIMIT),
        name="hgrn_scan",
    )(raw, lb_logits, gain, s0, tin, tup, ones)


def _attn_body(q_ref, k_ref, vt_ref, o_ref, acc_scr, *, tq, nq):
    qi = pl.program_id(2)
    q = q_ref[0]

    def run(nblk):
        scores = lambda i: _dot_nt(k_ref[0, i * tq:(i + 1) * tq, :], q)
        pending = [scores(i) for i in range(min(ATTN_LOOKAHEAD, nblk))]
        m = jnp.full((1, tq), -jnp.inf, F32)
        l = jnp.zeros((1, tq), F32)
        for ki in range(nblk):
            st = pending.pop(0)
            if ki + ATTN_LOOKAHEAD < nblk:
                pending.append(scores(ki + ATTN_LOOKAHEAD))
            if ki == nblk - 1:
                kc = lax.broadcasted_iota(jnp.int32, st.shape, 0) // CHUNK
                qc = lax.broadcasted_iota(jnp.int32, st.shape, 1) // CHUNK
                st = jnp.where(kc <= qc, st, -jnp.inf)
            m_new = jnp.maximum(m, jnp.max(st, axis=0, keepdims=True))
            p = jnp.exp2(st - m_new)
            alpha = jnp.exp2(m - m_new)
            l = alpha * l + jnp.sum(p, axis=0, keepdims=True)
            pv = _dot(vt_ref[0, :, ki * tq:(ki + 1) * tq], p.astype(BF16))
            if ki == 0:
                acc_scr[...] = pv
            else:
                acc_scr[...] = alpha * acc_scr[...] + pv
            m = m_new
        o_ref[0] = (acc_scr[...] / l).T.astype(BF16)

    for qv in range(nq):
        @pl.when(qi == qv)
        def _():
            run(qv + 1)


def _attn(q, k, vt, tq):
    B, L, _ = q.shape
    return pl.pallas_call(
        functools.partial(_attn_body, tq=tq, nq=L // tq),
        out_shape=jax.ShapeDtypeStruct((B, L, MLA_HEADS * MLA_V), BF16),
        grid=(B, MLA_HEADS, L // tq),
        in_specs=[pl.BlockSpec((1, tq, QK_W), lambda b, h, i: (b, i, h)),
                  pl.BlockSpec((1, L, QK_W), lambda b, h, i: (b, 0, h)),
                  pl.BlockSpec((1, MLA_V, L), lambda b, h, i: (b, h, 0))],
        out_specs=pl.BlockSpec((1, tq, MLA_V), lambda b, h, i: (b, i, h)),
        scratch_shapes=[pltpu.VMEM((MLA_V, tq), F32)],
        compiler_params=pltpu.CompilerParams(dimension_semantics=("arbitrary",) * 3,
                                             vmem_limit_bytes=VMEM_LIMIT),
        name="mla_attn_prompt",
    )(q, k, vt)


def _attn_cache_body(q_ref, kn_ref, vn_ref, clat_ref, ckpe_ref, wuk_ref, wuv_ref, place_ref, o_ref, *, tk, nk):
    latc = clat_ref[0].astype(BF16)
    kpec = ckpe_ref[0].astype(BF16)
    for h in range(MLA_HEADS):
        q = q_ref[0, :, QK_W * h:QK_W * (h + 1)]
        ws = slice(MLA_NOPE * h, MLA_NOPE * (h + 1))
        s_new = _dot_nt(q, kn_ref[0, :, QK_W * h:QK_W * (h + 1)])
        s_c = []
        for c in range(nk):
            rs = slice(c * tk, (c + 1) * tk)
            kc = _dot(latc[rs], wuk_ref[:, ws]).astype(BF16)
            kxc = _dot(kpec[rs], place_ref[:, ws]).astype(BF16)
            s_c.append(_dot_nt(q[:, 0:MLA_NOPE], kc) + _dot_nt(q[:, MLA_NOPE:QK_W], kxc))
        m = jnp.max(s_new, axis=-1, keepdims=True)
        for s in s_c:
            m = jnp.maximum(m, jnp.max(s, axis=-1, keepdims=True))
        p_new = jnp.exp2(s_new - m)
        l = jnp.sum(p_new, axis=-1, keepdims=True)
        acc = _dot_nt(p_new.astype(BF16), vn_ref[0, ws, :])
        for c in range(nk):
            rs = slice(c * tk, (c + 1) * tk)
            p = jnp.exp2(s_c[c] - m)
            l = l + jnp.sum(p, axis=-1, keepdims=True)
            vc = _dot(latc[rs], wuv_ref[:, ws]).astype(BF16)
            acc = acc + _dot(p.astype(BF16), vc)
        o_ref[0, :, ws] = (acc / l).astype(BF16)


def _rope_place():
    p = np.zeros((MLA_ROPE, MLA_HEADS * MLA_NOPE), np.float32)
    for h in range(MLA_HEADS):
        for i in range(ROPE_HALF):
            p[i, MLA_NOPE * h + ROPE_HALF * h + i] = 1.0
            p[ROPE_HALF + i, MLA_NOPE * h + ROPE_HALF * ((h + 1) % MLA_HEADS) + i] = 1.0
    return jnp.asarray(p, BF16)


def _attn_cache(q, kn, vn, clat, ckpe, wuk, wuv):
    B, L, _ = q.shape
    P = clat.shape[1]
    tk = min(P, 1024)
    place = _rope_place()
    bspec = lambda n, w: pl.BlockSpec((1, n, w), lambda b: (b, 0, 0))
    return pl.pallas_call(
        functools.partial(_attn_cache_body, tk=tk, nk=P // tk),
        out_shape=jax.ShapeDtypeStruct((B, L, MLA_HEADS * MLA_V), BF16),
        grid=(B,),
        in_specs=[bspec(L, MLA_HEADS * QK_W), bspec(L, MLA_HEADS * QK_W), bspec(MLA_HEADS * MLA_V, L),
                  bspec(P, KV_LORA), bspec(P, MLA_ROPE),
                  _const_spec(wuk.shape), _const_spec(wuv.shape), _const_spec(place.shape)],
        out_specs=bspec(L, MLA_HEADS * MLA_V),
        compiler_params=pltpu.CompilerParams(dimension_semantics=("arbitrary",),
                                             vmem_limit_bytes=VMEM_LIMIT),
        name="mla_attn_cache",
    )(q, kn, vn, clat, ckpe, wuk, wuv, place)


def _ffn_body(x_ref, ohg_ref, omla_ref, mod_ref, conv0_ref, wout_ref, gffn_ref, wup_ref, cw_ref, cb_ref,
              wdn_ref, gfin_ref, y_ref, conv_out, carry_scr, a_scr, u_scr, *, tm, ft):
    i = pl.program_id(1)
    last = pl.num_programs(1) - 1

    @pl.when(i == 0)
    def _():
        carry_scr[...] = conv0_ref[0]

    mod = mod_ref[0]
    g1 = mod[2:3]
    sh2 = mod[3:4]
    sc2 = mod[4:5]
    g2 = mod[5:6]
    o = _dot(ohg_ref[0], wout_ref[0:HG_W, :]) + _dot(omla_ref[0], wout_ref[HG_W:2 * HG_W, :])
    x1 = x_ref[0] + g1 * o
    h2 = (_rms(x1, gffn_ref[...]) * (1.0 + sc2) + sh2).astype(BF16)
    for j in range(D_FF // ft):
        cs = slice(j * ft, (j + 1) * ft)
        a = _dot(h2, wup_ref[:, cs])
        v = _dot(h2, wup_ref[:, D_FF + j * ft:D_FF + (j + 1) * ft])
        a_scr[0:CARRY_ROWS, :] = carry_scr[:, cs]
        a_scr[CARRY_ROWS:CARRY_ROWS + tm, :] = a
        carry_scr[:, cs] = a[tm - CARRY_ROWS:tm]
        cw = cw_ref[:, cs]
        conv = (cb_ref[:, cs] + cw[0:1] * a_scr[CARRY_ROWS - 2:CARRY_ROWS - 2 + tm, :]
                + cw[1:2] * a_scr[CARRY_ROWS - 1:CARRY_ROWS - 1 + tm, :] + cw[2:3] * a)
        g = 0.5 * conv * (1.0 + lax.erf(conv * np.float32(np.sqrt(0.5))))
        u_scr[:, cs] = (g * v).astype(BF16)

        @pl.when(i == last)
        def _():
            conv_out[0, :, cs] = a[tm - (CONV_W - 1):tm]

    x2 = x1 + g2 * _dot(u_scr[...], wdn_ref[...])
    y_ref[0] = _rms(x2, gfin_ref[...])


def _ffn(x, ohg, omla, mod, conv0, wout, gffn, wup, cw, cb, wdn, gfin, tm, ft):
    B, L, D = x.shape
    row = lambda w: pl.BlockSpec((1, tm, w), lambda b, i: (b, i, 0))
    once = lambda a: pl.BlockSpec(a.shape, lambda b, i: (0,) * a.ndim, pipeline_mode=pl.Buffered(1))
    return pl.pallas_call(
        functools.partial(_ffn_body, tm=tm, ft=ft),
        out_shape=[jax.ShapeDtypeStruct((B, L, D), F32),
                   jax.ShapeDtypeStruct((B, CONV_W - 1, D_FF), F32)],
        grid=(B, L // tm),
        in_specs=[row(D), row(HG_W), row(MLA_HEADS * MLA_V),
                  pl.BlockSpec((1, 6, D), lambda b, i: (b, 0, 0)),
                  pl.BlockSpec((1, CARRY_ROWS, D_FF), lambda b, i: (b, 0, 0)),
                  once(wout), once(gffn), once(wup), once(cw), once(cb), once(wdn), once(gfin)],
        out_specs=[row(D), pl.BlockSpec((1, CONV_W - 1, D_FF), lambda b, i: (b, 0, 0))],
        scratch_shapes=[pltpu.VMEM((CARRY_ROWS, D_FF), F32), pltpu.VMEM((CARRY_ROWS + tm, ft), F32),
                        pltpu.VMEM((tm, D_FF), BF16)],
        compiler_params=pltpu.CompilerParams(dimension_semantics=("arbitrary", "arbitrary"),
                                             vmem_limit_bytes=VMEM_LIMIT),
        name="out_ffn",
    )(x, ohg, omla, mod, conv0, wout, gffn, wup, cw, cb, wdn, gfin)


def _prep_weights(w_in, mla_q_norm_gain, mla_kv_norm_gain, w_uq, w_uk, w_uv):
    hgw = 4 * HG_W
    whg = w_in[:, :hgw].astype(BF16)
    wcq = w_in[:, hgw:hgw + Q_LORA].astype(BF16)
    wckv = w_in[:, hgw + Q_LORA:hgw + Q_LORA + KV_LORA].astype(BF16)
    kp = w_in[:, hgw + Q_LORA + KV_LORA:]
    wkx = jnp.concatenate([jnp.tile(kp[:, :ROPE_HALF], (1, MLA_HEADS)),
                           jnp.tile(kp[:, ROPE_HALF:], (1, MLA_HEADS))], axis=1).astype(BF16)
    uq = w_uq.reshape(Q_LORA, MLA_HEADS, MLA_NOPE + MLA_ROPE)
    wuqn = uq[:, :, :MLA_NOPE].reshape(Q_LORA, MLA_HEADS * MLA_NOPE).astype(BF16)
    x1a = uq[:, :, MLA_NOPE:MLA_NOPE + ROPE_HALF].reshape(Q_LORA, MLA_HEADS * ROPE_HALF)
    x2a = uq[:, :, MLA_NOPE + ROPE_HALF:].reshape(Q_LORA, MLA_HEADS * ROPE_HALF)
    wuqr = jnp.concatenate([x1a, x2a, jnp.roll(x1a, ROPE_HALF, axis=1), jnp.roll(x2a, ROPE_HALF, axis=1)],
                           axis=1).astype(BF16)
    wuk = w_uk.reshape(KV_LORA, MLA_HEADS * MLA_NOPE).astype(BF16)
    wuv = w_uv.reshape(KV_LORA, MLA_HEADS * MLA_V).astype(BF16)
    return (whg, wcq, wckv, wkx, mla_q_norm_gain.reshape(1, -1), mla_kv_norm_gain.reshape(1, -1),
            wuqn, wuqr, wuk, wuv)


def _rope_tables(pos):
    inv_freq = ROPE_THETA ** (-jnp.arange(ROPE_HALF, dtype=F32) / ROPE_HALF)
    ang = pos.astype(F32)[:, None] * inv_freq[None, :]
    return jnp.tile(jnp.cos(ang), (1, MLA_HEADS)), jnp.tile(jnp.sin(ang), (1, MLA_HEADS))


def _tile(n, pref):
    return pref if n % pref == 0 else n


def kernel(x_prompt, x_sample, c_prompt, c_sample, cache_kv_latent, cache_k_rope, state_hgrn, state_ffn_conv, w_ada, b_ada, norm_mix_gain, w_in, hg_lb_logits, hg_norm_gain, mla_q_norm_gain, mla_kv_norm_gain, w_uq, w_uk, w_uv, w_out, norm_ffn_gain, w_up, conv_w, conv_b, w_down, final_norm_gain):
    assert w_ada.shape[0] == 1, "single-layer trunk"
    B, L, D = x_prompt.shape
    Bs, Ls, _ = x_sample.shape
    past = cache_kv_latent.shape[2]

    mod = _ada(jnp.concatenate([c_prompt, c_sample], axis=0), w_ada[0], b_ada)
    mod = mod.reshape(B + Bs, 6, D)
    wts = _prep_weights(w_in[0], mla_q_norm_gain[0], mla_kv_norm_gain[0], w_uq[0], w_uk[0], w_uv[0])
    gmix = norm_mix_gain
    wout = w_out[0].astype(BF16)
    wup = w_up[0].astype(BF16)
    wdn = w_down[0].astype(BF16)
    gfin = final_norm_gain.reshape(1, D)
    ffn_w = (wout, norm_ffn_gain, wup, conv_w[0], conv_b, wdn, gfin)

    def layer(x, mod_x, pos, s0, conv0, cache):
        n, l, _ = x.shape
        cos, sin = _rope_tables(pos)
        raw, q, k, v, lat, kpe = _inproj(x, mod_x, gmix, wts, cos, sin, _tile(l, 256))
        ohg, s_new = _hgrn(raw, hg_lb_logits, hg_norm_gain, s0, _tile(l, 256))
        if cache is None:
            omla = _attn(q, k, v, _tile(l, 512))
        else:
            omla = _attn_cache(q, k, v, cache[0], cache[1], wts[8], wts[9])
        y, conv_new = _ffn(x, ohg, omla, mod_x, conv0, *ffn_w, _tile(l, 512), FF_TILE)
        return y, lat[None], kpe[None], s_new[None], conv_new[None]

    zeros_state = jnp.zeros((B, HG_HEADS, HG_DK, HG_DV), F32)
    zeros_conv = jnp.zeros((B, CARRY_ROWS, D_FF), F32)
    conv0_s = jnp.pad(state_ffn_conv[0], ((0, 0), (CARRY_ROWS - (CONV_W - 1), 0), (0, 0)))
    yp, latp, kpep, hgp, cvp = layer(x_prompt, mod[:B], jnp.arange(L), zeros_state, zeros_conv, None)
    ys, lats, kpes, hgs, cvs = layer(x_sample, mod[B:], past + jnp.arange(Ls), state_hgrn[0], conv0_s,
                                     (cache_kv_latent[0], cache_k_rope[0]))
    return (yp, ys, latp, kpep, hgp, cvp, lats, kpes, hgs, cvs)
```

```python
import functools

import numpy as np
import jax
import jax.numpy as jnp
from jax import lax
from jax.experimental import pallas as pl
from jax.experimental.pallas import tpu as pltpu

F32 = jnp.float32
BF16 = jnp.bfloat16

D_MODEL = 1024
CHUNK = 64
HG_HEADS = 4
HG_DK = 128
HG_DV = 128
HG_W = HG_HEADS * HG_DK
MLA_HEADS = 4
MLA_NOPE = 128
MLA_ROPE = 64
ROPE_HALF = MLA_ROPE // 2
MLA_V = 128
Q_LORA = 384
KV_LORA = 256
ROPE_THETA = 10000.0
MLA_SCALE = (MLA_NOPE + MLA_ROPE) ** -0.5
Q_SCALE = MLA_SCALE * float(np.log2(np.e))
QK_W = 2 * MLA_NOPE
D_FF = 2816
CONV_W = 3
EPS = 1e-6

SUB = 16
FF_TILE = D_FF
ATTN_LOOKAHEAD = 2
CARRY_ROWS = 8
VMEM_LIMIT = 56 * 1024 * 1024


def _dot(a, b):
    return jnp.dot(a, b, preferred_element_type=F32)


def _dot_nt(a, b):
    return lax.dot_general(a, b, (((1,), (1,)), ((), ())), preferred_element_type=F32)


def _dot_tn(a, b):
    return lax.dot_general(a, b, (((0,), (0,)), ((), ())), preferred_element_type=F32)


def _split3(x):
    x1 = x.astype(BF16)
    r1 = x - x1.astype(F32)
    x2 = r1.astype(BF16)
    r2 = r1 - x2.astype(F32)
    return x1, x2, r2.astype(BF16)


def _rms(x, g):
    return x * lax.rsqrt(jnp.mean(x * x, axis=-1, keepdims=True) + EPS) * g


def _const_spec(shape):
    n = len(shape)
    return pl.BlockSpec(shape, lambda *_: (0,) * n)


def _ada_body(c_ref, w_ref, b_ref, o_ref):
    c = c_ref[...]
    s = c * jax.nn.sigmoid(c)
    s1, s2, _ = _split3(s)
    w = w_ref[...]
    w1 = w.astype(BF16)
    w2 = (w - w1.astype(F32)).astype(BF16)
    o_ref[...] = _dot(s1, w1) + _dot(s1, w2) + _dot(s2, w1) + b_ref[...]


def _ada(c, w_ada, b_ada):
    n, d = c.shape
    nout = w_ada.shape[1]
    tn = 1024
    return pl.pallas_call(
        _ada_body,
        out_shape=jax.ShapeDtypeStruct((n, nout), F32),
        grid=(nout // tn,),
        in_specs=[pl.BlockSpec((n, d), lambda j: (0, 0)),
                  pl.BlockSpec((d, tn), lambda j: (0, j)),
                  pl.BlockSpec((1, tn), lambda j: (0, j))],
        out_specs=pl.BlockSpec((n, tn), lambda j: (0, j)),
        compiler_params=pltpu.CompilerParams(dimension_semantics=("arbitrary",)),
        name="adaln_mod",
    )(c, w_ada, b_ada)


def _rope_slots(r1, r2, nope, out_ref, slot):
    for h in range(MLA_HEADS):
        xh = jnp.where(slot == h, r1, jnp.where(slot == (h + 1) % MLA_HEADS, r2, 0.0))
        out_ref[0, :, QK_W * h:QK_W * h + MLA_NOPE] = nope[:, MLA_NOPE * h:MLA_NOPE * (h + 1)].astype(BF16)
        out_ref[0, :, QK_W * h + MLA_NOPE:QK_W * (h + 1)] = xh.astype(BF16)


def _inproj_body(x_ref, mod_ref, gmix_ref, whg_ref, wcq_ref, wckv_ref, wkx_ref, qng_ref, kvng_ref,
                 wuqn_ref, wuqr_ref, wuk_ref, wuvt_ref, cos_ref, sin_ref,
                 hg_out, q_out, k_out, v_out, lat_out, kpe_out):
    x = x_ref[0]
    mod = mod_ref[0]
    sh1 = mod[0:1]
    sc1 = mod[1:2]
    h = _rms(x, gmix_ref[...]) * (1.0 + sc1) + sh1
    hb = h.astype(BF16)
    hg_out[0] = _dot(hb, whg_ref[...])

    cos = cos_ref[...]
    sin = sin_ref[...]
    lane = lax.broadcasted_iota(jnp.int32, cos.shape, 1)
    slot = lane // ROPE_HALF

    cq = _rms(_dot(hb, wcq_ref[...]), qng_ref[...]).astype(BF16)
    qn = _dot(cq, wuqn_ref[...]) * Q_SCALE
    qr = _dot(cq, wuqr_ref[...]) * Q_SCALE
    qr1 = qr[:, 0:128] * cos - qr[:, 128:256] * sin
    qr2 = qr[:, 384:512] * cos + qr[:, 256:384] * sin
    _rope_slots(qr1, qr2, qn, q_out, slot)

    lat = _rms(_dot(hb, wckv_ref[...]), kvng_ref[...])
    lat_out[0] = lat
    latb = lat.astype(BF16)
    kn = _dot(latb, wuk_ref[...])
    v_out[0] = _dot_nt(wuvt_ref[...], latb).astype(BF16)
    kx = _dot(hb, wkx_ref[...])
    kr1 = kx[:, 0:128] * cos - kx[:, 128:256] * sin
    kr2 = kx[:, 128:256] * cos + kx[:, 0:128] * sin
    kpe_out[0] = jnp.where(lane < ROPE_HALF, kr1, kr2)[:, 0:MLA_ROPE]
    _rope_slots(kr1, kr2, kn, k_out, slot)


def _inproj(x, mod, gmix, wts, cos, sin, tm):
    B, L, D = x.shape
    (whg, wcq, wckv, wkx, qng, kvng, wuqn, wuqr, wuk, wuv) = wts
    wuvt = wuv.T
    row = lambda w: pl.BlockSpec((1, tm, w), lambda b, i: (b, i, 0))
    in_specs = [row(D), pl.BlockSpec((1, 6, D), lambda b, i: (b, 0, 0)), _const_spec(gmix.shape)]
    in_specs += [_const_spec(w.shape) for w in (whg, wcq, wckv, wkx, qng, kvng, wuqn, wuqr, wuk, wuvt)]
    in_specs += [pl.BlockSpec((tm, 128), lambda b, i: (i, 0))] * 2
    vw = MLA_HEADS * MLA_V
    shapes = [((B, L, 4 * HG_W), F32), ((B, L, MLA_HEADS * QK_W), BF16), ((B, L, MLA_HEADS * QK_W), BF16),
              ((B, vw, L), BF16), ((B, L, KV_LORA), F32), ((B, L, MLA_ROPE), F32)]
    out_specs = [row(s[2]) for s, _ in shapes]
    out_specs[3] = pl.BlockSpec((1, vw, tm), lambda b, i: (b, 0, i))
    return pl.pallas_call(
        _inproj_body,
        out_shape=[jax.ShapeDtypeStruct(s, dt) for s, dt in shapes],
        grid=(B, L // tm),
        in_specs=in_specs,
        out_specs=out_specs,
        compiler_params=pltpu.CompilerParams(dimension_semantics=("arbitrary", "arbitrary"),
                                             vmem_limit_bytes=VMEM_LIMIT),
        name="in_proj",
    )(x, mod, gmix, whg, wcq, wckv, wkx, qng, kvng, wuqn, wuqr, wuk, wuvt, cos, sin)


def _hgrn_body(raw_ref, lbl_ref, gain_ref, s0_ref, tin_ref, tup_ref, ones_ref,
               o_out, s_out,
               st_scr, b_scr, q_scr, kk_scr, v_scr, eb_scr, qt_scr, kt_scr, vb_scr, p_scr, o_scr, *, th):
    i = pl.program_id(1)

    @pl.when(i == 0)
    def _():
        for h in range(HG_HEADS):
            st_scr[h] = s0_ref[0, h].T

    lbl = lbl_ref[...]
    e = jnp.exp(lbl - jnp.max(lbl, axis=0, keepdims=True))
    lb = (e / jnp.sum(e, axis=0, keepdims=True))[0:1]

    q = raw_ref[0, :, 0:HG_W] * HG_DK ** -0.5
    f = lb + (1.0 - lb) * jax.nn.sigmoid(raw_ref[0, :, HG_W:2 * HG_W])
    kk = 1.0 - f
    v = raw_ref[0, :, 2 * HG_W:3 * HG_W]
    l1, l2, l3 = _split3(jnp.log(f))
    tin = tin_ref[...]
    tup = tup_ref[...]
    b = _dot(tin, l1) + _dot(tin, l2) + _dot(tin, l3)
    c = _dot(tup, l1) + _dot(tup, l2) + _dot(tup, l3)
    eb = jnp.exp(b)
    b_scr[...] = b
    q_scr[...] = q
    kk_scr[...] = kk
    v_scr[...] = v
    eb_scr[...] = eb
    qt_scr[...] = (q * eb).astype(BF16)
    kt_scr[...] = (kk * jnp.exp(c)).astype(BF16)
    vb_scr[...] = v.astype(BF16)

    rowid = lax.broadcasted_iota(jnp.int32, (SUB, HG_W), 0)

    def step(j, carry):
        r0 = pl.multiple_of(j * SUB, SUB)
        bj = b_scr[pl.ds(r0, SUB), :]
        qj = q_scr[pl.ds(r0, SUB), :]
        for s in range(SUB):
            bs = b_scr[pl.ds(r0 + s, 1), :]
            ks = kk_scr[pl.ds(r0 + s, 1), :]
            dec = jnp.exp(jnp.where(rowid >= s, bj - bs, -jnp.inf))
            p_scr[s * SUB:(s + 1) * SUB, :] = (qj * dec * ks).astype(BF16)
        r = _dot(p_scr[...], ones_ref[...])
        od = jnp.zeros((SUB, HG_W), F32)
        for s in range(SUB):
            od = od + r[s * SUB:(s + 1) * SUB, :] * v_scr[pl.ds(r0 + s, 1), :]
        qt = qt_scr[pl.ds(r0, SUB), :]
        kt = kt_scr[pl.ds(r0, SUB), :]
        vb = vb_scr[pl.ds(r0, SUB), :]
        dj = eb_scr[pl.ds(r0 + SUB - 1, 1), :]
        for h in range(HG_HEADS):
            hs = slice(h * HG_DK, (h + 1) * HG_DK)
            st = st_scr[h]
            oi = _dot_nt(qt[:, hs], st.astype(BF16))
            st_scr[h] = st * dj[:, hs] + _dot_tn(vb[:, hs], kt[:, hs])
            o_scr[pl.ds(r0, SUB), hs] = oi + od[:, hs]
        return carry

    lax.fori_loop(0, th // SUB, step, 0)

    gate = raw_ref[0, :, 3 * HG_W:4 * HG_W]
    gate = gate * jax.nn.sigmoid(gate)
    gain = gain_ref[...]
    for h in range(HG_HEADS):
        hs = slice(h * HG_DV, (h + 1) * HG_DV)
        o_out[0, :, hs] = (_rms(o_scr[:, hs], gain) * gate[:, hs]).astype(BF16)

    @pl.when(i == pl.num_programs(1) - 1)
    def _():
        for h in range(HG_HEADS):
            s_out[0, h] = st_scr[h].T


def _step_matrices(th):
    r = np.arange(th)
    same = (r[:, None] // SUB) == (r[None, :] // SUB)
    tin = (same & (r[None, :] <= r[:, None])).astype(np.float32)
    tup = (same & (r[None, :] > r[:, None])).astype(np.float32)
    hd = np.arange(HG_W) // HG_DK
    ones = (hd[:, None] == hd[None, :]).astype(np.float32)
    return jnp.asarray(tin, BF16), jnp.asarray(tup, BF16), jnp.asarray(ones, BF16)


def _hgrn(raw, lb_logits, gain, s0, th):
    B, L, _ = raw.shape
    tin, tup, ones = _step_matrices(th)
    f32s = lambda: pltpu.VMEM((th, HG_W), F32)
    b16s = lambda: pltpu.VMEM((th, HG_W), BF16)
    return pl.pallas_call(
        functools.partial(_hgrn_body, th=th),
        out_shape=[jax.ShapeDtypeStruct((B, L, HG_W), BF16),
                   jax.ShapeDtypeStruct((B, HG_HEADS, HG_DK, HG_DV), F32)],
        grid=(B, L // th),
        in_specs=[pl.BlockSpec((1, th, 4 * HG_W), lambda b, i: (b, i, 0)),
                  _const_spec(lb_logits.shape), _const_spec(gain.shape),
                  pl.BlockSpec((1, HG_HEADS, HG_DK, HG_DV), lambda b, i: (b, 0, 0, 0)),
                  _const_spec(tin.shape), _const_spec(tup.shape), _const_spec(ones.shape)],
        out_specs=[pl.BlockSpec((1, th, HG_W), lambda b, i: (b, i, 0)),
                   pl.BlockSpec((1, HG_HEADS, HG_DK, HG_DV), lambda b, i: (b, 0, 0, 0))],
        scratch_shapes=[pltpu.VMEM((HG_HEADS, HG_DV, HG_DK), F32),
                        f32s(), f32s(), f32s(), f32s(), f32s(),
                        b16s(), b16s(), b16s(),
                        pltpu.VMEM((SUB * SUB, HG_W), BF16), f32s()],
        compiler_params=pltpu.CompilerParams(dimension_semantics=("arbitrary", "arbitrary"),
                                             vmem_limit_bytes=VMEM_LIMIT),
        name="hgrn_scan",
    )(raw, lb_logits, gain, s0, tin, tup, ones)


UNIT = 128
SLABS = 8
SLAB_ROWS = UNIT // SLABS
N_LEVELS = 7
DIAG_LEVEL = N_LEVELS


def _unit_time():
    i = np.arange(UNIT)
    return SLABS * (i % SLAB_ROWS) + i // SLAB_ROWS


def _unit_tables():
    t = _unit_time()
    tin = (t[None, :] <= t[:, None]).astype(np.float32)
    rows = []
    for c in (1, 2, 4, 8):
        g = np.arange(SLAB_ROWS)
        t_bnd = SLABS * (2 * c * (g // (2 * c)) + c - 1) + SLABS - 1
        rows.append((t[None, :] <= t_bnd[:, None]).astype(np.float32))
    tin_ext = np.concatenate([tin] + rows, axis=0)
    x = t[:, None] ^ t[None, :]
    lv = np.where(x > 0, np.floor(np.log2(np.maximum(x, 1))), DIAG_LEVEL).astype(np.int32)
    lv = np.where(t[None, :] <= t[:, None], lv, -1)
    return jnp.asarray(tin_ext, BF16), jnp.asarray(np.concatenate([lv, lv], axis=1), jnp.int32)


def _pair_blockdiag(a, b):
    z = jnp.zeros_like(a)
    return jnp.concatenate([jnp.concatenate([a, z], axis=1), jnp.concatenate([z, b], axis=1)], axis=0)


def _hgrn_units_body(raw_ref, lbl_ref, gain_ref, s0_ref, tin_ref, lv_ref, o_out, s_out, st_scr, *, nu):
    i = pl.program_id(1)

    @pl.when(i == 0)
    def _():
        for h in range(HG_HEADS):
            st_scr[h] = s0_ref[0, h].T

    lbl = lbl_ref[...]
    e = jnp.exp(lbl - jnp.max(lbl, axis=0, keepdims=True))
    lb = (e / jnp.sum(e, axis=0, keepdims=True))[0:1]
    tin = tin_ref[...]
    lv = lv_ref[...]
    gain = gain_ref[...]
    raw_w = 4 * HG_W

    units = range(nu)
    pairs = range(HG_HEADS // 2)

    def field(u, k):
        return jnp.concatenate([raw_ref[0, u, :, raw_w * p + HG_W * k:raw_w * p + HG_W * (k + 1)]
                                for p in range(SLABS)], axis=0)

    def pair_cols(x, hp):
        c0 = 2 * HG_DK * hp
        return x[:, c0:c0 + HG_DK], x[:, c0 + HG_DK:c0 + 2 * HG_DK]

    q = [field(u, 0) * HG_DK ** -0.5 for u in units]
    f = [lb + (1.0 - lb) * jax.nn.sigmoid(field(u, 1)) for u in units]
    kk = [1.0 - f[u] for u in units]
    lsplit = [_split3(jnp.log(f[u]) * np.float32(1.0 / np.log(2.0))) for u in units]
    bx = [_dot(tin, lsplit[u][0]) + _dot(tin, lsplit[u][1]) + _dot(tin, lsplit[u][2]) for u in units]
    b = [bx[u][0:UNIT] for u in units]
    bs = [[b[u][SLAB_ROWS * p:SLAB_ROWS * (p + 1)] for p in range(SLABS)] for u in units]
    vb = [field(u, 2).astype(BF16) for u in units]
    qb = [q[u].astype(BF16) for u in units]
    kkb = [kk[u].astype(BF16) for u in units]

    def slab_decay(u, level, p):
        if level < 3:
            half = 1 << level
            ref = p - p % (2 * half) + half - 1
            if p == ref:
                return jnp.ones_like(bs[u][p])
            return jnp.exp2(bs[u][p] - bs[u][ref] if p > ref else bs[u][ref] - bs[u][p])
        r = bx[u][UNIT + SLAB_ROWS * (level - 3):UNIT + SLAB_ROWS * (level - 2)]
        if level == N_LEVELS - 1:
            h = SLAB_ROWS // 2
            return jnp.concatenate([jnp.exp2(r[:h] - bs[u][p][:h]), jnp.exp2(bs[u][p][h:] - r[h:])], axis=0)
        return jnp.exp2(-jnp.abs(bs[u][p] - r))

    acc = [[jnp.zeros((UNIT, 2 * HG_DK), F32) for _ in pairs] for _ in units]
    for level in range(N_LEVELS + 1):
        here = lv == level
        for u in units:
            if level == DIAG_LEVEL:
                qt, kt = qb[u], kkb[u]
            else:
                dec = jnp.concatenate([slab_decay(u, level, p) for p in range(SLABS)], axis=0).astype(BF16)
                qt, kt = qb[u] * dec, kkb[u] * dec
            for hp in pairs:
                a2 = _dot_nt(qt[:, 2 * HG_DK * hp:2 * HG_DK * (hp + 1)], _pair_blockdiag(*pair_cols(kt, hp)))
                acc[u][hp] = jnp.where(here, a2, acc[u][hp])

    b_end = [b[u][UNIT - 1:UNIT] for u in units]
    qi = [(q[u] * jnp.exp2(b[u])).astype(BF16) for u in units]
    ks = [(kk[u] * jnp.exp2(b_end[u] - b[u])).astype(BF16) for u in units]
    d = [jnp.exp2(b_end[u]) for u in units]
    o_intra = [[_dot(acc[u][hp].astype(BF16), _pair_blockdiag(*pair_cols(vb[u], hp))) for hp in pairs]
               for u in units]
    ds = [[_dot_tn(vb[u][:, h * HG_DV:(h + 1) * HG_DV], ks[u][:, h * HG_DK:(h + 1) * HG_DK])
           for h in range(HG_HEADS)] for u in units]
    st = [st_scr[h] for h in range(HG_HEADS)]
    for u in units:
        o = jnp.concatenate(
            [o_intra[u][hp] + _dot_nt(qi[u][:, 2 * HG_DK * hp:2 * HG_DK * (hp + 1)],
                                      _pair_blockdiag(st[2 * hp].astype(BF16), st[2 * hp + 1].astype(BF16)))
             for hp in pairs], axis=1)
        st = [st[h] * d[u][:, h * HG_DK:(h + 1) * HG_DK] + ds[u][h] for h in range(HG_HEADS)]
        gate = field(u, 3)
        gate = gate * jax.nn.sigmoid(gate)
        res = jnp.concatenate([_rms(o[:, h * HG_DV:(h + 1) * HG_DV], gain) for h in range(HG_HEADS)], axis=1)
        res = (res * gate).astype(BF16)
        for p in range(SLABS):
            o_out[0, u, :, HG_W * p:HG_W * (p + 1)] = res[SLAB_ROWS * p:SLAB_ROWS * (p + 1)]
    for h in range(HG_HEADS):
        st_scr[h] = st[h]

    @pl.when(i == pl.num_programs(1) - 1)
    def _():
        for h in range(HG_HEADS):
            s_out[0, h] = st_scr[h].T


def _hgrn_units(raw, lb_logits, gain, s0, nu):
    B, L, W = raw.shape
    n_units = L // UNIT
    raw4 = raw.reshape(B, n_units, SLAB_ROWS, SLABS * W)
    tin, lv = _unit_tables()
    o4, s_new = pl.pallas_call(
        functools.partial(_hgrn_units_body, nu=nu),
        out_shape=[jax.ShapeDtypeStruct((B, n_units, SLAB_ROWS, SLABS * HG_W), BF16),
                   jax.ShapeDtypeStruct((B, HG_HEADS, HG_DK, HG_DV), F32)],
        grid=(B, n_units // nu),
        in_specs=[pl.BlockSpec((1, nu, SLAB_ROWS, SLABS * W), lambda b, i: (b, i, 0, 0)),
                  _const_spec(lb_logits.shape), _const_spec(gain.shape),
                  pl.BlockSpec((1, HG_HEADS, HG_DK, HG_DV), lambda b, i: (b, 0, 0, 0)),
                  _const_spec(tin.shape), _const_spec(lv.shape)],
        out_specs=[pl.BlockSpec((1, nu, SLAB_ROWS, SLABS * HG_W), lambda b, i: (b, i, 0, 0)),
                   pl.BlockSpec((1, HG_HEADS, HG_DK, HG_DV), lambda b, i: (b, 0, 0, 0))],
        scratch_shapes=[pltpu.VMEM((HG_HEADS, HG_DV, HG_DK), F32)],
        compiler_params=pltpu.CompilerParams(dimension_semantics=("arbitrary", "arbitrary"),
                                             vmem_limit_bytes=VMEM_LIMIT),
        name="hgrn_units",
    )(raw4, lb_logits, gain, s0, tin, lv)
    return o4.reshape(B, L, HG_W), s_new


def _attn_body(q_ref, k_ref, vt_ref, o_ref, acc_scr, *, tq, nq):
    qi = pl.program_id(2)
    q = q_ref[0]

    def run(nblk):
        scores = lambda i: _dot_nt(k_ref[0, i * tq:(i + 1) * tq, :], q)
        pending = [scores(i) for i in range(min(ATTN_LOOKAHEAD, nblk))]
        m = jnp.full((1, tq), -jnp.inf, F32)
        l = jnp.zeros((1, tq), F32)
        for ki in range(nblk):
            st = pending.pop(0)
            if ki + ATTN_LOOKAHEAD < nblk:
                pending.append(scores(ki + ATTN_LOOKAHEAD))
            if ki == nblk - 1:
                kc = lax.broadcasted_iota(jnp.int32, st.shape, 0) // CHUNK
                qc = lax.broadcasted_iota(jnp.int32, st.shape, 1) // CHUNK
                st = jnp.where(kc <= qc, st, -jnp.inf)
            m_new = jnp.maximum(m, jnp.max(st, axis=0, keepdims=True))
            p = jnp.exp2(st - m_new)
            alpha = jnp.exp2(m - m_new)
            l = alpha * l + jnp.sum(p, axis=0, keepdims=True)
            pv = _dot(vt_ref[0, :, ki * tq:(ki + 1) * tq], p.astype(BF16))
            if ki == 0:
                acc_scr[...] = pv
            else:
                acc_scr[...] = alpha * acc_scr[...] + pv
            m = m_new
        o_ref[0] = (acc_scr[...] / l).T.astype(BF16)

    for qv in range(nq):
        @pl.when(qi == qv)
        def _():
            run(qv + 1)


def _attn(q, k, vt, tq):
    B, L, _ = q.shape
    return pl.pallas_call(
        functools.partial(_attn_body, tq=tq, nq=L // tq),
        out_shape=jax.ShapeDtypeStruct((B, L, MLA_HEADS * MLA_V), BF16),
        grid=(B, MLA_HEADS, L // tq),
        in_specs=[pl.BlockSpec((1, tq, QK_W), lambda b, h, i: (b, i, h)),
                  pl.BlockSpec((1, L, QK_W), lambda b, h, i: (b, 0, h)),
                  pl.BlockSpec((1, MLA_V, L), lambda b, h, i: (b, h, 0))],
        out_specs=pl.BlockSpec((1, tq, MLA_V), lambda b, h, i: (b, i, h)),
        scratch_shapes=[pltpu.VMEM((MLA_V, tq), F32)],
        compiler_params=pltpu.CompilerParams(dimension_semantics=("arbitrary",) * 3,
                                             vmem_limit_bytes=VMEM_LIMIT),
        name="mla_attn_prompt",
    )(q, k, vt)


def _attn_cache_body(q_ref, kn_ref, vn_ref, clat_ref, ckpe_ref, wuk_ref, wuv_ref, place_ref, o_ref, *, tk, nk):
    latc = clat_ref[0].astype(BF16)
    kpec = ckpe_ref[0].astype(BF16)
    for h in range(MLA_HEADS):
        q = q_ref[0, :, QK_W * h:QK_W * (h + 1)]
        ws = slice(MLA_NOPE * h, MLA_NOPE * (h + 1))
        s_new = _dot_nt(q, kn_ref[0, :, QK_W * h:QK_W * (h + 1)])
        s_c = []
        for c in range(nk):
            rs = slice(c * tk, (c + 1) * tk)
            kc = _dot(latc[rs], wuk_ref[:, ws]).astype(BF16)
            kxc = _dot(kpec[rs], place_ref[:, ws]).astype(BF16)
            s_c.append(_dot_nt(q[:, 0:MLA_NOPE], kc) + _dot_nt(q[:, MLA_NOPE:QK_W], kxc))
        m = jnp.max(s_new, axis=-1, keepdims=True)
        for s in s_c:
            m = jnp.maximum(m, jnp.max(s, axis=-1, keepdims=True))
        p_new = jnp.exp2(s_new - m)
        l = jnp.sum(p_new, axis=-1, keepdims=True)
        acc = _dot_nt(p_new.astype(BF16), vn_ref[0, ws, :])
        for c in range(nk):
            rs = slice(c * tk, (c + 1) * tk)
            p = jnp.exp2(s_c[c] - m)
            l = l + jnp.sum(p, axis=-1, keepdims=True)
            vc = _dot(latc[rs], wuv_ref[:, ws]).astype(BF16)
            acc = acc + _dot(p.astype(BF16), vc)
        o_ref[0, :, ws] = (acc / l).astype(BF16)


def _rope_place():
    p = np.zeros((MLA_ROPE, MLA_HEADS * MLA_NOPE), np.float32)
    for h in range(MLA_HEADS):
        for i in range(ROPE_HALF):
            p[i, MLA_NOPE * h + ROPE_HALF * h + i] = 1.0
            p[ROPE_HALF + i, MLA_NOPE * h + ROPE_HALF * ((h + 1) % MLA_HEADS) + i] = 1.0
    return jnp.asarray(p, BF16)


def _attn_cache(q, kn, vn, clat, ckpe, wuk, wuv):
    B, L, _ = q.shape
    P = clat.shape[1]
    tk = min(P, 1024)
    place = _rope_place()
    bspec = lambda n, w: pl.BlockSpec((1, n, w), lambda b: (b, 0, 0))
    return pl.pallas_call(
        functools.partial(_attn_cache_body, tk=tk, nk=P // tk),
        out_shape=jax.ShapeDtypeStruct((B, L, MLA_HEADS * MLA_V), BF16),
        grid=(B,),
        in_specs=[bspec(L, MLA_HEADS * QK_W), bspec(L, MLA_HEADS * QK_W), bspec(MLA_HEADS * MLA_V, L),
                  bspec(P, KV_LORA), bspec(P, MLA_ROPE),
                  _const_spec(wuk.shape), _const_spec(wuv.shape), _const_spec(place.shape)],
        out_specs=bspec(L, MLA_HEADS * MLA_V),
        compiler_params=pltpu.CompilerParams(dimension_semantics=("arbitrary",),
                                             vmem_limit_bytes=VMEM_LIMIT),
        name="mla_attn_cache",
    )(q, kn, vn, clat, ckpe, wuk, wuv, place)


def _ffn_body(x_ref, ohg_ref, omla_ref, mod_ref, conv0_ref, wout_ref, gffn_ref, wup_ref, cw_ref, cb_ref,
              wdn_ref, gfin_ref, y_ref, conv_out, carry_scr, a_scr, u_scr, *, tm, ft):
    i = pl.program_id(1)
    last = pl.num_programs(1) - 1

    @pl.when(i == 0)
    def _():
        carry_scr[...] = conv0_ref[0]

    mod = mod_ref[0]
    g1 = mod[2:3]
    sh2 = mod[3:4]
    sc2 = mod[4:5]
    g2 = mod[5:6]
    o = _dot(ohg_ref[0], wout_ref[0:HG_W, :]) + _dot(omla_ref[0], wout_ref[HG_W:2 * HG_W, :])
    x1 = x_ref[0] + g1 * o
    h2 = (_rms(x1, gffn_ref[...]) * (1.0 + sc2) + sh2).astype(BF16)
    for j in range(D_FF // ft):
        cs = slice(j * ft, (j + 1) * ft)
        a = _dot(h2, wup_ref[:, cs])
        v = _dot(h2, wup_ref[:, D_FF + j * ft:D_FF + (j + 1) * ft])
        a_scr[0:CARRY_ROWS, :] = carry_scr[:, cs]
        a_scr[CARRY_ROWS:CARRY_ROWS + tm, :] = a
        carry_scr[:, cs] = a[tm - CARRY_ROWS:tm]
        cw = cw_ref[:, cs]
        conv = (cb_ref[:, cs] + cw[0:1] * a_scr[CARRY_ROWS - 2:CARRY_ROWS - 2 + tm, :]
                + cw[1:2] * a_scr[CARRY_ROWS - 1:CARRY_ROWS - 1 + tm, :] + cw[2:3] * a)
        g = 0.5 * conv * (1.0 + lax.erf(conv * np.float32(np.sqrt(0.5))))
        u_scr[:, cs] = (g * v).astype(BF16)

        @pl.when(i == last)
        def _():
            conv_out[0, :, cs] = a[tm - (CONV_W - 1):tm]

    x2 = x1 + g2 * _dot(u_scr[...], wdn_ref[...])
    y_ref[0] = _rms(x2, gfin_ref[...])


def _ffn(x, ohg, omla, mod, conv0, wout, gffn, wup, cw, cb, wdn, gfin, tm, ft):
    B, L, D = x.shape
    row = lambda w: pl.BlockSpec((1, tm, w), lambda b, i: (b, i, 0))
    once = lambda a: pl.BlockSpec(a.shape, lambda b, i: (0,) * a.ndim, pipeline_mode=pl.Buffered(1))
    return pl.pallas_call(
        functools.partial(_ffn_body, tm=tm, ft=ft),
        out_shape=[jax.ShapeDtypeStruct((B, L, D), F32),
                   jax.ShapeDtypeStruct((B, CONV_W - 1, D_FF), F32)],
        grid=(B, L // tm),
        in_specs=[row(D), row(HG_W), row(MLA_HEADS * MLA_V),
                  pl.BlockSpec((1, 6, D), lambda b, i: (b, 0, 0)),
                  pl.BlockSpec((1, CARRY_ROWS, D_FF), lambda b, i: (b, 0, 0)),
                  once(wout), once(gffn), once(wup), once(cw), once(cb), once(wdn), once(gfin)],
        out_specs=[row(D), pl.BlockSpec((1, CONV_W - 1, D_FF), lambda b, i: (b, 0, 0))],
        scratch_shapes=[pltpu.VMEM((CARRY_ROWS, D_FF), F32), pltpu.VMEM((CARRY_ROWS + tm, ft), F32),
                        pltpu.VMEM((tm, D_FF), BF16)],
        compiler_params=pltpu.CompilerParams(dimension_semantics=("arbitrary", "arbitrary"),
                                             vmem_limit_bytes=VMEM_LIMIT),
        name="out_ffn",
    )(x, ohg, omla, mod, conv0, wout, gffn, wup, cw, cb, wdn, gfin)


def _prep_weights(w_in, mla_q_norm_gain, mla_kv_norm_gain, w_uq, w_uk, w_uv):
    hgw = 4 * HG_W
    whg = w_in[:, :hgw].astype(BF16)
    wcq = w_in[:, hgw:hgw + Q_LORA].astype(BF16)
    wckv = w_in[:, hgw + Q_LORA:hgw + Q_LORA + KV_LORA].astype(BF16)
    kp = w_in[:, hgw + Q_LORA + KV_LORA:]
    wkx = jnp.concatenate([jnp.tile(kp[:, :ROPE_HALF], (1, MLA_HEADS)),
                           jnp.tile(kp[:, ROPE_HALF:], (1, MLA_HEADS))], axis=1).astype(BF16)
    uq = w_uq.reshape(Q_LORA, MLA_HEADS, MLA_NOPE + MLA_ROPE)
    wuqn = uq[:, :, :MLA_NOPE].reshape(Q_LORA, MLA_HEADS * MLA_NOPE).astype(BF16)
    x1a = uq[:, :, MLA_NOPE:MLA_NOPE + ROPE_HALF].reshape(Q_LORA, MLA_HEADS * ROPE_HALF)
    x2a = uq[:, :, MLA_NOPE + ROPE_HALF:].reshape(Q_LORA, MLA_HEADS * ROPE_HALF)
    wuqr = jnp.concatenate([x1a, x2a, jnp.roll(x1a, ROPE_HALF, axis=1), jnp.roll(x2a, ROPE_HALF, axis=1)],
                           axis=1).astype(BF16)
    wuk = w_uk.reshape(KV_LORA, MLA_HEADS * MLA_NOPE).astype(BF16)
    wuv = w_uv.reshape(KV_LORA, MLA_HEADS * MLA_V).astype(BF16)
    return (whg, wcq, wckv, wkx, mla_q_norm_gain.reshape(1, -1), mla_kv_norm_gain.reshape(1, -1),
            wuqn, wuqr, wuk, wuv)


def _rope_tables(pos):
    inv_freq = ROPE_THETA ** (-jnp.arange(ROPE_HALF, dtype=F32) / ROPE_HALF)
    ang = pos.astype(F32)[:, None] * inv_freq[None, :]
    return jnp.tile(jnp.cos(ang), (1, MLA_HEADS)), jnp.tile(jnp.sin(ang), (1, MLA_HEADS))


def _tile(n, pref):
    return pref if n % pref == 0 else n


def kernel(x_prompt, x_sample, c_prompt, c_sample, cache_kv_latent, cache_k_rope, state_hgrn, state_ffn_conv, w_ada, b_ada, norm_mix_gain, w_in, hg_lb_logits, hg_norm_gain, mla_q_norm_gain, mla_kv_norm_gain, w_uq, w_uk, w_uv, w_out, norm_ffn_gain, w_up, conv_w, conv_b, w_down, final_norm_gain):
    assert w_ada.shape[0] == 1, "single-layer trunk"
    B, L, D = x_prompt.shape
    Bs, Ls, _ = x_sample.shape
    past = cache_kv_latent.shape[2]

    mod = _ada(jnp.concatenate([c_prompt, c_sample], axis=0), w_ada[0], b_ada)
    mod = mod.reshape(B + Bs, 6, D)
    wts = _prep_weights(w_in[0], mla_q_norm_gain[0], mla_kv_norm_gain[0], w_uq[0], w_uk[0], w_uv[0])
    gmix = norm_mix_gain
    wout = w_out[0].astype(BF16)
    wup = w_up[0].astype(BF16)
    wdn = w_down[0].astype(BF16)
    gfin = final_norm_gain.reshape(1, D)
    ffn_w = (wout, norm_ffn_gain, wup, conv_w[0], conv_b, wdn, gfin)

    def layer(x, mod_x, pos, s0, conv0, cache):
        n, l, _ = x.shape
        cos, sin = _rope_tables(pos)
        raw, q, k, v, lat, kpe = _inproj(x, mod_x, gmix, wts, cos, sin, _tile(l, 256))
        if l % UNIT == 0:
            nu = max(n for n in (4, 2, 1) if (l // UNIT) % n == 0)
            ohg, s_new = _hgrn_units(raw, hg_lb_logits, hg_norm_gain, s0, nu)
        else:
            ohg, s_new = _hgrn(raw, hg_lb_logits, hg_norm_gain, s0, _tile(l, 256))
        if cache is None:
            omla = _attn(q, k, v, _tile(l, 512))
        else:
            omla = _attn_cache(q, k, v, cache[0], cache[1], wts[8], wts[9])
        y, conv_new = _ffn(x, ohg, omla, mod_x, conv0, *ffn_w, _tile(l, 512), FF_TILE)
        return y, lat[None], kpe[None], s_new[None], conv_new[None]

    zeros_state = jnp.zeros((B, HG_HEADS, HG_DK, HG_DV), F32)
    zeros_conv = jnp.zeros((B, CARRY_ROWS, D_FF), F32)
    conv0_s = jnp.pad(state_ffn_conv[0], ((0, 0), (CARRY_ROWS - (CONV_W - 1), 0), (0, 0)))
    yp, latp, kpep, hgp, cvp = layer(x_prompt, mod[:B], jnp.arange(L), zeros_state, zeros_conv, None)
    ys, lats, kpes, hgs, cvs = layer(x_sample, mod[B:], past + jnp.arange(Ls), state_hgrn[0], conv0_s,
                                     (cache_kv_latent[0], cache_k_rope[0]))
    return (yp, ys, latp, kpep, hgp, cvp, lats, kpes, hgs, cvs)
```

```python
import functools

import numpy as np
import jax
import jax.numpy as jnp
from jax import lax
from jax.experimental import pallas as pl
from jax.experimental.pallas import tpu as pltpu

F32 = jnp.float32
BF16 = jnp.bfloat16

D_MODEL = 1024
CHUNK = 64
HG_HEADS = 4
HG_DK = 128
HG_DV = 128
HG_W = HG_HEADS * HG_DK
MLA_HEADS = 4
MLA_NOPE = 128
MLA_ROPE = 64
ROPE_HALF = MLA_ROPE // 2
MLA_V = 128
Q_LORA = 384
KV_LORA = 256
ROPE_THETA = 10000.0
MLA_SCALE = (MLA_NOPE + MLA_ROPE) ** -0.5
Q_SCALE = MLA_SCALE * float(np.log2(np.e))
QK_W = 2 * MLA_NOPE
D_FF = 2816
CONV_W = 3
EPS = 1e-6

SUB = 16
FF_TILE = D_FF
ATTN_LOOKAHEAD = 2
CARRY_ROWS = 8
VMEM_LIMIT = 56 * 1024 * 1024
UNIT = 128
SLABS = 8
SLAB_ROWS = UNIT // SLABS
N_LEVELS = 7
DIAG_LEVEL = N_LEVELS


def _dot(a, b):
    return jnp.dot(a, b, preferred_element_type=F32)


def _dot_nt(a, b):
    return lax.dot_general(a, b, (((1,), (1,)), ((), ())), preferred_element_type=F32)


def _dot_tn(a, b):
    return lax.dot_general(a, b, (((0,), (0,)), ((), ())), preferred_element_type=F32)


def _split3(x):
    x1 = x.astype(BF16)
    r1 = x - x1.astype(F32)
    x2 = r1.astype(BF16)
    r2 = r1 - x2.astype(F32)
    return x1, x2, r2.astype(BF16)


def _rms(x, g):
    return x * lax.rsqrt(jnp.mean(x * x, axis=-1, keepdims=True) + EPS) * g


def _const_spec(shape):
    n = len(shape)
    return pl.BlockSpec(shape, lambda *_: (0,) * n)


def _ada_body(c_ref, w_ref, b_ref, o_ref):
    c = c_ref[...]
    s = c * jax.nn.sigmoid(c)
    s1, s2, _ = _split3(s)
    w = w_ref[...]
    w1 = w.astype(BF16)
    w2 = (w - w1.astype(F32)).astype(BF16)
    o_ref[...] = _dot(s1, w1) + _dot(s1, w2) + _dot(s2, w1) + b_ref[...]


def _ada(c, w_ada, b_ada):
    n, d = c.shape
    nout = w_ada.shape[1]
    tn = 1024
    return pl.pallas_call(
        _ada_body,
        out_shape=jax.ShapeDtypeStruct((n, nout), F32),
        grid=(nout // tn,),
        in_specs=[pl.BlockSpec((n, d), lambda j: (0, 0)),
                  pl.BlockSpec((d, tn), lambda j: (0, j)),
                  pl.BlockSpec((1, tn), lambda j: (0, j))],
        out_specs=pl.BlockSpec((n, tn), lambda j: (0, j)),
        compiler_params=pltpu.CompilerParams(dimension_semantics=("arbitrary",)),
        name="adaln_mod",
    )(c, w_ada, b_ada)


def _rope_slots(r1, r2, nope, out_ref, slot):
    for h in range(MLA_HEADS):
        xh = jnp.where(slot == h, r1, jnp.where(slot == (h + 1) % MLA_HEADS, r2, 0.0))
        out_ref[0, :, QK_W * h:QK_W * h + MLA_NOPE] = nope[:, MLA_NOPE * h:MLA_NOPE * (h + 1)].astype(BF16)
        out_ref[0, :, QK_W * h + MLA_NOPE:QK_W * (h + 1)] = xh.astype(BF16)


def _inproj_body(x_ref, mod_ref, gmix_ref, perm_ref, whg_ref, wcq_ref, wckv_ref, wkx_ref, qng_ref, kvng_ref,
                 wuqn_ref, wuqr_ref, wuk_ref, wuvt_ref, cos_ref, sin_ref,
                 hg_out, q_out, k_out, v_out, lat_out, kpe_out, *, slab):
    x = x_ref[0]
    mod = mod_ref[0]
    sh1 = mod[0:1]
    sc1 = mod[1:2]
    h = _rms(x, gmix_ref[...]) * (1.0 + sc1) + sh1
    hb = h.astype(BF16)
    if slab:
        raw = _dot(_dot(perm_ref[...], hb).astype(BF16), whg_ref[...])
        raw_w = raw.shape[1]
        for u in range(raw.shape[0] // UNIT):
            for p in range(SLABS):
                r0 = UNIT * u + SLAB_ROWS * p
                hg_out[0, u, :, raw_w * p:raw_w * (p + 1)] = raw[r0:r0 + SLAB_ROWS]
    else:
        hg_out[0] = _dot(hb, whg_ref[...])

    cos = cos_ref[...]
    sin = sin_ref[...]
    lane = lax.broadcasted_iota(jnp.int32, cos.shape, 1)
    slot = lane // ROPE_HALF

    cq = _rms(_dot(hb, wcq_ref[...]), qng_ref[...]).astype(BF16)
    qn = _dot(cq, wuqn_ref[...]) * Q_SCALE
    qr = _dot(cq, wuqr_ref[...]) * Q_SCALE
    qr1 = qr[:, 0:128] * cos - qr[:, 128:256] * sin
    qr2 = qr[:, 384:512] * cos + qr[:, 256:384] * sin
    _rope_slots(qr1, qr2, qn, q_out, slot)

    lat = _rms(_dot(hb, wckv_ref[...]), kvng_ref[...])
    lat_out[0] = lat
    latb = lat.astype(BF16)
    kn = _dot(latb, wuk_ref[...])
    v_out[0] = _dot_nt(wuvt_ref[...], latb).astype(BF16)
    kx = _dot(hb, wkx_ref[...])
    kr1 = kx[:, 0:128] * cos - kx[:, 128:256] * sin
    kr2 = kx[:, 128:256] * cos + kx[:, 0:128] * sin
    kpe_out[0] = jnp.where(lane < ROPE_HALF, kr1, kr2)[:, 0:MLA_ROPE]
    _rope_slots(kr1, kr2, kn, k_out, slot)


def _unit_perm(n):
    i = np.arange(n)
    src = UNIT * (i // UNIT) + _unit_time()[i % UNIT]
    return jnp.asarray(src[:, None] == i[None, :], BF16)


def _inproj(x, mod, gmix, wts, cos, sin, tm, slab):
    B, L, D = x.shape
    (whg, wcq, wckv, wkx, qng, kvng, wuqn, wuqr, wuk, wuv) = wts
    wuvt = wuv.T
    perm = _unit_perm(tm) if slab else jnp.zeros((8, 128), BF16)
    row = lambda w: pl.BlockSpec((1, tm, w), lambda b, i: (b, i, 0))
    in_specs = [row(D), pl.BlockSpec((1, 6, D), lambda b, i: (b, 0, 0)), _const_spec(gmix.shape)]
    in_specs += [_const_spec(w.shape) for w in (perm, whg, wcq, wckv, wkx, qng, kvng, wuqn, wuqr, wuk, wuvt)]
    in_specs += [pl.BlockSpec((tm, 128), lambda b, i: (i, 0))] * 2
    vw = MLA_HEADS * MLA_V
    shapes = [((B, L, 4 * HG_W), F32), ((B, L, MLA_HEADS * QK_W), BF16), ((B, L, MLA_HEADS * QK_W), BF16),
              ((B, vw, L), BF16), ((B, L, KV_LORA), F32), ((B, L, MLA_ROPE), F32)]
    out_specs = [row(s[2]) for s, _ in shapes]
    out_specs[3] = pl.BlockSpec((1, vw, tm), lambda b, i: (b, 0, i))
    if slab:
        shapes[0] = ((B, L // UNIT, SLAB_ROWS, SLABS * 4 * HG_W), F32)
        out_specs[0] = pl.BlockSpec((1, tm // UNIT, SLAB_ROWS, SLABS * 4 * HG_W), lambda b, i: (b, i, 0, 0))
    return pl.pallas_call(
        functools.partial(_inproj_body, slab=slab),
        out_shape=[jax.ShapeDtypeStruct(s, dt) for s, dt in shapes],
        grid=(B, L // tm),
        in_specs=in_specs,
        out_specs=out_specs,
        compiler_params=pltpu.CompilerParams(dimension_semantics=("arbitrary", "arbitrary"),
                                             vmem_limit_bytes=VMEM_LIMIT),
        name="in_proj",
    )(x, mod, gmix, perm, whg, wcq, wckv, wkx, qng, kvng, wuqn, wuqr, wuk, wuvt, cos, sin)


def _hgrn_body(raw_ref, lbl_ref, gain_ref, s0_ref, tin_ref, tup_ref, ones_ref,
               o_out, s_out,
               st_scr, b_scr, q_scr, kk_scr, v_scr, eb_scr, qt_scr, kt_scr, vb_scr, p_scr, o_scr, *, th):
    i = pl.program_id(1)

    @pl.when(i == 0)
    def _():
        for h in range(HG_HEADS):
            st_scr[h] = s0_ref[0, h].T

    lbl = lbl_ref[...]
    e = jnp.exp(lbl - jnp.max(lbl, axis=0, keepdims=True))
    lb = (e / jnp.sum(e, axis=0, keepdims=True))[0:1]

    q = raw_ref[0, :, 0:HG_W] * HG_DK ** -0.5
    f = lb + (1.0 - lb) * jax.nn.sigmoid(raw_ref[0, :, HG_W:2 * HG_W])
    kk = 1.0 - f
    v = raw_ref[0, :, 2 * HG_W:3 * HG_W]
    l1, l2, l3 = _split3(jnp.log(f))
    tin = tin_ref[...]
    tup = tup_ref[...]
    b = _dot(tin, l1) + _dot(tin, l2) + _dot(tin, l3)
    c = _dot(tup, l1) + _dot(tup, l2) + _dot(tup, l3)
    eb = jnp.exp(b)
    b_scr[...] = b
    q_scr[...] = q
    kk_scr[...] = kk
    v_scr[...] = v
    eb_scr[...] = eb
    qt_scr[...] = (q * eb).astype(BF16)
    kt_scr[...] = (kk * jnp.exp(c)).astype(BF16)
    vb_scr[...] = v.astype(BF16)

    rowid = lax.broadcasted_iota(jnp.int32, (SUB, HG_W), 0)

    def step(j, carry):
        r0 = pl.multiple_of(j * SUB, SUB)
        bj = b_scr[pl.ds(r0, SUB), :]
        qj = q_scr[pl.ds(r0, SUB), :]
        for s in range(SUB):
            bs = b_scr[pl.ds(r0 + s, 1), :]
            ks = kk_scr[pl.ds(r0 + s, 1), :]
            dec = jnp.exp(jnp.where(rowid >= s, bj - bs, -jnp.inf))
            p_scr[s * SUB:(s + 1) * SUB, :] = (qj * dec * ks).astype(BF16)
        r = _dot(p_scr[...], ones_ref[...])
        od = jnp.zeros((SUB, HG_W), F32)
        for s in range(SUB):
            od = od + r[s * SUB:(s + 1) * SUB, :] * v_scr[pl.ds(r0 + s, 1), :]
        qt = qt_scr[pl.ds(r0, SUB), :]
        kt = kt_scr[pl.ds(r0, SUB), :]
        vb = vb_scr[pl.ds(r0, SUB), :]
        dj = eb_scr[pl.ds(r0 + SUB - 1, 1), :]
        for h in range(HG_HEADS):
            hs = slice(h * HG_DK, (h + 1) * HG_DK)
            st = st_scr[h]
            oi = _dot_nt(qt[:, hs], st.astype(BF16))
            st_scr[h] = st * dj[:, hs] + _dot_tn(vb[:, hs], kt[:, hs])
            o_scr[pl.ds(r0, SUB), hs] = oi + od[:, hs]
        return carry

    lax.fori_loop(0, th // SUB, step, 0)

    gate = raw_ref[0, :, 3 * HG_W:4 * HG_W]
    gate = gate * jax.nn.sigmoid(gate)
    gain = gain_ref[...]
    for h in range(HG_HEADS):
        hs = slice(h * HG_DV, (h + 1) * HG_DV)
        o_out[0, :, hs] = (_rms(o_scr[:, hs], gain) * gate[:, hs]).astype(BF16)

    @pl.when(i == pl.num_programs(1) - 1)
    def _():
        for h in range(HG_HEADS):
            s_out[0, h] = st_scr[h].T


def _step_matrices(th):
    r = np.arange(th)
    same = (r[:, None] // SUB) == (r[None, :] // SUB)
    tin = (same & (r[None, :] <= r[:, None])).astype(np.float32)
    tup = (same & (r[None, :] > r[:, None])).astype(np.float32)
    hd = np.arange(HG_W) // HG_DK
    ones = (hd[:, None] == hd[None, :]).astype(np.float32)
    return jnp.asarray(tin, BF16), jnp.asarray(tup, BF16), jnp.asarray(ones, BF16)


def _hgrn(raw, lb_logits, gain, s0, th):
    B, L, _ = raw.shape
    tin, tup, ones = _step_matrices(th)
    f32s = lambda: pltpu.VMEM((th, HG_W), F32)
    b16s = lambda: pltpu.VMEM((th, HG_W), BF16)
    return pl.pallas_call(
        functools.partial(_hgrn_body, th=th),
        out_shape=[jax.ShapeDtypeStruct((B, L, HG_W), BF16),
                   jax.ShapeDtypeStruct((B, HG_HEADS, HG_DK, HG_DV), F32)],
        grid=(B, L // th),
        in_specs=[pl.BlockSpec((1, th, 4 * HG_W), lambda b, i: (b, i, 0)),
                  _const_spec(lb_logits.shape), _const_spec(gain.shape),
                  pl.BlockSpec((1, HG_HEADS, HG_DK, HG_DV), lambda b, i: (b, 0, 0, 0)),
                  _const_spec(tin.shape), _const_spec(tup.shape), _const_spec(ones.shape)],
        out_specs=[pl.BlockSpec((1, th, HG_W), lambda b, i: (b, i, 0)),
                   pl.BlockSpec((1, HG_HEADS, HG_DK, HG_DV), lambda b, i: (b, 0, 0, 0))],
        scratch_shapes=[pltpu.VMEM((HG_HEADS, HG_DV, HG_DK), F32),
                        f32s(), f32s(), f32s(), f32s(), f32s(),
                        b16s(), b16s(), b16s(),
                        pltpu.VMEM((SUB * SUB, HG_W), BF16), f32s()],
        compiler_params=pltpu.CompilerParams(dimension_semantics=("arbitrary", "arbitrary"),
                                             vmem_limit_bytes=VMEM_LIMIT),
        name="hgrn_scan",
    )(raw, lb_logits, gain, s0, tin, tup, ones)


def _unit_time():
    i = np.arange(UNIT)
    return SLABS * (i % SLAB_ROWS) + i // SLAB_ROWS


def _unit_tables():
    t = _unit_time()
    tin = (t[None, :] <= t[:, None]).astype(np.float32)
    rows = []
    for c in (1, 2, 4, 8):
        g = np.arange(SLAB_ROWS)
        t_bnd = SLABS * (2 * c * (g // (2 * c)) + c - 1) + SLABS - 1
        rows.append((t[None, :] <= t_bnd[:, None]).astype(np.float32))
    tin_ext = np.concatenate([tin] + rows, axis=0)
    x = t[:, None] ^ t[None, :]
    lv = np.where(x > 0, np.floor(np.log2(np.maximum(x, 1))), DIAG_LEVEL).astype(np.int32)
    lv = np.where(t[None, :] <= t[:, None], lv, -1)
    return jnp.asarray(tin_ext, BF16), jnp.asarray(np.concatenate([lv, lv], axis=1), jnp.int32)


def _pair_blockdiag(a, b):
    z = jnp.zeros_like(a)
    return jnp.concatenate([jnp.concatenate([a, z], axis=1), jnp.concatenate([z, b], axis=1)], axis=0)


def _hgrn_units_body(raw_ref, lbl_ref, gain_ref, s0_ref, tin_ref, lv_ref, unperm_ref, o_out, s_out, st_scr, *, nu):
    i = pl.program_id(1)

    @pl.when(i == 0)
    def _():
        for h in range(HG_HEADS):
            st_scr[h] = s0_ref[0, h].T

    lbl = lbl_ref[...]
    e = jnp.exp(lbl - jnp.max(lbl, axis=0, keepdims=True))
    lb = (e / jnp.sum(e, axis=0, keepdims=True))[0:1]
    tin = tin_ref[...]
    lv = lv_ref[...]
    gain = gain_ref[...]
    raw_w = 4 * HG_W

    units = range(nu)
    pairs = range(HG_HEADS // 2)

    def field(u, k):
        return jnp.concatenate([raw_ref[0, u, :, raw_w * p + HG_W * k:raw_w * p + HG_W * (k + 1)]
                                for p in range(SLABS)], axis=0)

    def pair_cols(x, hp):
        c0 = 2 * HG_DK * hp
        return x[:, c0:c0 + HG_DK], x[:, c0 + HG_DK:c0 + 2 * HG_DK]

    q = [field(u, 0) * HG_DK ** -0.5 for u in units]
    f = [lb + (1.0 - lb) * jax.nn.sigmoid(field(u, 1)) for u in units]
    kk = [1.0 - f[u] for u in units]
    lsplit = [_split3(jnp.log(f[u]) * np.float32(1.0 / np.log(2.0))) for u in units]
    bx = [_dot(tin, lsplit[u][0]) + _dot(tin, lsplit[u][1]) + _dot(tin, lsplit[u][2]) for u in units]
    b = [bx[u][0:UNIT] for u in units]
    bs = [[b[u][SLAB_ROWS * p:SLAB_ROWS * (p + 1)] for p in range(SLABS)] for u in units]
    vb = [field(u, 2).astype(BF16) for u in units]
    qb = [q[u].astype(BF16) for u in units]
    kkb = [kk[u].astype(BF16) for u in units]

    def slab_decay(u, level, p):
        if level < 3:
            half = 1 << level
            ref = p - p % (2 * half) + half - 1
            if p == ref:
                return jnp.ones_like(bs[u][p])
            return jnp.exp2(bs[u][p] - bs[u][ref] if p > ref else bs[u][ref] - bs[u][p])
        r = bx[u][UNIT + SLAB_ROWS * (level - 3):UNIT + SLAB_ROWS * (level - 2)]
        if level == N_LEVELS - 1:
            h = SLAB_ROWS // 2
            return jnp.concatenate([jnp.exp2(r[:h] - bs[u][p][:h]), jnp.exp2(bs[u][p][h:] - r[h:])], axis=0)
        return jnp.exp2(-jnp.abs(bs[u][p] - r))

    acc = [[jnp.zeros((UNIT, 2 * HG_DK), F32) for _ in pairs] for _ in units]
    for level in range(N_LEVELS + 1):
        here = lv == level
        for u in units:
            if level == DIAG_LEVEL:
                qt, kt = qb[u], kkb[u]
            else:
                dec = jnp.concatenate([slab_decay(u, level, p) for p in range(SLABS)], axis=0).astype(BF16)
                qt, kt = qb[u] * dec, kkb[u] * dec
            for hp in pairs:
                a2 = _dot_nt(qt[:, 2 * HG_DK * hp:2 * HG_DK * (hp + 1)], _pair_blockdiag(*pair_cols(kt, hp)))
                acc[u][hp] = jnp.where(here, a2, acc[u][hp])

    b_end = [b[u][UNIT - 1:UNIT] for u in units]
    qi = [(q[u] * jnp.exp2(b[u])).astype(BF16) for u in units]
    ks = [(kk[u] * jnp.exp2(b_end[u] - b[u])).astype(BF16) for u in units]
    d = [jnp.exp2(b_end[u]) for u in units]
    o_intra = [[_dot(acc[u][hp].astype(BF16), _pair_blockdiag(*pair_cols(vb[u], hp))) for hp in pairs]
               for u in units]
    ds = [[_dot_tn(vb[u][:, h * HG_DV:(h + 1) * HG_DV], ks[u][:, h * HG_DK:(h + 1) * HG_DK])
           for h in range(HG_HEADS)] for u in units]
    st = [st_scr[h] for h in range(HG_HEADS)]
    for u in units:
        o = jnp.concatenate(
            [o_intra[u][hp] + _dot_nt(qi[u][:, 2 * HG_DK * hp:2 * HG_DK * (hp + 1)],
                                      _pair_blockdiag(st[2 * hp].astype(BF16), st[2 * hp + 1].astype(BF16)))
             for hp in pairs], axis=1)
        st = [st[h] * d[u][:, h * HG_DK:(h + 1) * HG_DK] + ds[u][h] for h in range(HG_HEADS)]
        gate = field(u, 3)
        gate = gate * jax.nn.sigmoid(gate)
        res = jnp.concatenate([_rms(o[:, h * HG_DV:(h + 1) * HG_DV], gain) for h in range(HG_HEADS)], axis=1)
        res = (res * gate).astype(BF16)
        o_out[0, UNIT * u:UNIT * (u + 1), :] = _dot(unperm_ref[...], res).astype(BF16)
    for h in range(HG_HEADS):
        st_scr[h] = st[h]

    @pl.when(i == pl.num_programs(1) - 1)
    def _():
        for h in range(HG_HEADS):
            s_out[0, h] = st_scr[h].T


def _hgrn_units(raw4, lb_logits, gain, s0, nu):
    B, n_units, _, w8 = raw4.shape
    L = n_units * UNIT
    tin, lv = _unit_tables()
    unperm = _unit_perm(UNIT).T
    return pl.pallas_call(
        functools.partial(_hgrn_units_body, nu=nu),
        out_shape=[jax.ShapeDtypeStruct((B, L, HG_W), BF16),
                   jax.ShapeDtypeStruct((B, HG_HEADS, HG_DK, HG_DV), F32)],
        grid=(B, n_units // nu),
        in_specs=[pl.BlockSpec((1, nu, SLAB_ROWS, w8), lambda b, i: (b, i, 0, 0)),
                  _const_spec(lb_logits.shape), _const_spec(gain.shape),
                  pl.BlockSpec((1, HG_HEADS, HG_DK, HG_DV), lambda b, i: (b, 0, 0, 0)),
                  _const_spec(tin.shape), _const_spec(lv.shape), _const_spec(unperm.shape)],
        out_specs=[pl.BlockSpec((1, nu * UNIT, HG_W), lambda b, i: (b, i, 0)),
                   pl.BlockSpec((1, HG_HEADS, HG_DK, HG_DV), lambda b, i: (b, 0, 0, 0))],
        scratch_shapes=[pltpu.VMEM((HG_HEADS, HG_DV, HG_DK), F32)],
        compiler_params=pltpu.CompilerParams(dimension_semantics=("arbitrary", "arbitrary"),
                                             vmem_limit_bytes=VMEM_LIMIT),
        name="hgrn_units",
    )(raw4, lb_logits, gain, s0, tin, lv, unperm)


def _attn_body(q_ref, k_ref, vt_ref, o_ref, acc_scr, *, tq, nq):
    qi = pl.program_id(2)
    q = q_ref[0]

    def run(nblk):
        scores = lambda i: _dot_nt(k_ref[0, i * tq:(i + 1) * tq, :], q)
        pending = [scores(i) for i in range(min(ATTN_LOOKAHEAD, nblk))]
        m = jnp.full((1, tq), -jnp.inf, F32)
        l = jnp.zeros((1, tq), F32)
        for ki in range(nblk):
            st = pending.pop(0)
            if ki + ATTN_LOOKAHEAD < nblk:
                pending.append(scores(ki + ATTN_LOOKAHEAD))
            if ki == nblk - 1:
                kc = lax.broadcasted_iota(jnp.int32, st.shape, 0) // CHUNK
                qc = lax.broadcasted_iota(jnp.int32, st.shape, 1) // CHUNK
                st = jnp.where(kc <= qc, st, -jnp.inf)
            m_new = jnp.maximum(m, jnp.max(st, axis=0, keepdims=True))
            p = jnp.exp2(st - m_new)
            alpha = jnp.exp2(m - m_new)
            l = alpha * l + jnp.sum(p, axis=0, keepdims=True)
            pv = _dot(vt_ref[0, :, ki * tq:(ki + 1) * tq], p.astype(BF16))
            if ki == 0:
                acc_scr[...] = pv
            else:
                acc_scr[...] = alpha * acc_scr[...] + pv
            m = m_new
        o_ref[0] = (acc_scr[...] / l).T.astype(BF16)

    for qv in range(nq):
        @pl.when(qi == qv)
        def _():
            run(qv + 1)


def _attn(q, k, vt, tq):
    B, L, _ = q.shape
    return pl.pallas_call(
        functools.partial(_attn_body, tq=tq, nq=L // tq),
        out_shape=jax.ShapeDtypeStruct((B, L, MLA_HEADS * MLA_V), BF16),
        grid=(B, MLA_HEADS, L // tq),
        in_specs=[pl.BlockSpec((1, tq, QK_W), lambda b, h, i: (b, i, h)),
                  pl.BlockSpec((1, L, QK_W), lambda b, h, i: (b, 0, h)),
                  pl.BlockSpec((1, MLA_V, L), lambda b, h, i: (b, h, 0))],
        out_specs=pl.BlockSpec((1, tq, MLA_V), lambda b, h, i: (b, i, h)),
        scratch_shapes=[pltpu.VMEM((MLA_V, tq), F32)],
        compiler_params=pltpu.CompilerParams(dimension_semantics=("arbitrary",) * 3,
                                             vmem_limit_bytes=VMEM_LIMIT),
        name="mla_attn_prompt",
    )(q, k, vt)


def _attn_cache_body(q_ref, kn_ref, vn_ref, clat_ref, ckpe_ref, wuk_ref, wuv_ref, place_ref, o_ref, *, tk, nk):
    latc = clat_ref[0].astype(BF16)
    kpec = ckpe_ref[0].astype(BF16)
    for h in range(MLA_HEADS):
        q = q_ref[0, :, QK_W * h:QK_W * (h + 1)]
        ws = slice(MLA_NOPE * h, MLA_NOPE * (h + 1))
        s_new = _dot_nt(q, kn_ref[0, :, QK_W * h:QK_W * (h + 1)])
        s_c = []
        for c in range(nk):
            rs = slice(c * tk, (c + 1) * tk)
            kc = _dot(latc[rs], wuk_ref[:, ws]).astype(BF16)
            kxc = _dot(kpec[rs], place_ref[:, ws]).astype(BF16)
            s_c.append(_dot_nt(q[:, 0:MLA_NOPE], kc) + _dot_nt(q[:, MLA_NOPE:QK_W], kxc))
        m = jnp.max(s_new, axis=-1, keepdims=True)
        for s in s_c:
            m = jnp.maximum(m, jnp.max(s, axis=-1, keepdims=True))
        p_new = jnp.exp2(s_new - m)
        l = jnp.sum(p_new, axis=-1, keepdims=True)
        acc = _dot_nt(p_new.astype(BF16), vn_ref[0, ws, :])
        for c in range(nk):
            rs = slice(c * tk, (c + 1) * tk)
            p = jnp.exp2(s_c[c] - m)
            l = l + jnp.sum(p, axis=-1, keepdims=True)
            vc = _dot(latc[rs], wuv_ref[:, ws]).astype(BF16)
            acc = acc + _dot(p.astype(BF16), vc)
        o_ref[0, :, ws] = (acc / l).astype(BF16)


def _rope_place():
    p = np.zeros((MLA_ROPE, MLA_HEADS * MLA_NOPE), np.float32)
    for h in range(MLA_HEADS):
        for i in range(ROPE_HALF):
            p[i, MLA_NOPE * h + ROPE_HALF * h + i] = 1.0
            p[ROPE_HALF + i, MLA_NOPE * h + ROPE_HALF * ((h + 1) % MLA_HEADS) + i] = 1.0
    return jnp.asarray(p, BF16)


def _attn_cache(q, kn, vn, clat, ckpe, wuk, wuv):
    B, L, _ = q.shape
    P = clat.shape[1]
    tk = min(P, 1024)
    place = _rope_place()
    bspec = lambda n, w: pl.BlockSpec((1, n, w), lambda b: (b, 0, 0))
    return pl.pallas_call(
        functools.partial(_attn_cache_body, tk=tk, nk=P // tk),
        out_shape=jax.ShapeDtypeStruct((B, L, MLA_HEADS * MLA_V), BF16),
        grid=(B,),
        in_specs=[bspec(L, MLA_HEADS * QK_W), bspec(L, MLA_HEADS * QK_W), bspec(MLA_HEADS * MLA_V, L),
                  bspec(P, KV_LORA), bspec(P, MLA_ROPE),
                  _const_spec(wuk.shape), _const_spec(wuv.shape), _const_spec(place.shape)],
        out_specs=bspec(L, MLA_HEADS * MLA_V),
        compiler_params=pltpu.CompilerParams(dimension_semantics=("arbitrary",),
                                             vmem_limit_bytes=VMEM_LIMIT),
        name="mla_attn_cache",
    )(q, kn, vn, clat, ckpe, wuk, wuv, place)


def _ffn_body(x_ref, ohg_ref, omla_ref, mod_ref, conv0_ref, wout_ref, gffn_ref, wup_ref, cw_ref, cb_ref,
              wdn_ref, gfin_ref, y_ref, conv_out, carry_scr, a_scr, u_scr, *, tm, ft):
    i = pl.program_id(1)
    last = pl.num_programs(1) - 1

    @pl.when(i == 0)
    def _():
        carry_scr[...] = conv0_ref[0]

    mod = mod_ref[0]
    g1 = mod[2:3]
    sh2 = mod[3:4]
    sc2 = mod[4:5]
    g2 = mod[5:6]
    o = _dot(ohg_ref[0], wout_ref[0:HG_W, :]) + _dot(omla_ref[0], wout_ref[HG_W:2 * HG_W, :])
    x1 = x_ref[0] + g1 * o
    h2 = (_rms(x1, gffn_ref[...]) * (1.0 + sc2) + sh2).astype(BF16)
    for j in range(D_FF // ft):
        cs = slice(j * ft, (j + 1) * ft)
        a = _dot(h2, wup_ref[:, cs])
        v = _dot(h2, wup_ref[:, D_FF + j * ft:D_FF + (j + 1) * ft])
        a_scr[0:CARRY_ROWS, :] = carry_scr[:, cs]
        a_scr[CARRY_ROWS:CARRY_ROWS + tm, :] = a
        carry_scr[:, cs] = a[tm - CARRY_ROWS:tm]
        cw = cw_ref[:, cs]
        conv = (cb_ref[:, cs] + cw[0:1] * a_scr[CARRY_ROWS - 2:CARRY_ROWS - 2 + tm, :]
                + cw[1:2] * a_scr[CARRY_ROWS - 1:CARRY_ROWS - 1 + tm, :] + cw[2:3] * a)
        g = 0.5 * conv * (1.0 + lax.erf(conv * np.float32(np.sqrt(0.5))))
        u_scr[:, cs] = (g * v).astype(BF16)

        @pl.when(i == last)
        def _():
            conv_out[0, :, cs] = a[tm - (CONV_W - 1):tm]

    x2 = x1 + g2 * _dot(u_scr[...], wdn_ref[...])
    y_ref[0] = _rms(x2, gfin_ref[...])


def _ffn(x, ohg, omla, mod, conv0, wout, gffn, wup, cw, cb, wdn, gfin, tm, ft):
    B, L, D = x.shape
    row = lambda w: pl.BlockSpec((1, tm, w), lambda b, i: (b, i, 0))
    once = lambda a: pl.BlockSpec(a.shape, lambda b, i: (0,) * a.ndim, pipeline_mode=pl.Buffered(1))
    return pl.pallas_call(
        functools.partial(_ffn_body, tm=tm, ft=ft),
        out_shape=[jax.ShapeDtypeStruct((B, L, D), F32),
                   jax.ShapeDtypeStruct((B, CONV_W - 1, D_FF), F32)],
        grid=(B, L // tm),
        in_specs=[row(D), row(HG_W), row(MLA_HEADS * MLA_V),
                  pl.BlockSpec((1, 6, D), lambda b, i: (b, 0, 0)),
                  pl.BlockSpec((1, CARRY_ROWS, D_FF), lambda b, i: (b, 0, 0)),
                  once(wout), once(gffn), once(wup), once(cw), once(cb), once(wdn), once(gfin)],
        out_specs=[row(D), pl.BlockSpec((1, CONV_W - 1, D_FF), lambda b, i: (b, 0, 0))],
        scratch_shapes=[pltpu.VMEM((CARRY_ROWS, D_FF), F32), pltpu.VMEM((CARRY_ROWS + tm, ft), F32),
                        pltpu.VMEM((tm, D_FF), BF16)],
        compiler_params=pltpu.CompilerParams(dimension_semantics=("arbitrary", "arbitrary"),
                                             vmem_limit_bytes=VMEM_LIMIT),
        name="out_ffn",
    )(x, ohg, omla, mod, conv0, wout, gffn, wup, cw, cb, wdn, gfin)


def _prep_weights(w_in, mla_q_norm_gain, mla_kv_norm_gain, w_uq, w_uk, w_uv):
    hgw = 4 * HG_W
    whg = w_in[:, :hgw].astype(BF16)
    wcq = w_in[:, hgw:hgw + Q_LORA].astype(BF16)
    wckv = w_in[:, hgw + Q_LORA:hgw + Q_LORA + KV_LORA].astype(BF16)
    kp = w_in[:, hgw + Q_LORA + KV_LORA:]
    wkx = jnp.concatenate([jnp.tile(kp[:, :ROPE_HALF], (1, MLA_HEADS)),
                           jnp.tile(kp[:, ROPE_HALF:], (1, MLA_HEADS))], axis=1).astype(BF16)
    uq = w_uq.reshape(Q_LORA, MLA_HEADS, MLA_NOPE + MLA_ROPE)
    wuqn = uq[:, :, :MLA_NOPE].reshape(Q_LORA, MLA_HEADS * MLA_NOPE).astype(BF16)
    x1a = uq[:, :, MLA_NOPE:MLA_NOPE + ROPE_HALF].reshape(Q_LORA, MLA_HEADS * ROPE_HALF)
    x2a = uq[:, :, MLA_NOPE + ROPE_HALF:].reshape(Q_LORA, MLA_HEADS * ROPE_HALF)
    wuqr = jnp.concatenate([x1a, x2a, jnp.roll(x1a, ROPE_HALF, axis=1), jnp.roll(x2a, ROPE_HALF, axis=1)],
                           axis=1).astype(BF16)
    wuk = w_uk.reshape(KV_LORA, MLA_HEADS * MLA_NOPE).astype(BF16)
    wuv = w_uv.reshape(KV_LORA, MLA_HEADS * MLA_V).astype(BF16)
    return (whg, wcq, wckv, wkx, mla_q_norm_gain.reshape(1, -1), mla_kv_norm_gain.reshape(1, -1),
            wuqn, wuqr, wuk, wuv)


def _rope_tables(pos):
    inv_freq = ROPE_THETA ** (-jnp.arange(ROPE_HALF, dtype=F32) / ROPE_HALF)
    ang = pos.astype(F32)[:, None] * inv_freq[None, :]
    return jnp.tile(jnp.cos(ang), (1, MLA_HEADS)), jnp.tile(jnp.sin(ang), (1, MLA_HEADS))


def _tile(n, pref):
    return pref if n % pref == 0 else n


def kernel(x_prompt, x_sample, c_prompt, c_sample, cache_kv_latent, cache_k_rope, state_hgrn, state_ffn_conv, w_ada, b_ada, norm_mix_gain, w_in, hg_lb_logits, hg_norm_gain, mla_q_norm_gain, mla_kv_norm_gain, w_uq, w_uk, w_uv, w_out, norm_ffn_gain, w_up, conv_w, conv_b, w_down, final_norm_gain):
    assert w_ada.shape[0] == 1, "single-layer trunk"
    B, L, D = x_prompt.shape
    Bs, Ls, _ = x_sample.shape
    past = cache_kv_latent.shape[2]

    mod = _ada(jnp.concatenate([c_prompt, c_sample], axis=0), w_ada[0], b_ada)
    mod = mod.reshape(B + Bs, 6, D)
    wts = _prep_weights(w_in[0], mla_q_norm_gain[0], mla_kv_norm_gain[0], w_uq[0], w_uk[0], w_uv[0])
    gmix = norm_mix_gain
    wout = w_out[0].astype(BF16)
    wup = w_up[0].astype(BF16)
    wdn = w_down[0].astype(BF16)
    gfin = final_norm_gain.reshape(1, D)
    ffn_w = (wout, norm_ffn_gain, wup, conv_w[0], conv_b, wdn, gfin)

    def layer(x, mod_x, pos, s0, conv0, cache):
        n, l, _ = x.shape
        cos, sin = _rope_tables(pos)
        units = l % UNIT == 0
        raw, q, k, v, lat, kpe = _inproj(x, mod_x, gmix, wts, cos, sin, _tile(l, 256), units)
        if units:
            nu = max(n for n in (4, 2, 1) if (l // UNIT) % n == 0)
            ohg, s_new = _hgrn_units(raw, hg_lb_logits, hg_norm_gain, s0, nu)
        else:
            ohg, s_new = _hgrn(raw, hg_lb_logits, hg_norm_gain, s0, _tile(l, 256))
        if cache is None:
            omla = _attn(q, k, v, _tile(l, 512))
        else:
            omla = _attn_cache(q, k, v, cache[0], cache[1], wts[8], wts[9])
        y, conv_new = _ffn(x, ohg, omla, mod_x, conv0, *ffn_w, _tile(l, 512), FF_TILE)
        return y, lat[None], kpe[None], s_new[None], conv_new[None]

    zeros_state = jnp.zeros((B, HG_HEADS, HG_DK, HG_DV), F32)
    zeros_conv = jnp.zeros((B, CARRY_ROWS, D_FF), F32)
    conv0_s = jnp.pad(state_ffn_conv[0], ((0, 0), (CARRY_ROWS - (CONV_W - 1), 0), (0, 0)))
    yp, latp, kpep, hgp, cvp = layer(x_prompt, mod[:B], jnp.arange(L), zeros_state, zeros_conv, None)
    ys, lats, kpes, hgs, cvs = layer(x_sample, mod[B:], past + jnp.arange(Ls), state_hgrn[0], conv0_s,
                                     (cache_kv_latent[0], cache_k_rope[0]))
    return (yp, ys, latp, kpep, hgp, cvp, lats, kpes, hgs, cvs)
```

```python
import functools

import numpy as np
import jax
import jax.numpy as jnp
from jax import lax
from jax.experimental import pallas as pl
from jax.experimental.pallas import tpu as pltpu

F32 = jnp.float32
BF16 = jnp.bfloat16

D_MODEL = 1024
CHUNK = 64
HG_HEADS = 4
HG_DK = 128
HG_DV = 128
HG_W = HG_HEADS * HG_DK
MLA_HEADS = 4
MLA_NOPE = 128
MLA_ROPE = 64
ROPE_HALF = MLA_ROPE // 2
MLA_V = 128
Q_LORA = 384
KV_LORA = 256
ROPE_THETA = 10000.0
MLA_SCALE = (MLA_NOPE + MLA_ROPE) ** -0.5
Q_SCALE = MLA_SCALE * float(np.log2(np.e))
QK_W = 2 * MLA_NOPE
D_FF = 2816
CONV_W = 3
EPS = 1e-6

SUB = 16
FF_TILE = D_FF
ATTN_LOOKAHEAD = 2
CARRY_ROWS = 8
VMEM_LIMIT = 56 * 1024 * 1024
UNIT = 128
SLABS = 8
SLAB_ROWS = UNIT // SLABS
N_LEVELS = 7
DIAG_LEVEL = N_LEVELS


def _dot(a, b):
    return jnp.dot(a, b, preferred_element_type=F32)


def _dot_nt(a, b):
    return lax.dot_general(a, b, (((1,), (1,)), ((), ())), preferred_element_type=F32)


def _dot_tn(a, b):
    return lax.dot_general(a, b, (((0,), (0,)), ((), ())), preferred_element_type=F32)


def _split3(x):
    x1 = x.astype(BF16)
    r1 = x - x1.astype(F32)
    x2 = r1.astype(BF16)
    r2 = r1 - x2.astype(F32)
    return x1, x2, r2.astype(BF16)


def _rms(x, g):
    return x * lax.rsqrt(jnp.mean(x * x, axis=-1, keepdims=True) + EPS) * g


def _const_spec(shape):
    n = len(shape)
    return pl.BlockSpec(shape, lambda *_: (0,) * n)


def _ada_body(c_ref, w_ref, b_ref, o_ref):
    c = c_ref[...]
    s = c * jax.nn.sigmoid(c)
    s1, s2, _ = _split3(s)
    w = w_ref[...]
    w1 = w.astype(BF16)
    w2 = (w - w1.astype(F32)).astype(BF16)
    o_ref[...] = _dot(s1, w1) + _dot(s1, w2) + _dot(s2, w1) + b_ref[...]


def _ada(c, w_ada, b_ada):
    n, d = c.shape
    nout = w_ada.shape[1]
    tn = 1024
    return pl.pallas_call(
        _ada_body,
        out_shape=jax.ShapeDtypeStruct((n, nout), F32),
        grid=(nout // tn,),
        in_specs=[pl.BlockSpec((n, d), lambda j: (0, 0)),
                  pl.BlockSpec((d, tn), lambda j: (0, j)),
                  pl.BlockSpec((1, tn), lambda j: (0, j))],
        out_specs=pl.BlockSpec((n, tn), lambda j: (0, j)),
        compiler_params=pltpu.CompilerParams(dimension_semantics=("arbitrary",)),
        name="adaln_mod",
    )(c, w_ada, b_ada)


def _rope_slots(r1, r2, nope, out_ref, slot):
    for h in range(MLA_HEADS):
        xh = jnp.where(slot == h, r1, jnp.where(slot == (h + 1) % MLA_HEADS, r2, 0.0))
        out_ref[0, :, QK_W * h:QK_W * h + MLA_NOPE] = nope[:, MLA_NOPE * h:MLA_NOPE * (h + 1)].astype(BF16)
        out_ref[0, :, QK_W * h + MLA_NOPE:QK_W * (h + 1)] = xh.astype(BF16)


def _inproj_body(x_ref, mod_ref, gmix_ref, perm_ref, whg_ref, wcq_ref, wckv_ref, wkx_ref, qng_ref, kvng_ref,
                 wuqn_ref, wuqr_ref, wuk_ref, wuvt_ref, cos_ref, sin_ref,
                 hg_out, q_out, k_out, v_out, lat_out, kpe_out, *, slab):
    x = x_ref[0]
    mod = mod_ref[0]
    sh1 = mod[0:1]
    sc1 = mod[1:2]
    h = _rms(x, gmix_ref[...]) * (1.0 + sc1) + sh1
    hb = h.astype(BF16)
    if slab:
        raw = _dot(_dot(perm_ref[...], hb).astype(BF16), whg_ref[...])
        raw_w = raw.shape[1]
        for u in range(raw.shape[0] // UNIT):
            for p in range(SLABS):
                r0 = UNIT * u + SLAB_ROWS * p
                hg_out[0, u, :, raw_w * p:raw_w * (p + 1)] = raw[r0:r0 + SLAB_ROWS]
    else:
        hg_out[0] = _dot(hb, whg_ref[...])

    cos = cos_ref[...]
    sin = sin_ref[...]
    lane = lax.broadcasted_iota(jnp.int32, cos.shape, 1)
    slot = lane // ROPE_HALF

    cq = _rms(_dot(hb, wcq_ref[...]), qng_ref[...]).astype(BF16)
    qn = _dot(cq, wuqn_ref[...]) * Q_SCALE
    qr = _dot(cq, wuqr_ref[...]) * Q_SCALE
    qr1 = qr[:, 0:128] * cos - qr[:, 128:256] * sin
    qr2 = qr[:, 384:512] * cos + qr[:, 256:384] * sin
    _rope_slots(qr1, qr2, qn, q_out, slot)

    lat = _rms(_dot(hb, wckv_ref[...]), kvng_ref[...])
    lat_out[0] = lat
    latb = lat.astype(BF16)
    kn = _dot(latb, wuk_ref[...])
    v_out[0] = _dot_nt(wuvt_ref[...], latb).astype(BF16)
    kx = _dot(hb, wkx_ref[...])
    kr1 = kx[:, 0:128] * cos - kx[:, 128:256] * sin
    kr2 = kx[:, 128:256] * cos + kx[:, 0:128] * sin
    kpe_out[0] = jnp.where(lane < ROPE_HALF, kr1, kr2)[:, 0:MLA_ROPE]
    _rope_slots(kr1, kr2, kn, k_out, slot)


def _unit_perm(n):
    i = np.arange(n)
    src = UNIT * (i // UNIT) + _unit_time()[i % UNIT]
    return jnp.asarray(src[:, None] == i[None, :], BF16)


def _inproj(x, mod, gmix, wts, cos, sin, tm, slab):
    B, L, D = x.shape
    (whg, wcq, wckv, wkx, qng, kvng, wuqn, wuqr, wuk, wuv) = wts
    wuvt = wuv.T
    perm = _unit_perm(tm) if slab else jnp.zeros((8, 128), BF16)
    row = lambda w: pl.BlockSpec((1, tm, w), lambda b, i: (b, i, 0))
    in_specs = [row(D), pl.BlockSpec((1, 6, D), lambda b, i: (b, 0, 0)), _const_spec(gmix.shape)]
    in_specs += [_const_spec(w.shape) for w in (perm, whg, wcq, wckv, wkx, qng, kvng, wuqn, wuqr, wuk, wuvt)]
    in_specs += [pl.BlockSpec((tm, 128), lambda b, i: (i, 0))] * 2
    vw = MLA_HEADS * MLA_V
    shapes = [((B, L, 4 * HG_W), F32), ((B, L, MLA_HEADS * QK_W), BF16), ((B, L, MLA_HEADS * QK_W), BF16),
              ((B, vw, L), BF16), ((B, L, KV_LORA), F32), ((B, L, MLA_ROPE), F32)]
    out_specs = [row(s[2]) for s, _ in shapes]
    out_specs[3] = pl.BlockSpec((1, vw, tm), lambda b, i: (b, 0, i))
    if slab:
        shapes[0] = ((B, L // UNIT, SLAB_ROWS, SLABS * 4 * HG_W), F32)
        out_specs[0] = pl.BlockSpec((1, tm // UNIT, SLAB_ROWS, SLABS * 4 * HG_W), lambda b, i: (b, i, 0, 0))
    return pl.pallas_call(
        functools.partial(_inproj_body, slab=slab),
        out_shape=[jax.ShapeDtypeStruct(s, dt) for s, dt in shapes],
        grid=(B, L // tm),
        in_specs=in_specs,
        out_specs=out_specs,
        compiler_params=pltpu.CompilerParams(dimension_semantics=("arbitrary", "arbitrary"),
                                             vmem_limit_bytes=VMEM_LIMIT),
        name="in_proj",
    )(x, mod, gmix, perm, whg, wcq, wckv, wkx, qng, kvng, wuqn, wuqr, wuk, wuvt, cos, sin)


def _hgrn_body(raw_ref, lbl_ref, gain_ref, s0_ref, tin_ref, tup_ref, ones_ref,
               o_out, s_out,
               st_scr, b_scr, q_scr, kk_scr, v_scr, eb_scr, qt_scr, kt_scr, vb_scr, p_scr, o_scr, *, th):
    i = pl.program_id(1)

    @pl.when(i == 0)
    def _():
        for h in range(HG_HEADS):
            st_scr[h] = s0_ref[0, h].T

    lbl = lbl_ref[...]
    e = jnp.exp(lbl - jnp.max(lbl, axis=0, keepdims=True))
    lb = (e / jnp.sum(e, axis=0, keepdims=True))[0:1]

    q = raw_ref[0, :, 0:HG_W] * HG_DK ** -0.5
    f = lb + (1.0 - lb) * jax.nn.sigmoid(raw_ref[0, :, HG_W:2 * HG_W])
    kk = 1.0 - f
    v = raw_ref[0, :, 2 * HG_W:3 * HG_W]
    l1, l2, l3 = _split3(jnp.log(f))
    tin = tin_ref[...]
    tup = tup_ref[...]
    b = _dot(tin, l1) + _dot(tin, l2) + _dot(tin, l3)
    c = _dot(tup, l1) + _dot(tup, l2) + _dot(tup, l3)
    eb = jnp.exp(b)
    b_scr[...] = b
    q_scr[...] = q
    kk_scr[...] = kk
    v_scr[...] = v
    eb_scr[...] = eb
    qt_scr[...] = (q * eb).astype(BF16)
    kt_scr[...] = (kk * jnp.exp(c)).astype(BF16)
    vb_scr[...] = v.astype(BF16)

    rowid = lax.broadcasted_iota(jnp.int32, (SUB, HG_W), 0)

    def step(j, carry):
        r0 = pl.multiple_of(j * SUB, SUB)
        bj = b_scr[pl.ds(r0, SUB), :]
        qj = q_scr[pl.ds(r0, SUB), :]
        for s in range(SUB):
            bs = b_scr[pl.ds(r0 + s, 1), :]
            ks = kk_scr[pl.ds(r0 + s, 1), :]
            dec = jnp.exp(jnp.where(rowid >= s, bj - bs, -jnp.inf))
            p_scr[s * SUB:(s + 1) * SUB, :] = (qj * dec * ks).astype(BF16)
        r = _dot(p_scr[...], ones_ref[...])
        od = jnp.zeros((SUB, HG_W), F32)
        for s in range(SUB):
            od = od + r[s * SUB:(s + 1) * SUB, :] * v_scr[pl.ds(r0 + s, 1), :]
        qt = qt_scr[pl.ds(r0, SUB), :]
        kt = kt_scr[pl.ds(r0, SUB), :]
        vb = vb_scr[pl.ds(r0, SUB), :]
        dj = eb_scr[pl.ds(r0 + SUB - 1, 1), :]
        for h in range(HG_HEADS):
            hs = slice(h * HG_DK, (h + 1) * HG_DK)
            st = st_scr[h]
            oi = _dot_nt(qt[:, hs], st.astype(BF16))
            st_scr[h] = st * dj[:, hs] + _dot_tn(vb[:, hs], kt[:, hs])
            o_scr[pl.ds(r0, SUB), hs] = oi + od[:, hs]
        return carry

    lax.fori_loop(0, th // SUB, step, 0)

    gate = raw_ref[0, :, 3 * HG_W:4 * HG_W]
    gate = gate * jax.nn.sigmoid(gate)
    gain = gain_ref[...]
    for h in range(HG_HEADS):
        hs = slice(h * HG_DV, (h + 1) * HG_DV)
        o_out[0, :, hs] = (_rms(o_scr[:, hs], gain) * gate[:, hs]).astype(BF16)

    @pl.when(i == pl.num_programs(1) - 1)
    def _():
        for h in range(HG_HEADS):
            s_out[0, h] = st_scr[h].T


def _step_matrices(th):
    r = np.arange(th)
    same = (r[:, None] // SUB) == (r[None, :] // SUB)
    tin = (same & (r[None, :] <= r[:, None])).astype(np.float32)
    tup = (same & (r[None, :] > r[:, None])).astype(np.float32)
    hd = np.arange(HG_W) // HG_DK
    ones = (hd[:, None] == hd[None, :]).astype(np.float32)
    return jnp.asarray(tin, BF16), jnp.asarray(tup, BF16), jnp.asarray(ones, BF16)


def _hgrn(raw, lb_logits, gain, s0, th):
    B, L, _ = raw.shape
    tin, tup, ones = _step_matrices(th)
    f32s = lambda: pltpu.VMEM((th, HG_W), F32)
    b16s = lambda: pltpu.VMEM((th, HG_W), BF16)
    return pl.pallas_call(
        functools.partial(_hgrn_body, th=th),
        out_shape=[jax.ShapeDtypeStruct((B, L, HG_W), BF16),
                   jax.ShapeDtypeStruct((B, HG_HEADS, HG_DK, HG_DV), F32)],
        grid=(B, L // th),
        in_specs=[pl.BlockSpec((1, th, 4 * HG_W), lambda b, i: (b, i, 0)),
                  _const_spec(lb_logits.shape), _const_spec(gain.shape),
                  pl.BlockSpec((1, HG_HEADS, HG_DK, HG_DV), lambda b, i: (b, 0, 0, 0)),
                  _const_spec(tin.shape), _const_spec(tup.shape), _const_spec(ones.shape)],
        out_specs=[pl.BlockSpec((1, th, HG_W), lambda b, i: (b, i, 0)),
                   pl.BlockSpec((1, HG_HEADS, HG_DK, HG_DV), lambda b, i: (b, 0, 0, 0))],
        scratch_shapes=[pltpu.VMEM((HG_HEADS, HG_DV, HG_DK), F32),
                        f32s(), f32s(), f32s(), f32s(), f32s(),
                        b16s(), b16s(), b16s(),
                        pltpu.VMEM((SUB * SUB, HG_W), BF16), f32s()],
        compiler_params=pltpu.CompilerParams(dimension_semantics=("arbitrary", "arbitrary"),
                                             vmem_limit_bytes=VMEM_LIMIT),
        name="hgrn_scan",
    )(raw, lb_logits, gain, s0, tin, tup, ones)


def _unit_time():
    i = np.arange(UNIT)
    return SLABS * (i % SLAB_ROWS) + i // SLAB_ROWS


def _unit_tables():
    t = _unit_time()
    tin = (t[None, :] <= t[:, None]).astype(np.float32)
    rows = []
    for c in (1, 2, 4, 8):
        g = np.arange(SLAB_ROWS)
        t_bnd = SLABS * (2 * c * (g // (2 * c)) + c - 1) + SLABS - 1
        rows.append((t[None, :] <= t_bnd[:, None]).astype(np.float32))
    tin_ext = np.concatenate([tin] + rows, axis=0)
    x = t[:, None] ^ t[None, :]
    lv = np.where(x > 0, np.floor(np.log2(np.maximum(x, 1))), DIAG_LEVEL).astype(np.int32)
    lv = np.where(t[None, :] <= t[:, None], lv, -1)
    return jnp.asarray(tin_ext, BF16), jnp.asarray(np.concatenate([lv, lv], axis=1), jnp.int32)


def _pair_blockdiag(a, b):
    z = jnp.zeros_like(a)
    return jnp.concatenate([jnp.concatenate([a, z], axis=1), jnp.concatenate([z, b], axis=1)], axis=0)


def _hgrn_units_body(raw_ref, lbl_ref, gain_ref, s0_ref, tin_ref, lv_ref, unperm_ref, o_out, s_out, st_scr, *, nu):
    i = pl.program_id(1)

    @pl.when(i == 0)
    def _():
        for h in range(HG_HEADS):
            st_scr[h] = s0_ref[0, h].T

    lbl = lbl_ref[...]
    e = jnp.exp(lbl - jnp.max(lbl, axis=0, keepdims=True))
    lb = (e / jnp.sum(e, axis=0, keepdims=True))[0:1]
    tin = tin_ref[...]
    lv = lv_ref[...]
    gain = gain_ref[...]
    raw_w = 4 * HG_W

    units = range(nu)
    pairs = range(HG_HEADS // 2)

    def field(u, k):
        return jnp.concatenate([raw_ref[0, u, :, raw_w * p + HG_W * k:raw_w * p + HG_W * (k + 1)]
                                for p in range(SLABS)], axis=0)

    def pair_cols(x, hp):
        c0 = 2 * HG_DK * hp
        return x[:, c0:c0 + HG_DK], x[:, c0 + HG_DK:c0 + 2 * HG_DK]

    q = [field(u, 0) * HG_DK ** -0.5 for u in units]
    f = [lb + (1.0 - lb) * jax.nn.sigmoid(field(u, 1)) for u in units]
    kk = [1.0 - f[u] for u in units]
    lsplit = [_split3(jnp.log(f[u]) * np.float32(1.0 / np.log(2.0))) for u in units]
    bx = [_dot(tin, lsplit[u][0]) + _dot(tin, lsplit[u][1]) + _dot(tin, lsplit[u][2]) for u in units]
    b = [bx[u][0:UNIT] for u in units]
    bs = [[b[u][SLAB_ROWS * p:SLAB_ROWS * (p + 1)] for p in range(SLABS)] for u in units]
    vb = [field(u, 2).astype(BF16) for u in units]
    qb = [q[u].astype(BF16) for u in units]
    kkb = [kk[u].astype(BF16) for u in units]

    def slab_decay(u, level, p):
        if level < 3:
            half = 1 << level
            ref = p - p % (2 * half) + half - 1
            if p == ref:
                return jnp.ones_like(bs[u][p])
            return jnp.exp2(bs[u][p] - bs[u][ref] if p > ref else bs[u][ref] - bs[u][p])
        r = bx[u][UNIT + SLAB_ROWS * (level - 3):UNIT + SLAB_ROWS * (level - 2)]
        if level == N_LEVELS - 1:
            h = SLAB_ROWS // 2
            return jnp.concatenate([jnp.exp2(r[:h] - bs[u][p][:h]), jnp.exp2(bs[u][p][h:] - r[h:])], axis=0)
        return jnp.exp2(-jnp.abs(bs[u][p] - r))

    acc = [[jnp.zeros((UNIT, 2 * HG_DK), F32) for _ in pairs] for _ in units]
    for level in range(N_LEVELS + 1):
        here = lv == level
        for u in units:
            if level == DIAG_LEVEL:
                qt, kt = qb[u], kkb[u]
            else:
                dec = jnp.concatenate([slab_decay(u, level, p) for p in range(SLABS)], axis=0).astype(BF16)
                qt, kt = qb[u] * dec, kkb[u] * dec
            for hp in pairs:
                a2 = _dot_nt(qt[:, 2 * HG_DK * hp:2 * HG_DK * (hp + 1)], _pair_blockdiag(*pair_cols(kt, hp)))
                acc[u][hp] = jnp.where(here, a2, acc[u][hp])

    b_end = [b[u][UNIT - 1:UNIT] for u in units]
    qi = [(q[u] * jnp.exp2(b[u])).astype(BF16) for u in units]
    ks = [(kk[u] * jnp.exp2(b_end[u] - b[u])).astype(BF16) for u in units]
    d = [jnp.exp2(b_end[u]) for u in units]
    o_intra = [[_dot(acc[u][hp].astype(BF16), _pair_blockdiag(*pair_cols(vb[u], hp))) for hp in pairs]
               for u in units]
    ds = [[_dot_tn(vb[u][:, h * HG_DV:(h + 1) * HG_DV], ks[u][:, h * HG_DK:(h + 1) * HG_DK])
           for h in range(HG_HEADS)] for u in units]
    st = [st_scr[h] for h in range(HG_HEADS)]
    for u in units:
        o = jnp.concatenate(
            [o_intra[u][hp] + _dot_nt(qi[u][:, 2 * HG_DK * hp:2 * HG_DK * (hp + 1)],
                                      _pair_blockdiag(st[2 * hp].astype(BF16), st[2 * hp + 1].astype(BF16)))
             for hp in pairs], axis=1)
        st = [st[h] * d[u][:, h * HG_DK:(h + 1) * HG_DK] + ds[u][h] for h in range(HG_HEADS)]
        gate = field(u, 3)
        gate = gate * jax.nn.sigmoid(gate)
        res = jnp.concatenate([_rms(o[:, h * HG_DV:(h + 1) * HG_DV], gain) for h in range(HG_HEADS)], axis=1)
        res = (res * gate).astype(BF16)
        o_out[0, UNIT * u:UNIT * (u + 1), :] = _dot(unperm_ref[...], res).astype(BF16)
    for h in range(HG_HEADS):
        st_scr[h] = st[h]

    @pl.when(i == pl.num_programs(1) - 1)
    def _():
        for h in range(HG_HEADS):
            s_out[0, h] = st_scr[h].T


def _hgrn_units(raw4, lb_logits, gain, s0, nu):
    B, n_units, _, w8 = raw4.shape
    L = n_units * UNIT
    tin, lv = _unit_tables()
    unperm = _unit_perm(UNIT).T
    return pl.pallas_call(
        functools.partial(_hgrn_units_body, nu=nu),
        out_shape=[jax.ShapeDtypeStruct((B, L, HG_W), BF16),
                   jax.ShapeDtypeStruct((B, HG_HEADS, HG_DK, HG_DV), F32)],
        grid=(B, n_units // nu),
        in_specs=[pl.BlockSpec((1, nu, SLAB_ROWS, w8), lambda b, i: (b, i, 0, 0)),
                  _const_spec(lb_logits.shape), _const_spec(gain.shape),
                  pl.BlockSpec((1, HG_HEADS, HG_DK, HG_DV), lambda b, i: (b, 0, 0, 0)),
                  _const_spec(tin.shape), _const_spec(lv.shape), _const_spec(unperm.shape)],
        out_specs=[pl.BlockSpec((1, nu * UNIT, HG_W), lambda b, i: (b, i, 0)),
                   pl.BlockSpec((1, HG_HEADS, HG_DK, HG_DV), lambda b, i: (b, 0, 0, 0))],
        scratch_shapes=[pltpu.VMEM((HG_HEADS, HG_DV, HG_DK), F32)],
        compiler_params=pltpu.CompilerParams(dimension_semantics=("arbitrary", "arbitrary"),
                                             vmem_limit_bytes=VMEM_LIMIT),
        name="hgrn_units",
    )(raw4, lb_logits, gain, s0, tin, lv, unperm)


def _attn_body(q_ref, k_ref, vt_ref, o_ref, acc_scr, *, tq, nq):
    for qi in range(nq):
        q = q_ref[0, qi * tq:(qi + 1) * tq, :]
        nblk = qi + 1
        scores = lambda i: _dot_nt(k_ref[0, i * tq:(i + 1) * tq, :], q)
        pending = [scores(i) for i in range(min(ATTN_LOOKAHEAD, nblk))]
        m = jnp.full((1, tq), -jnp.inf, F32)
        l = jnp.zeros((1, tq), F32)
        for ki in range(nblk):
            st = pending.pop(0)
            if ki + ATTN_LOOKAHEAD < nblk:
                pending.append(scores(ki + ATTN_LOOKAHEAD))
            if ki == nblk - 1:
                kc = lax.broadcasted_iota(jnp.int32, st.shape, 0) // CHUNK
                qc = lax.broadcasted_iota(jnp.int32, st.shape, 1) // CHUNK
                st = jnp.where(kc <= qc, st, -jnp.inf)
            m_new = jnp.maximum(m, jnp.max(st, axis=0, keepdims=True))
            p = jnp.exp2(st - m_new)
            alpha = jnp.exp2(m - m_new)
            l = alpha * l + jnp.sum(p, axis=0, keepdims=True)
            pv = _dot(vt_ref[0, :, ki * tq:(ki + 1) * tq], p.astype(BF16))
            if ki == 0:
                acc_scr[qi % 2] = pv
            else:
                acc_scr[qi % 2] = alpha * acc_scr[qi % 2] + pv
            m = m_new
        o_ref[0, qi * tq:(qi + 1) * tq, :] = (acc_scr[qi % 2] / l).T.astype(BF16)


def _attn(q, k, vt, tq):
    B, L, _ = q.shape
    return pl.pallas_call(
        functools.partial(_attn_body, tq=tq, nq=L // tq),
        out_shape=jax.ShapeDtypeStruct((B, L, MLA_HEADS * MLA_V), BF16),
        grid=(B, MLA_HEADS),
        in_specs=[pl.BlockSpec((1, L, QK_W), lambda b, h: (b, 0, h)),
                  pl.BlockSpec((1, L, QK_W), lambda b, h: (b, 0, h)),
                  pl.BlockSpec((1, MLA_V, L), lambda b, h: (b, h, 0))],
        out_specs=pl.BlockSpec((1, L, MLA_V), lambda b, h: (b, 0, h)),
        scratch_shapes=[pltpu.VMEM((2, MLA_V, tq), F32)],
        compiler_params=pltpu.CompilerParams(dimension_semantics=("arbitrary",) * 2,
                                             vmem_limit_bytes=VMEM_LIMIT),
        name="mla_attn_prompt",
    )(q, k, vt)


def _attn_cache_body(q_ref, kn_ref, vn_ref, clat_ref, ckpe_ref, wuk_ref, wuv_ref, place_ref, o_ref):
    n = q_ref.shape[1]
    latc = clat_ref[0].astype(BF16)
    kpec = ckpe_ref[0].astype(BF16)
    qlat, qrope, s_new = [], [], []
    for h in range(MLA_HEADS):
        qh = q_ref[0, :, QK_W * h:QK_W * (h + 1)]
        ws = slice(MLA_NOPE * h, MLA_NOPE * (h + 1))
        qlat.append(_dot_nt(qh[:, 0:MLA_NOPE], wuk_ref[:, ws]).astype(BF16))
        qrope.append(_dot_nt(qh[:, MLA_NOPE:QK_W], place_ref[:, ws]).astype(BF16))
        s_new.append(_dot_nt(qh, kn_ref[0, :, QK_W * h:QK_W * (h + 1)]))
    s_new = jnp.concatenate(s_new, axis=0)
    s_c = (_dot_nt(jnp.concatenate(qlat, axis=0), latc)
           + _dot_nt(jnp.concatenate(qrope, axis=0), kpec))
    m = jnp.maximum(jnp.max(s_c, axis=-1, keepdims=True), jnp.max(s_new, axis=-1, keepdims=True))
    p_c = jnp.exp2(s_c - m)
    p_new = jnp.exp2(s_new - m)
    l = jnp.sum(p_c, axis=-1, keepdims=True) + jnp.sum(p_new, axis=-1, keepdims=True)
    o_lat = _dot(p_c.astype(BF16), latc).astype(BF16)
    p_new = p_new.astype(BF16)
    for h in range(MLA_HEADS):
        ws = slice(MLA_V * h, MLA_V * (h + 1))
        rs = slice(n * h, n * (h + 1))
        acc = _dot(o_lat[rs], wuv_ref[:, ws]) + _dot_nt(p_new[rs], vn_ref[0, ws, :])
        o_ref[0, :, ws] = (acc / l[rs]).astype(BF16)


def _rope_place():
    p = np.zeros((MLA_ROPE, MLA_HEADS * MLA_NOPE), np.float32)
    for h in range(MLA_HEADS):
        for i in range(ROPE_HALF):
            p[i, MLA_NOPE * h + ROPE_HALF * h + i] = 1.0
            p[ROPE_HALF + i, MLA_NOPE * h + ROPE_HALF * ((h + 1) % MLA_HEADS) + i] = 1.0
    return jnp.asarray(p, BF16)


def _attn_cache(q, kn, vn, clat, ckpe, wuk, wuv):
    B, L, _ = q.shape
    P = clat.shape[1]
    place = _rope_place()
    bspec = lambda n, w: pl.BlockSpec((1, n, w), lambda b: (b, 0, 0))
    return pl.pallas_call(
        _attn_cache_body,
        out_shape=jax.ShapeDtypeStruct((B, L, MLA_HEADS * MLA_V), BF16),
        grid=(B,),
        in_specs=[bspec(L, MLA_HEADS * QK_W), bspec(L, MLA_HEADS * QK_W), bspec(MLA_HEADS * MLA_V, L),
                  bspec(P, KV_LORA), bspec(P, MLA_ROPE),
                  _const_spec(wuk.shape), _const_spec(wuv.shape), _const_spec(place.shape)],
        out_specs=bspec(L, MLA_HEADS * MLA_V),
        compiler_params=pltpu.CompilerParams(dimension_semantics=("arbitrary",),
                                             vmem_limit_bytes=VMEM_LIMIT),
        name="mla_attn_cache",
    )(q, kn, vn, clat, ckpe, wuk, wuv, place)


def _ffn_body(x_ref, ohg_ref, omla_ref, mod_ref, conv0_ref, wout_ref, gffn_ref, wup_ref, cw_ref, cb_ref,
              wdn_ref, gfin_ref, y_ref, conv_out, carry_scr, a_scr, u_scr, *, tm, ft):
    i = pl.program_id(1)
    last = pl.num_programs(1) - 1

    @pl.when(i == 0)
    def _():
        carry_scr[...] = conv0_ref[0]

    mod = mod_ref[0]
    g1 = mod[2:3]
    sh2 = mod[3:4]
    sc2 = mod[4:5]
    g2 = mod[5:6]
    o = _dot(ohg_ref[0], wout_ref[0:HG_W, :]) + _dot(omla_ref[0], wout_ref[HG_W:2 * HG_W, :])
    x1 = x_ref[0] + g1 * o
    h2 = (_rms(x1, gffn_ref[...]) * (1.0 + sc2) + sh2).astype(BF16)
    for j in range(D_FF // ft):
        cs = slice(j * ft, (j + 1) * ft)
        a = _dot(h2, wup_ref[:, cs])
        v = _dot(h2, wup_ref[:, D_FF + j * ft:D_FF + (j + 1) * ft])
        a_scr[0:CARRY_ROWS, :] = carry_scr[:, cs]
        a_scr[CARRY_ROWS:CARRY_ROWS + tm, :] = a
        carry_scr[:, cs] = a[tm - CARRY_ROWS:tm]
        cw = cw_ref[:, cs]
        conv = (cb_ref[:, cs] + cw[0:1] * a_scr[CARRY_ROWS - 2:CARRY_ROWS - 2 + tm, :]
                + cw[1:2] * a_scr[CARRY_ROWS - 1:CARRY_ROWS - 1 + tm, :] + cw[2:3] * a)
        g = 0.5 * conv * (1.0 + lax.erf(conv * np.float32(np.sqrt(0.5))))
        u_scr[:, cs] = (g * v).astype(BF16)

        @pl.when(i == last)
        def _():
            conv_out[0, :, cs] = a[tm - (CONV_W - 1):tm]

    x2 = x1 + g2 * _dot(u_scr[...], wdn_ref[...])
    y_ref[0] = _rms(x2, gfin_ref[...])


def _ffn(x, ohg, omla, mod, conv0, wout, gffn, wup, cw, cb, wdn, gfin, tm, ft):
    B, L, D = x.shape
    row = lambda w: pl.BlockSpec((1, tm, w), lambda b, i: (b, i, 0))
    once = lambda a: pl.BlockSpec(a.shape, lambda b, i: (0,) * a.ndim, pipeline_mode=pl.Buffered(1))
    return pl.pallas_call(
        functools.partial(_ffn_body, tm=tm, ft=ft),
        out_shape=[jax.ShapeDtypeStruct((B, L, D), F32),
                   jax.ShapeDtypeStruct((B, CONV_W - 1, D_FF), F32)],
        grid=(B, L // tm),
        in_specs=[row(D), row(HG_W), row(MLA_HEADS * MLA_V),
                  pl.BlockSpec((1, 6, D), lambda b, i: (b, 0, 0)),
                  pl.BlockSpec((1, CARRY_ROWS, D_FF), lambda b, i: (b, 0, 0)),
                  once(wout), once(gffn), once(wup), once(cw), once(cb), once(wdn), once(gfin)],
        out_specs=[row(D), pl.BlockSpec((1, CONV_W - 1, D_FF), lambda b, i: (b, 0, 0))],
        scratch_shapes=[pltpu.VMEM((CARRY_ROWS, D_FF), F32), pltpu.VMEM((CARRY_ROWS + tm, ft), F32),
                        pltpu.VMEM((tm, D_FF), BF16)],
        compiler_params=pltpu.CompilerParams(dimension_semantics=("arbitrary", "arbitrary"),
                                             vmem_limit_bytes=VMEM_LIMIT),
        name="out_ffn",
    )(x, ohg, omla, mod, conv0, wout, gffn, wup, cw, cb, wdn, gfin)


def _prep_weights(w_in, mla_q_norm_gain, mla_kv_norm_gain, w_uq, w_uk, w_uv):
    hgw = 4 * HG_W
    whg = w_in[:, :hgw].astype(BF16)
    wcq = w_in[:, hgw:hgw + Q_LORA].astype(BF16)
    wckv = w_in[:, hgw + Q_LORA:hgw + Q_LORA + KV_LORA].astype(BF16)
    kp = w_in[:, hgw + Q_LORA + KV_LORA:]
    wkx = jnp.concatenate([jnp.tile(kp[:, :ROPE_HALF], (1, MLA_HEADS)),
                           jnp.tile(kp[:, ROPE_HALF:], (1, MLA_HEADS))], axis=1).astype(BF16)
    uq = w_uq.reshape(Q_LORA, MLA_HEADS, MLA_NOPE + MLA_ROPE)
    wuqn = uq[:, :, :MLA_NOPE].reshape(Q_LORA, MLA_HEADS * MLA_NOPE).astype(BF16)
    x1a = uq[:, :, MLA_NOPE:MLA_NOPE + ROPE_HALF].reshape(Q_LORA, MLA_HEADS * ROPE_HALF)
    x2a = uq[:, :, MLA_NOPE + ROPE_HALF:].reshape(Q_LORA, MLA_HEADS * ROPE_HALF)
    wuqr = jnp.concatenate([x1a, x2a, jnp.roll(x1a, ROPE_HALF, axis=1), jnp.roll(x2a, ROPE_HALF, axis=1)],
                           axis=1).astype(BF16)
    wuk = w_uk.reshape(KV_LORA, MLA_HEADS * MLA_NOPE).astype(BF16)
    wuv = w_uv.reshape(KV_LORA, MLA_HEADS * MLA_V).astype(BF16)
    return (whg, wcq, wckv, wkx, mla_q_norm_gain.reshape(1, -1), mla_kv_norm_gain.reshape(1, -1),
            wuqn, wuqr, wuk, wuv)


def _rope_tables(pos):
    inv_freq = ROPE_THETA ** (-jnp.arange(ROPE_HALF, dtype=F32) / ROPE_HALF)
    ang = pos.astype(F32)[:, None] * inv_freq[None, :]
    return jnp.tile(jnp.cos(ang), (1, MLA_HEADS)), jnp.tile(jnp.sin(ang), (1, MLA_HEADS))


def _tile(n, pref):
    return pref if n % pref == 0 else n


def kernel(x_prompt, x_sample, c_prompt, c_sample, cache_kv_latent, cache_k_rope, state_hgrn, state_ffn_conv, w_ada, b_ada, norm_mix_gain, w_in, hg_lb_logits, hg_norm_gain, mla_q_norm_gain, mla_kv_norm_gain, w_uq, w_uk, w_uv, w_out, norm_ffn_gain, w_up, conv_w, conv_b, w_down, final_norm_gain):
    assert w_ada.shape[0] == 1, "single-layer trunk"
    B, L, D = x_prompt.shape
    Bs, Ls, _ = x_sample.shape
    past = cache_kv_latent.shape[2]

    mod = _ada(jnp.concatenate([c_prompt, c_sample], axis=0), w_ada[0], b_ada)
    mod = mod.reshape(B + Bs, 6, D)
    wts = _prep_weights(w_in[0], mla_q_norm_gain[0], mla_kv_norm_gain[0], w_uq[0], w_uk[0], w_uv[0])
    gmix = norm_mix_gain
    wout = w_out[0].astype(BF16)
    wup = w_up[0].astype(BF16)
    wdn = w_down[0].astype(BF16)
    gfin = final_norm_gain.reshape(1, D)
    ffn_w = (wout, norm_ffn_gain, wup, conv_w[0], conv_b, wdn, gfin)

    def layer(x, mod_x, pos, s0, conv0, cache):
        n, l, _ = x.shape
        cos, sin = _rope_tables(pos)
        units = l % UNIT == 0
        raw, q, k, v, lat, kpe = _inproj(x, mod_x, gmix, wts, cos, sin, _tile(l, 256), units)
        if units:
            nu = max(n for n in (4, 2, 1) if (l // UNIT) % n == 0)
            ohg, s_new = _hgrn_units(raw, hg_lb_logits, hg_norm_gain, s0, nu)
        else:
            ohg, s_new = _hgrn(raw, hg_lb_logits, hg_norm_gain, s0, _tile(l, 256))
        if cache is None:
            omla = _attn(q, k, v, _tile(l, 512))
        else:
            omla = _attn_cache(q, k, v, cache[0], cache[1], wts[8], wts[9])
        y, conv_new = _ffn(x, ohg, omla, mod_x, conv0, *ffn_w, _tile(l, 512), FF_TILE)
        return y, lat[None], kpe[None], s_new[None], conv_new[None]

    zeros_state = jnp.zeros((B, HG_HEADS, HG_DK, HG_DV), F32)
    zeros_conv = jnp.zeros((B, CARRY_ROWS, D_FF), F32)
    conv0_s = jnp.pad(state_ffn_conv[0], ((0, 0), (CARRY_ROWS - (CONV_W - 1), 0), (0, 0)))
    yp, latp, kpep, hgp, cvp = layer(x_prompt, mod[:B], jnp.arange(L), zeros_state, zeros_conv, None)
    ys, lats, kpes, hgs, cvs = layer(x_sample, mod[B:], past + jnp.arange(Ls), state_hgrn[0], conv0_s,
                                     (cache_kv_latent[0], cache_k_rope[0]))
    return (yp, ys, latp, kpep, hgp, cvp, lats, kpes, hgs, cvs)
```

```python
import functools

import numpy as np
import jax
import jax.numpy as jnp
from jax import lax
from jax.experimental import pallas as pl
from jax.experimental.pallas import tpu as pltpu

F32 = jnp.float32
BF16 = jnp.bfloat16

D_MODEL = 1024
CHUNK = 64
HG_HEADS = 4
HG_DK = 128
HG_DV = 128
HG_W = HG_HEADS * HG_DK
MLA_HEADS = 4
MLA_NOPE = 128
MLA_ROPE = 64
ROPE_HALF = MLA_ROPE // 2
MLA_V = 128
Q_LORA = 384
KV_LORA = 256
ROPE_THETA = 10000.0
MLA_SCALE = (MLA_NOPE + MLA_ROPE) ** -0.5
Q_SCALE = MLA_SCALE * float(np.log2(np.e))
QK_W = 2 * MLA_NOPE
D_FF = 2816
CONV_W = 3
EPS = 1e-6

SUB = 16
FF_TILE = D_FF
ATTN_LOOKAHEAD = 2
CARRY_ROWS = 8
VMEM_LIMIT = 56 * 1024 * 1024
UNIT = 128
SLABS = 8
SLAB_ROWS = UNIT // SLABS
N_LEVELS = 7
DIAG_LEVEL = N_LEVELS


def _dot(a, b):
    return jnp.dot(a, b, preferred_element_type=F32)


def _dot_nt(a, b):
    return lax.dot_general(a, b, (((1,), (1,)), ((), ())), preferred_element_type=F32)


def _dot_tn(a, b):
    return lax.dot_general(a, b, (((0,), (0,)), ((), ())), preferred_element_type=F32)


def _split3(x):
    x1 = x.astype(BF16)
    r1 = x - x1.astype(F32)
    x2 = r1.astype(BF16)
    r2 = r1 - x2.astype(F32)
    return x1, x2, r2.astype(BF16)


def _rms(x, g):
    return x * lax.rsqrt(jnp.mean(x * x, axis=-1, keepdims=True) + EPS) * g


def _const_spec(shape):
    n = len(shape)
    return pl.BlockSpec(shape, lambda *_: (0,) * n)


def _ada_body(c_ref, w_ref, b_ref, o_ref):
    c = c_ref[...]
    s = c * jax.nn.sigmoid(c)
    s1, s2, _ = _split3(s)
    w = w_ref[...]
    w1 = w.astype(BF16)
    w2 = (w - w1.astype(F32)).astype(BF16)
    o_ref[...] = _dot(s1, w1) + _dot(s1, w2) + _dot(s2, w1) + b_ref[...]


def _ada(c, w_ada, b_ada):
    n, d = c.shape
    nout = w_ada.shape[1]
    tn = 1024
    return pl.pallas_call(
        _ada_body,
        out_shape=jax.ShapeDtypeStruct((n, nout), F32),
        grid=(nout // tn,),
        in_specs=[pl.BlockSpec((n, d), lambda j: (0, 0)),
                  pl.BlockSpec((d, tn), lambda j: (0, j)),
                  pl.BlockSpec((1, tn), lambda j: (0, j))],
        out_specs=pl.BlockSpec((n, tn), lambda j: (0, j)),
        compiler_params=pltpu.CompilerParams(dimension_semantics=("arbitrary",)),
        name="adaln_mod",
    )(c, w_ada, b_ada)


def _rope_slots(r1, r2, nope, out_ref, slot):
    for h in range(MLA_HEADS):
        xh = jnp.where(slot == h, r1, jnp.where(slot == (h + 1) % MLA_HEADS, r2, 0.0))
        out_ref[0, :, QK_W * h:QK_W * h + MLA_NOPE] = nope[:, MLA_NOPE * h:MLA_NOPE * (h + 1)].astype(BF16)
        out_ref[0, :, QK_W * h + MLA_NOPE:QK_W * (h + 1)] = xh.astype(BF16)


def _inproj_body(x_ref, mod_ref, gmix_ref, perm_ref, win_ref, wkx_ref, wkpt_ref, qng_ref, kvng_ref,
                 wuqn_ref, wuqr_ref, wuk_ref, wuvt_ref, cos_ref, sin_ref, cost_ref, sint_ref,
                 hg_out, q_out, k_out, v_out, lat_out, kpe_out, *, slab):
    hgw = 4 * HG_W
    whg_ref = win_ref.at[:, 0:hgw]
    wcq_ref = win_ref.at[:, hgw:hgw + Q_LORA]
    wckv_ref = win_ref.at[:, hgw + Q_LORA:hgw + Q_LORA + KV_LORA]
    x = x_ref[0]
    mod = mod_ref[0]
    sh1 = mod[0:1]
    sc1 = mod[1:2]
    h = _rms(x, gmix_ref[...]) * (1.0 + sc1) + sh1
    hb = h.astype(BF16)
    if slab:
        n_units = hb.shape[0] // UNIT
        hbp = jnp.concatenate([_dot(perm_ref[...], hb[UNIT * u:UNIT * (u + 1)]).astype(BF16)
                               for u in range(n_units)], axis=0)
        raw = _dot(hbp, whg_ref[...])
        raw_w = raw.shape[1]
        for u in range(n_units):
            for p in range(SLABS):
                r0 = UNIT * u + SLAB_ROWS * p
                hg_out[0, u, :, raw_w * p:raw_w * (p + 1)] = raw[r0:r0 + SLAB_ROWS]
    else:
        hg_out[0] = _dot(hb, whg_ref[...])

    cos = cos_ref[...]
    sin = sin_ref[...]
    lane = lax.broadcasted_iota(jnp.int32, cos.shape, 1)
    slot = lane // ROPE_HALF

    cq = _rms(_dot(hb, wcq_ref[...]), qng_ref[...]).astype(BF16)
    qn = _dot(cq, wuqn_ref[...]) * Q_SCALE
    qr = _dot(cq, wuqr_ref[...]) * Q_SCALE
    qr1 = qr[:, 0:128] * cos - qr[:, 128:256] * sin
    qr2 = qr[:, 384:512] * cos + qr[:, 256:384] * sin
    _rope_slots(qr1, qr2, qn, q_out, slot)

    lat = _rms(_dot(hb, wckv_ref[...]), kvng_ref[...])
    lat_out[0] = lat
    latb = lat.astype(BF16)
    kn = _dot(latb, wuk_ref[...])
    v_out[0] = _dot_nt(wuvt_ref[...], latb).astype(BF16)
    kx = _dot(hb, wkx_ref[...])
    kr1 = kx[:, 0:128] * cos - kx[:, 128:256] * sin
    kr2 = kx[:, 128:256] * cos + kx[:, 0:128] * sin
    kxt = _dot_nt(wkpt_ref[...], hb)
    x1t, x2t = kxt[0:ROPE_HALF], kxt[ROPE_HALF:MLA_ROPE]
    cost, sint = cost_ref[...], sint_ref[...]
    kpe_out[0] = jnp.concatenate([x1t * cost - x2t * sint, x2t * cost + x1t * sint], axis=0)
    _rope_slots(kr1, kr2, kn, k_out, slot)


def _unit_perm(n):
    i = np.arange(n)
    src = UNIT * (i // UNIT) + _unit_time()[i % UNIT]
    return jnp.asarray(src[:, None] == i[None, :], BF16)


def _inproj(x, mod, gmix, wts, rope, tm, slab):
    B, L, D = x.shape
    (win, wkx, wkpt, qng, kvng, wuqn, wuqr, wuk, wuv) = wts
    cos, sin, cost, sint = rope
    wuvt = wuv.T
    perm = _unit_perm(UNIT) if slab else jnp.zeros((8, 128), BF16)
    row = lambda w: pl.BlockSpec((1, tm, w), lambda b, i: (b, i, 0))
    in_specs = [row(D), pl.BlockSpec((1, 6, D), lambda b, i: (b, 0, 0)), _const_spec(gmix.shape)]
    in_specs += [_const_spec(w.shape) for w in (perm, win, wkx, wkpt, qng, kvng, wuqn, wuqr, wuk, wuvt)]
    in_specs += [pl.BlockSpec((tm, 128), lambda b, i: (i, 0))] * 2
    in_specs += [pl.BlockSpec((ROPE_HALF, tm), lambda b, i: (0, i))] * 2
    vw = MLA_HEADS * MLA_V
    shapes = [((B, L, 4 * HG_W), F32), ((B, L, MLA_HEADS * QK_W), BF16), ((B, L, MLA_HEADS * QK_W), BF16),
              ((B, vw, L), BF16), ((B, L, KV_LORA), F32), ((B, MLA_ROPE, L), F32)]
    out_specs = [row(s[2]) for s, _ in shapes]
    out_specs[3] = pl.BlockSpec((1, vw, tm), lambda b, i: (b, 0, i))
    out_specs[5] = pl.BlockSpec((1, MLA_ROPE, tm), lambda b, i: (b, 0, i))
    if slab:
        shapes[0] = ((B, L // UNIT, SLAB_ROWS, SLABS * 4 * HG_W), F32)
        out_specs[0] = pl.BlockSpec((1, tm // UNIT, SLAB_ROWS, SLABS * 4 * HG_W), lambda b, i: (b, i, 0, 0))
    return pl.pallas_call(
        functools.partial(_inproj_body, slab=slab),
        out_shape=[jax.ShapeDtypeStruct(s, dt) for s, dt in shapes],
        grid=(B, L // tm),
        in_specs=in_specs,
        out_specs=out_specs,
        compiler_params=pltpu.CompilerParams(dimension_semantics=("arbitrary", "arbitrary"),
                                             vmem_limit_bytes=VMEM_LIMIT),
        name="in_proj",
    )(x, mod, gmix, perm, win, wkx, wkpt, qng, kvng, wuqn, wuqr, wuk, wuvt, cos, sin, cost, sint)


def _hgrn_body(raw_ref, lbl_ref, gain_ref, s0_ref, tin_ref, tup_ref, ones_ref,
               o_out, s_out,
               st_scr, b_scr, q_scr, kk_scr, v_scr, eb_scr, qt_scr, kt_scr, vb_scr, p_scr, o_scr, *, th):
    i = pl.program_id(1)

    @pl.when(i == 0)
    def _():
        for h in range(HG_HEADS):
            st_scr[h] = s0_ref[0, h].T

    lbl = lbl_ref[...]
    e = jnp.exp(lbl - jnp.max(lbl, axis=0, keepdims=True))
    lb = (e / jnp.sum(e, axis=0, keepdims=True))[0:1]

    q = raw_ref[0, :, 0:HG_W] * HG_DK ** -0.5
    f = lb + (1.0 - lb) * jax.nn.sigmoid(raw_ref[0, :, HG_W:2 * HG_W])
    kk = 1.0 - f
    v = raw_ref[0, :, 2 * HG_W:3 * HG_W]
    l1, l2, l3 = _split3(jnp.log(f))
    tin = tin_ref[...]
    tup = tup_ref[...]
    b = _dot(tin, l1) + _dot(tin, l2) + _dot(tin, l3)
    c = _dot(tup, l1) + _dot(tup, l2) + _dot(tup, l3)
    eb = jnp.exp(b)
    b_scr[...] = b
    q_scr[...] = q
    kk_scr[...] = kk
    v_scr[...] = v
    eb_scr[...] = eb
    qt_scr[...] = (q * eb).astype(BF16)
    kt_scr[...] = (kk * jnp.exp(c)).astype(BF16)
    vb_scr[...] = v.astype(BF16)

    rowid = lax.broadcasted_iota(jnp.int32, (SUB, HG_W), 0)

    def step(j, carry):
        r0 = pl.multiple_of(j * SUB, SUB)
        bj = b_scr[pl.ds(r0, SUB), :]
        qj = q_scr[pl.ds(r0, SUB), :]
        for s in range(SUB):
            bs = b_scr[pl.ds(r0 + s, 1), :]
            ks = kk_scr[pl.ds(r0 + s, 1), :]
            dec = jnp.exp(jnp.where(rowid >= s, bj - bs, -jnp.inf))
            p_scr[s * SUB:(s + 1) * SUB, :] = (qj * dec * ks).astype(BF16)
        r = _dot(p_scr[...], ones_ref[...])
        od = jnp.zeros((SUB, HG_W), F32)
        for s in range(SUB):
            od = od + r[s * SUB:(s + 1) * SUB, :] * v_scr[pl.ds(r0 + s, 1), :]
        qt = qt_scr[pl.ds(r0, SUB), :]
        kt = kt_scr[pl.ds(r0, SUB), :]
        vb = vb_scr[pl.ds(r0, SUB), :]
        dj = eb_scr[pl.ds(r0 + SUB - 1, 1), :]
        for h in range(HG_HEADS):
            hs = slice(h * HG_DK, (h + 1) * HG_DK)
            st = st_scr[h]
            oi = _dot_nt(qt[:, hs], st.astype(BF16))
            st_scr[h] = st * dj[:, hs] + _dot_tn(vb[:, hs], kt[:, hs])
            o_scr[pl.ds(r0, SUB), hs] = oi + od[:, hs]
        return carry

    lax.fori_loop(0, th // SUB, step, 0)

    gate = raw_ref[0, :, 3 * HG_W:4 * HG_W]
    gate = gate * jax.nn.sigmoid(gate)
    gain = gain_ref[...]
    for h in range(HG_HEADS):
        hs = slice(h * HG_DV, (h + 1) * HG_DV)
        o_out[0, :, hs] = (_rms(o_scr[:, hs], gain) * gate[:, hs]).astype(BF16)

    @pl.when(i == pl.num_programs(1) - 1)
    def _():
        for h in range(HG_HEADS):
            s_out[0, h] = st_scr[h].T


def _step_matrices(th):
    r = np.arange(th)
    same = (r[:, None] // SUB) == (r[None, :] // SUB)
    tin = (same & (r[None, :] <= r[:, None])).astype(np.float32)
    tup = (same & (r[None, :] > r[:, None])).astype(np.float32)
    hd = np.arange(HG_W) // HG_DK
    ones = (hd[:, None] == hd[None, :]).astype(np.float32)
    return jnp.asarray(tin, BF16), jnp.asarray(tup, BF16), jnp.asarray(ones, BF16)


def _hgrn(raw, lb_logits, gain, s0, th):
    B, L, _ = raw.shape
    tin, tup, ones = _step_matrices(th)
    f32s = lambda: pltpu.VMEM((th, HG_W), F32)
    b16s = lambda: pltpu.VMEM((th, HG_W), BF16)
    return pl.pallas_call(
        functools.partial(_hgrn_body, th=th),
        out_shape=[jax.ShapeDtypeStruct((B, L, HG_W), BF16),
                   jax.ShapeDtypeStruct((B, HG_HEADS, HG_DK, HG_DV), F32)],
        grid=(B, L // th),
        in_specs=[pl.BlockSpec((1, th, 4 * HG_W), lambda b, i: (b, i, 0)),
                  _const_spec(lb_logits.shape), _const_spec(gain.shape),
                  pl.BlockSpec((1, HG_HEADS, HG_DK, HG_DV), lambda b, i: (b, 0, 0, 0)),
                  _const_spec(tin.shape), _const_spec(tup.shape), _const_spec(ones.shape)],
        out_specs=[pl.BlockSpec((1, th, HG_W), lambda b, i: (b, i, 0)),
                   pl.BlockSpec((1, HG_HEADS, HG_DK, HG_DV), lambda b, i: (b, 0, 0, 0))],
        scratch_shapes=[pltpu.VMEM((HG_HEADS, HG_DV, HG_DK), F32),
                        f32s(), f32s(), f32s(), f32s(), f32s(),
                        b16s(), b16s(), b16s(),
                        pltpu.VMEM((SUB * SUB, HG_W), BF16), f32s()],
        compiler_params=pltpu.CompilerParams(dimension_semantics=("arbitrary", "arbitrary"),
                                             vmem_limit_bytes=VMEM_LIMIT),
        name="hgrn_scan",
    )(raw, lb_logits, gain, s0, tin, tup, ones)


def _unit_time():
    i = np.arange(UNIT)
    return SLABS * (i % SLAB_ROWS) + i // SLAB_ROWS


def _unit_tables():
    t = _unit_time()
    tin = (t[None, :] <= t[:, None]).astype(np.float32)
    rows = []
    for c in (1, 2, 4, 8):
        g = np.arange(SLAB_ROWS)
        t_bnd = SLABS * (2 * c * (g // (2 * c)) + c - 1) + SLABS - 1
        rows.append((t[None, :] <= t_bnd[:, None]).astype(np.float32))
    tin_ext = np.concatenate([tin] + rows, axis=0)
    x = t[:, None] ^ t[None, :]
    lv = np.where(x > 0, np.floor(np.log2(np.maximum(x, 1))), DIAG_LEVEL).astype(np.int32)
    lv = np.where(t[None, :] <= t[:, None], lv, -1)
    return jnp.asarray(tin_ext, BF16), jnp.asarray(np.concatenate([lv, lv], axis=1), jnp.int32)


def _pair_blockdiag(a, b):
    z = jnp.zeros_like(a)
    return jnp.concatenate([jnp.concatenate([a, z], axis=1), jnp.concatenate([z, b], axis=1)], axis=0)


def _hgrn_units_body(raw_ref, lbl_ref, gain_ref, s0_ref, tin_ref, lv_ref, unperm_ref, o_out, s_out, st_scr, *, nu):
    i = pl.program_id(1)

    @pl.when(i == 0)
    def _():
        for h in range(HG_HEADS):
            st_scr[h] = s0_ref[0, h].T

    lbl = lbl_ref[...]
    e = jnp.exp(lbl - jnp.max(lbl, axis=0, keepdims=True))
    lb = (e / jnp.sum(e, axis=0, keepdims=True))[0:1]
    tin = tin_ref[...]
    lv = lv_ref[...]
    gain = gain_ref[...]
    raw_w = 4 * HG_W

    units = range(nu)
    pairs = range(HG_HEADS // 2)

    def field(u, k):
        return jnp.concatenate([raw_ref[0, u, :, raw_w * p + HG_W * k:raw_w * p + HG_W * (k + 1)]
                                for p in range(SLABS)], axis=0)

    def pair_cols(x, hp):
        c0 = 2 * HG_DK * hp
        return x[:, c0:c0 + HG_DK], x[:, c0 + HG_DK:c0 + 2 * HG_DK]

    q = [field(u, 0) * HG_DK ** -0.5 for u in units]
    f = [lb + (1.0 - lb) * jax.nn.sigmoid(field(u, 1)) for u in units]
    kk = [1.0 - f[u] for u in units]
    lsplit = [_split3(jnp.log(f[u]) * np.float32(1.0 / np.log(2.0))) for u in units]
    bx = [_dot(tin, lsplit[u][0]) + _dot(tin, lsplit[u][1]) + _dot(tin, lsplit[u][2]) for u in units]
    b = [bx[u][0:UNIT] for u in units]
    bs = [[b[u][SLAB_ROWS * p:SLAB_ROWS * (p + 1)] for p in range(SLABS)] for u in units]
    vb = [field(u, 2).astype(BF16) for u in units]
    qb = [q[u].astype(BF16) for u in units]
    kkb = [kk[u].astype(BF16) for u in units]

    def slab_decay(u, level, p):
        if level < 3:
            half = 1 << level
            ref = p - p % (2 * half) + half - 1
            if p == ref:
                return jnp.ones_like(bs[u][p])
            return jnp.exp2(bs[u][p] - bs[u][ref] if p > ref else bs[u][ref] - bs[u][p])
        r = bx[u][UNIT + SLAB_ROWS * (level - 3):UNIT + SLAB_ROWS * (level - 2)]
        if level == N_LEVELS - 1:
            h = SLAB_ROWS // 2
            return jnp.concatenate([jnp.exp2(r[:h] - bs[u][p][:h]), jnp.exp2(bs[u][p][h:] - r[h:])], axis=0)
        return jnp.exp2(-jnp.abs(bs[u][p] - r))

    acc = [[jnp.zeros((UNIT, 2 * HG_DK), F32) for _ in pairs] for _ in units]
    for level in range(N_LEVELS + 1):
        here = lv == level
        for u in units:
            if level == DIAG_LEVEL:
                qt, kt = qb[u], kkb[u]
            else:
                dec = jnp.concatenate([slab_decay(u, level, p) for p in range(SLABS)], axis=0).astype(BF16)
                qt, kt = qb[u] * dec, kkb[u] * dec
            for hp in pairs:
                a2 = _dot_nt(qt[:, 2 * HG_DK * hp:2 * HG_DK * (hp + 1)], _pair_blockdiag(*pair_cols(kt, hp)))
                acc[u][hp] = jnp.where(here, a2, acc[u][hp])

    b_end = [b[u][UNIT - 1:UNIT] for u in units]
    qi = [(q[u] * jnp.exp2(b[u])).astype(BF16) for u in units]
    ks = [(kk[u] * jnp.exp2(b_end[u] - b[u])).astype(BF16) for u in units]
    d = [jnp.exp2(b_end[u]) for u in units]
    o_intra = [[_dot(acc[u][hp].astype(BF16), _pair_blockdiag(*pair_cols(vb[u], hp))) for hp in pairs]
               for u in units]
    ds = [[_dot_tn(vb[u][:, h * HG_DV:(h + 1) * HG_DV], ks[u][:, h * HG_DK:(h + 1) * HG_DK])
           for h in range(HG_HEADS)] for u in units]
    st = [st_scr[h] for h in range(HG_HEADS)]
    o_inter = []
    for u in units:
        o_inter.append([_dot_nt(qi[u][:, 2 * HG_DK * hp:2 * HG_DK * (hp + 1)],
                                _pair_blockdiag(st[2 * hp].astype(BF16), st[2 * hp + 1].astype(BF16)))
                        for hp in pairs])
        st = [st[h] * d[u][:, h * HG_DK:(h + 1) * HG_DK] + ds[u][h] for h in range(HG_HEADS)]
    for h in range(HG_HEADS):
        st_scr[h] = st[h]
    res = []
    for u in units:
        o = jnp.concatenate([o_intra[u][hp] + o_inter[u][hp] for hp in pairs], axis=1)
        gate = field(u, 3)
        gate = gate * jax.nn.sigmoid(gate)
        r = jnp.concatenate([_rms(o[:, h * HG_DV:(h + 1) * HG_DV], gain) for h in range(HG_HEADS)], axis=1)
        res.append((r * gate).astype(BF16))
    for u in units:
        o_out[0, UNIT * u:UNIT * (u + 1), :] = _dot(unperm_ref[...], res[u]).astype(BF16)

    @pl.when(i == pl.num_programs(1) - 1)
    def _():
        for h in range(HG_HEADS):
            s_out[0, h] = st_scr[h].T


def _hgrn_units(raw4, lb_logits, gain, s0, nu):
    B, n_units, _, w8 = raw4.shape
    L = n_units * UNIT
    tin, lv = _unit_tables()
    unperm = _unit_perm(UNIT).T
    return pl.pallas_call(
        functools.partial(_hgrn_units_body, nu=nu),
        out_shape=[jax.ShapeDtypeStruct((B, L, HG_W), BF16),
                   jax.ShapeDtypeStruct((B, HG_HEADS, HG_DK, HG_DV), F32)],
        grid=(B, n_units // nu),
        in_specs=[pl.BlockSpec((1, nu, SLAB_ROWS, w8), lambda b, i: (b, i, 0, 0)),
                  _const_spec(lb_logits.shape), _const_spec(gain.shape),
                  pl.BlockSpec((1, HG_HEADS, HG_DK, HG_DV), lambda b, i: (b, 0, 0, 0)),
                  _const_spec(tin.shape), _const_spec(lv.shape), _const_spec(unperm.shape)],
        out_specs=[pl.BlockSpec((1, nu * UNIT, HG_W), lambda b, i: (b, i, 0)),
                   pl.BlockSpec((1, HG_HEADS, HG_DK, HG_DV), lambda b, i: (b, 0, 0, 0))],
        scratch_shapes=[pltpu.VMEM((HG_HEADS, HG_DV, HG_DK), F32)],
        compiler_params=pltpu.CompilerParams(dimension_semantics=("arbitrary", "arbitrary"),
                                             vmem_limit_bytes=VMEM_LIMIT),
        name="hgrn_units",
    )(raw4, lb_logits, gain, s0, tin, lv, unperm)


def _attn_body(q_ref, k_ref, vt_ref, o_ref, acc_scr, *, tq, nq):
    for qi in range(nq):
        q = q_ref[0, qi * tq:(qi + 1) * tq, :]
        nblk = qi + 1
        scores = lambda i: _dot_nt(k_ref[0, i * tq:(i + 1) * tq, :], q)
        pending = [scores(i) for i in range(min(ATTN_LOOKAHEAD, nblk))]
        m = jnp.full((1, tq), -jnp.inf, F32)
        l = jnp.zeros((1, tq), F32)
        for ki in range(nblk):
            st = pending.pop(0)
            if ki + ATTN_LOOKAHEAD < nblk:
                pending.append(scores(ki + ATTN_LOOKAHEAD))
            if ki == nblk - 1:
                kc = lax.broadcasted_iota(jnp.int32, st.shape, 0) // CHUNK
                qc = lax.broadcasted_iota(jnp.int32, st.shape, 1) // CHUNK
                st = jnp.where(kc <= qc, st, -jnp.inf)
            m_new = jnp.maximum(m, jnp.max(st, axis=0, keepdims=True))
            p = jnp.exp2(st - m_new)
            alpha = jnp.exp2(m - m_new)
            l = alpha * l + jnp.sum(p, axis=0, keepdims=True)
            pv = _dot(vt_ref[0, :, ki * tq:(ki + 1) * tq], p.astype(BF16))
            if ki == 0:
                acc_scr[qi % 2] = pv
            else:
                acc_scr[qi % 2] = alpha * acc_scr[qi % 2] + pv
            m = m_new
        o_ref[0, qi * tq:(qi + 1) * tq, :] = (acc_scr[qi % 2] / l).T.astype(BF16)


def _attn(q, k, vt, tq):
    B, L, _ = q.shape
    return pl.pallas_call(
        functools.partial(_attn_body, tq=tq, nq=L // tq),
        out_shape=jax.ShapeDtypeStruct((B, L, MLA_HEADS * MLA_V), BF16),
        grid=(B, MLA_HEADS),
        in_specs=[pl.BlockSpec((1, L, QK_W), lambda b, h: (b, 0, h)),
                  pl.BlockSpec((1, L, QK_W), lambda b, h: (b, 0, h)),
                  pl.BlockSpec((1, MLA_V, L), lambda b, h: (b, h, 0))],
        out_specs=pl.BlockSpec((1, L, MLA_V), lambda b, h: (b, 0, h)),
        scratch_shapes=[pltpu.VMEM((2, MLA_V, tq), F32)],
        compiler_params=pltpu.CompilerParams(dimension_semantics=("arbitrary",) * 2,
                                             vmem_limit_bytes=VMEM_LIMIT),
        name="mla_attn_prompt",
    )(q, k, vt)


def _attn_cache_body(q_ref, kn_ref, vn_ref, clat_ref, ckpe_ref, wuk_ref, wuv_ref, place_ref, o_ref):
    n = q_ref.shape[1]
    latc = clat_ref[0].astype(BF16)
    kpec = ckpe_ref[0].astype(BF16)
    qlat, qrope, s_new = [], [], []
    for h in range(MLA_HEADS):
        qh = q_ref[0, :, QK_W * h:QK_W * (h + 1)]
        ws = slice(MLA_NOPE * h, MLA_NOPE * (h + 1))
        qlat.append(_dot_nt(qh[:, 0:MLA_NOPE], wuk_ref[:, ws]).astype(BF16))
        qrope.append(_dot_nt(qh[:, MLA_NOPE:QK_W], place_ref[:, ws]).astype(BF16))
        s_new.append(_dot_nt(qh, kn_ref[0, :, QK_W * h:QK_W * (h + 1)]))
    s_new = jnp.concatenate(s_new, axis=0)
    s_c = (_dot_nt(jnp.concatenate(qlat, axis=0), latc)
           + _dot(jnp.concatenate(qrope, axis=0), kpec))
    m = jnp.maximum(jnp.max(s_c, axis=-1, keepdims=True), jnp.max(s_new, axis=-1, keepdims=True))
    p_c = jnp.exp2(s_c - m)
    p_new = jnp.exp2(s_new - m)
    l = jnp.sum(p_c, axis=-1, keepdims=True) + jnp.sum(p_new, axis=-1, keepdims=True)
    o_lat = _dot(p_c.astype(BF16), latc).astype(BF16)
    p_new = p_new.astype(BF16)
    for h in range(MLA_HEADS):
        ws = slice(MLA_V * h, MLA_V * (h + 1))
        rs = slice(n * h, n * (h + 1))
        acc = _dot(o_lat[rs], wuv_ref[:, ws]) + _dot_nt(p_new[rs], vn_ref[0, ws, :])
        o_ref[0, :, ws] = (acc / l[rs]).astype(BF16)


def _rope_place():
    p = np.zeros((MLA_ROPE, MLA_HEADS * MLA_NOPE), np.float32)
    for h in range(MLA_HEADS):
        for i in range(ROPE_HALF):
            p[i, MLA_NOPE * h + ROPE_HALF * h + i] = 1.0
            p[ROPE_HALF + i, MLA_NOPE * h + ROPE_HALF * ((h + 1) % MLA_HEADS) + i] = 1.0
    return jnp.asarray(p, BF16)


def _attn_cache(q, kn, vn, clat, ckpe, wuk, wuv):
    B, L, _ = q.shape
    P = clat.shape[1]
    place = _rope_place()
    bspec = lambda n, w: pl.BlockSpec((1, n, w), lambda b: (b, 0, 0))
    return pl.pallas_call(
        _attn_cache_body,
        out_shape=jax.ShapeDtypeStruct((B, L, MLA_HEADS * MLA_V), BF16),
        grid=(B,),
        in_specs=[bspec(L, MLA_HEADS * QK_W), bspec(L, MLA_HEADS * QK_W), bspec(MLA_HEADS * MLA_V, L),
                  bspec(P, KV_LORA), bspec(MLA_ROPE, P),
                  _const_spec(wuk.shape), _const_spec(wuv.shape), _const_spec(place.shape)],
        out_specs=bspec(L, MLA_HEADS * MLA_V),
        compiler_params=pltpu.CompilerParams(dimension_semantics=("arbitrary",),
                                             vmem_limit_bytes=VMEM_LIMIT),
        name="mla_attn_cache",
    )(q, kn, vn, clat, ckpe, wuk, wuv, place)


def _ffn_body(x_ref, ohg_ref, omla_ref, mod_ref, conv0_ref, wout_ref, gffn_ref, wup_ref, cw_ref, cb_ref,
              wdn_ref, gfin_ref, y_ref, conv_out, carry_scr, a_scr, u_scr, *, tm, ft):
    i = pl.program_id(1)
    last = pl.num_programs(1) - 1

    @pl.when(i == 0)
    def _():
        carry_scr[...] = conv0_ref[0]

    mod = mod_ref[0]
    g1 = mod[2:3]
    sh2 = mod[3:4]
    sc2 = mod[4:5]
    g2 = mod[5:6]
    o = _dot(ohg_ref[0], wout_ref[0:HG_W, :]) + _dot(omla_ref[0], wout_ref[HG_W:2 * HG_W, :])
    x1 = x_ref[0] + g1 * o
    h2 = (_rms(x1, gffn_ref[...]) * (1.0 + sc2) + sh2).astype(BF16)
    for j in range(D_FF // ft):
        cs = slice(j * ft, (j + 1) * ft)
        a = _dot(h2, wup_ref[:, cs])
        v = _dot(h2, wup_ref[:, D_FF + j * ft:D_FF + (j + 1) * ft])
        a_scr[0:CARRY_ROWS, :] = carry_scr[:, cs]
        a_scr[CARRY_ROWS:CARRY_ROWS + tm, :] = a
        carry_scr[:, cs] = a[tm - CARRY_ROWS:tm]
        cw = cw_ref[:, cs]
        conv = (cb_ref[:, cs] + cw[0:1] * a_scr[CARRY_ROWS - 2:CARRY_ROWS - 2 + tm, :]
                + cw[1:2] * a_scr[CARRY_ROWS - 1:CARRY_ROWS - 1 + tm, :] + cw[2:3] * a)
        g = 0.5 * conv * (1.0 + lax.erf(conv * np.float32(np.sqrt(0.5))))
        u_scr[:, cs] = (g * v).astype(BF16)

        @pl.when(i == last)
        def _():
            conv_out[0, :, cs] = a[tm - (CONV_W - 1):tm]

    x2 = x1 + g2 * _dot(u_scr[...], wdn_ref[...])
    y_ref[0] = _rms(x2, gfin_ref[...])


def _ffn(x, ohg, omla, mod, conv0, wout, gffn, wup, cw, cb, wdn, gfin, tm, ft):
    B, L, D = x.shape
    row = lambda w: pl.BlockSpec((1, tm, w), lambda b, i: (b, i, 0))
    once = lambda a: pl.BlockSpec(a.shape, lambda b, i: (0,) * a.ndim, pipeline_mode=pl.Buffered(1))
    return pl.pallas_call(
        functools.partial(_ffn_body, tm=tm, ft=ft),
        out_shape=[jax.ShapeDtypeStruct((B, L, D), F32),
                   jax.ShapeDtypeStruct((B, CONV_W - 1, D_FF), F32)],
        grid=(B, L // tm),
        in_specs=[row(D), row(HG_W), row(MLA_HEADS * MLA_V),
                  pl.BlockSpec((1, 6, D), lambda b, i: (b, 0, 0)),
                  pl.BlockSpec((1, CARRY_ROWS, D_FF), lambda b, i: (b, 0, 0)),
                  once(wout), once(gffn), once(wup), once(cw), once(cb), once(wdn), once(gfin)],
        out_specs=[row(D), pl.BlockSpec((1, CONV_W - 1, D_FF), lambda b, i: (b, 0, 0))],
        scratch_shapes=[pltpu.VMEM((CARRY_ROWS, D_FF), F32), pltpu.VMEM((CARRY_ROWS + tm, ft), F32),
                        pltpu.VMEM((tm, D_FF), BF16)],
        compiler_params=pltpu.CompilerParams(dimension_semantics=("arbitrary", "arbitrary"),
                                             vmem_limit_bytes=VMEM_LIMIT),
        name="out_ffn",
    )(x, ohg, omla, mod, conv0, wout, gffn, wup, cw, cb, wdn, gfin)


def _prep_weights(w_in, mla_q_norm_gain, mla_kv_norm_gain, w_uq, w_uk, w_uv):
    win = w_in.astype(BF16)
    kp = win[:, 4 * HG_W + Q_LORA + KV_LORA:]
    wkx = jnp.concatenate([jnp.tile(kp[:, :ROPE_HALF], (1, MLA_HEADS)),
                           jnp.tile(kp[:, ROPE_HALF:], (1, MLA_HEADS))], axis=1)
    uq = w_uq.reshape(Q_LORA, MLA_HEADS, MLA_NOPE + MLA_ROPE)
    wuqn = uq[:, :, :MLA_NOPE].reshape(Q_LORA, MLA_HEADS * MLA_NOPE).astype(BF16)
    x1a = uq[:, :, MLA_NOPE:MLA_NOPE + ROPE_HALF].reshape(Q_LORA, MLA_HEADS * ROPE_HALF)
    x2a = uq[:, :, MLA_NOPE + ROPE_HALF:].reshape(Q_LORA, MLA_HEADS * ROPE_HALF)
    wuqr = jnp.concatenate([x1a, x2a, jnp.roll(x1a, ROPE_HALF, axis=1), jnp.roll(x2a, ROPE_HALF, axis=1)],
                           axis=1).astype(BF16)
    wuk = w_uk.reshape(KV_LORA, MLA_HEADS * MLA_NOPE).astype(BF16)
    wuv = w_uv.reshape(KV_LORA, MLA_HEADS * MLA_V).astype(BF16)
    return (win, wkx, kp.T, mla_q_norm_gain.reshape(1, -1), mla_kv_norm_gain.reshape(1, -1), wuqn, wuqr, wuk, wuv)


def _rope_tables(pos):
    inv_freq = ROPE_THETA ** (-jnp.arange(ROPE_HALF, dtype=F32) / ROPE_HALF)
    ang = pos.astype(F32)[:, None] * inv_freq[None, :]
    cos, sin = jnp.cos(ang), jnp.sin(ang)
    return jnp.tile(cos, (1, MLA_HEADS)), jnp.tile(sin, (1, MLA_HEADS)), cos.T, sin.T


def _tile(n, pref):
    return pref if n % pref == 0 else n


def kernel(x_prompt, x_sample, c_prompt, c_sample, cache_kv_latent, cache_k_rope, state_hgrn, state_ffn_conv, w_ada, b_ada, norm_mix_gain, w_in, hg_lb_logits, hg_norm_gain, mla_q_norm_gain, mla_kv_norm_gain, w_uq, w_uk, w_uv, w_out, norm_ffn_gain, w_up, conv_w, conv_b, w_down, final_norm_gain):
    assert w_ada.shape[0] == 1, "single-layer trunk"
    B, L, D = x_prompt.shape
    Bs, Ls, _ = x_sample.shape
    past = cache_kv_latent.shape[2]

    mod = _ada(jnp.concatenate([c_prompt, c_sample], axis=0), w_ada[0], b_ada)
    mod = mod.reshape(B + Bs, 6, D)
    wts = _prep_weights(w_in[0], mla_q_norm_gain[0], mla_kv_norm_gain[0], w_uq[0], w_uk[0], w_uv[0])
    gmix = norm_mix_gain
    wout = w_out[0].astype(BF16)
    wup = w_up[0].astype(BF16)
    wdn = w_down[0].astype(BF16)
    gfin = final_norm_gain.reshape(1, D)
    ffn_w = (wout, norm_ffn_gain, wup, conv_w[0], conv_b, wdn, gfin)

    def layer(x, mod_x, pos, s0, conv0, cache):
        n, l, _ = x.shape
        rope = _rope_tables(pos)
        units = l % UNIT == 0
        raw, q, k, v, lat, kpe = _inproj(x, mod_x, gmix, wts, rope, _tile(l, 512), units)
        if units:
            nu = max(n for n in (4, 2, 1) if (l // UNIT) % n == 0)
            ohg, s_new = _hgrn_units(raw, hg_lb_logits, hg_norm_gain, s0, nu)
        else:
            ohg, s_new = _hgrn(raw, hg_lb_logits, hg_norm_gain, s0, _tile(l, 256))
        if cache is None:
            omla = _attn(q, k, v, _tile(l, 512))
        else:
            omla = _attn_cache(q, k, v, cache[0], cache[1], wts[7], wts[8])
        y, conv_new = _ffn(x, ohg, omla, mod_x, conv0, *ffn_w, _tile(l, 512), FF_TILE)
        return y, lat[None], jnp.swapaxes(kpe, 1, 2)[None], s_new[None], conv_new[None]

    zeros_state = jnp.zeros((B, HG_HEADS, HG_DK, HG_DV), F32)
    zeros_conv = jnp.zeros((B, CARRY_ROWS, D_FF), F32)
    conv0_s = jnp.pad(state_ffn_conv[0], ((0, 0), (CARRY_ROWS - (CONV_W - 1), 0), (0, 0)))
    yp, latp, kpep, hgp, cvp = layer(x_prompt, mod[:B], jnp.arange(L), zeros_state, zeros_conv, None)
    ys, lats, kpes, hgs, cvs = layer(x_sample, mod[B:], past + jnp.arange(Ls), state_hgrn[0], conv0_s,
                                     (cache_kv_latent[0], jnp.swapaxes(cache_k_rope[0], 1, 2)))
    return (yp, ys, latp, kpep, hgp, cvp, lats, kpes, hgs, cvs)
```

```python
import functools

import numpy as np
import jax
import jax.numpy as jnp
from jax import lax
from jax.experimental import pallas as pl
from jax.experimental.pallas import tpu as pltpu

F32 = jnp.float32
BF16 = jnp.bfloat16

D_MODEL = 1024
CHUNK = 64
HG_HEADS = 4
HG_DK = 128
HG_DV = 128
HG_W = HG_HEADS * HG_DK
MLA_HEADS = 4
MLA_NOPE = 128
MLA_ROPE = 64
ROPE_HALF = MLA_ROPE // 2
MLA_V = 128
Q_LORA = 384
KV_LORA = 256
ROPE_THETA = 10000.0
MLA_SCALE = (MLA_NOPE + MLA_ROPE) ** -0.5
Q_SCALE = MLA_SCALE * float(np.log2(np.e))
QK_W = 2 * MLA_NOPE
D_FF = 2816
CONV_W = 3
EPS = 1e-6

SUB = 16
FF_TILE = D_FF
ATTN_LOOKAHEAD = 4
CARRY_ROWS = 8
VMEM_LIMIT = 56 * 1024 * 1024
UNIT = 128
SLABS = 8
SLAB_ROWS = UNIT // SLABS
N_LEVELS = 7
DIAG_LEVEL = N_LEVELS


def _dot(a, b):
    return jnp.dot(a, b, preferred_element_type=F32)


def _dot_nt(a, b):
    return lax.dot_general(a, b, (((1,), (1,)), ((), ())), preferred_element_type=F32)


def _dot_tn(a, b):
    return lax.dot_general(a, b, (((0,), (0,)), ((), ())), preferred_element_type=F32)


def _split3(x):
    x1 = x.astype(BF16)
    r1 = x - x1.astype(F32)
    x2 = r1.astype(BF16)
    r2 = r1 - x2.astype(F32)
    return x1, x2, r2.astype(BF16)


def _rms(x, g):
    return x * lax.rsqrt(jnp.mean(x * x, axis=-1, keepdims=True) + EPS) * g


def _const_spec(shape):
    n = len(shape)
    return pl.BlockSpec(shape, lambda *_: (0,) * n)


def _ada_body(c_ref, w_ref, b_ref, o_ref):
    c = c_ref[...]
    s = c * jax.nn.sigmoid(c)
    s1, s2, _ = _split3(s)
    w = w_ref[...]
    w1 = w.astype(BF16)
    w2 = (w - w1.astype(F32)).astype(BF16)
    o_ref[...] = _dot(s1, w1) + _dot(s1, w2) + _dot(s2, w1) + b_ref[...]


def _ada(c, w_ada, b_ada):
    n, d = c.shape
    nout = w_ada.shape[1]
    tn = 1024
    return pl.pallas_call(
        _ada_body,
        out_shape=jax.ShapeDtypeStruct((n, nout), F32),
        grid=(nout // tn,),
        in_specs=[pl.BlockSpec((n, d), lambda j: (0, 0)),
                  pl.BlockSpec((d, tn), lambda j: (0, j)),
                  pl.BlockSpec((1, tn), lambda j: (0, j))],
        out_specs=pl.BlockSpec((n, tn), lambda j: (0, j)),
        compiler_params=pltpu.CompilerParams(dimension_semantics=("arbitrary",)),
        name="adaln_mod",
    )(c, w_ada, b_ada)


def _rope_slots(r1, r2, nope, out_ref, slot):
    for h in range(MLA_HEADS):
        xh = jnp.where(slot == h, r1, jnp.where(slot == (h + 1) % MLA_HEADS, r2, 0.0))
        out_ref[0, :, QK_W * h:QK_W * h + MLA_NOPE] = nope[:, MLA_NOPE * h:MLA_NOPE * (h + 1)].astype(BF16)
        out_ref[0, :, QK_W * h + MLA_NOPE:QK_W * (h + 1)] = xh.astype(BF16)


def _mla_stages(hb, refs, outs, rope_refs):
    (wcq_ref, wckv_ref, wkx_ref, wkpt_ref, qng_ref, kvng_ref, wuqn_ref, wuqr_ref, wuk_ref, wuvt_ref) = refs
    q_out, k_out, v_out, lat_out, kpe_out = outs
    cos_ref, sin_ref, cost_ref, sint_ref = rope_refs
    t = {}

    def project():
        t["cq"] = _dot(hb, wcq_ref[...])
        t["ckv"] = _dot(hb, wckv_ref[...])
        t["kx"] = _dot(hb, wkx_ref[...])
        t["kxt"] = _dot_nt(wkpt_ref[...], hb)

    def q_up():
        cq = _rms(t["cq"], qng_ref[...]).astype(BF16)
        t["qn"] = _dot(cq, wuqn_ref[...]) * Q_SCALE
        t["qr"] = _dot(cq, wuqr_ref[...]) * Q_SCALE

    def kv_up():
        lat = _rms(t["ckv"], kvng_ref[...])
        lat_out[0] = lat
        latb = lat.astype(BF16)
        t["kn"] = _dot(latb, wuk_ref[...])
        v_out[0] = _dot_nt(wuvt_ref[...], latb).astype(BF16)

    def slot_of(cos):
        return lax.broadcasted_iota(jnp.int32, cos.shape, 1) // ROPE_HALF

    def q_rope():
        cos, sin, qr = cos_ref[...], sin_ref[...], t["qr"]
        qr1 = qr[:, 0:128] * cos - qr[:, 128:256] * sin
        qr2 = qr[:, 384:512] * cos + qr[:, 256:384] * sin
        _rope_slots(qr1, qr2, t["qn"], q_out, slot_of(cos))

    def k_rope():
        cos, sin, kx = cos_ref[...], sin_ref[...], t["kx"]
        kr1 = kx[:, 0:128] * cos - kx[:, 128:256] * sin
        kr2 = kx[:, 128:256] * cos + kx[:, 0:128] * sin
        _rope_slots(kr1, kr2, t["kn"], k_out, slot_of(cos))
        kxt = t["kxt"]
        x1t, x2t = kxt[0:ROPE_HALF], kxt[ROPE_HALF:MLA_ROPE]
        cost, sint = cost_ref[...], sint_ref[...]
        kpe_out[0] = jnp.concatenate([x1t * cost - x2t * sint, x2t * cost + x1t * sint], axis=0)

    return [project, q_up, kv_up, q_rope, k_rope]


def _normed_input(x_ref, mod_ref, gmix_ref):
    mod = mod_ref[0]
    return (_rms(x_ref[0], gmix_ref[...]) * (1.0 + mod[1:2]) + mod[0:1]).astype(BF16)


def _win_views(win_ref):
    hgw = 4 * HG_W
    return (win_ref.at[:, 0:hgw], win_ref.at[:, hgw:hgw + Q_LORA],
            win_ref.at[:, hgw + Q_LORA:hgw + Q_LORA + KV_LORA])


def _inproj_body(x_ref, mod_ref, gmix_ref, win_ref, wkx_ref, wkpt_ref, qng_ref, kvng_ref,
                 wuqn_ref, wuqr_ref, wuk_ref, wuvt_ref, cos_ref, sin_ref, cost_ref, sint_ref,
                 hg_out, q_out, k_out, v_out, lat_out, kpe_out):
    whg_ref, wcq_ref, wckv_ref = _win_views(win_ref)
    hb = _normed_input(x_ref, mod_ref, gmix_ref)
    hg_out[0] = _dot(hb, whg_ref[...])
    for stage in _mla_stages(hb, (wcq_ref, wckv_ref, wkx_ref, wkpt_ref, qng_ref, kvng_ref, wuqn_ref, wuqr_ref,
                                  wuk_ref, wuvt_ref),
                             (q_out, k_out, v_out, lat_out, kpe_out), (cos_ref, sin_ref, cost_ref, sint_ref)):
        stage()


def _inproj_hgrn_body(x_ref, mod_ref, gmix_ref, win_ref, wkx_ref, wkpt_ref, qng_ref, kvng_ref,
                      wuqn_ref, wuqr_ref, wuk_ref, wuvt_ref, cos_ref, sin_ref, cost_ref, sint_ref,
                      perm_ref, lbl_ref, gain_ref, s0t_ref, tin_ref, lv_ref, unperm_ref,
                      hg_out, st_out, q_out, k_out, v_out, lat_out, kpe_out, raw_scr, st_scr, *, nt):
    j = pl.program_id(0)

    @pl.when(j == 0)
    def _():
        raw_scr[...] = jnp.zeros(raw_scr.shape, F32)

    first_of_batch = lax.rem(jnp.maximum(j - 1, 0), nt) == 0
    st = [jnp.where(first_of_batch, s0t_ref[0, hd], st_scr[hd]) for hd in range(HG_HEADS)]

    whg_ref, wcq_ref, wckv_ref = _win_views(win_ref)
    n_units = x_ref.shape[1] // UNIT
    t = {}

    def start():
        hb = _normed_input(x_ref, mod_ref, gmix_ref)
        t["hbp"] = jnp.concatenate([_dot(perm_ref[...], hb[UNIT * u:UNIT * (u + 1)]).astype(BF16)
                                    for u in range(n_units)], axis=0)
        t["mla"] = _mla_stages(hb, (wcq_ref, wckv_ref, wkx_ref, wkpt_ref, qng_ref, kvng_ref, wuqn_ref,
                                    wuqr_ref, wuk_ref, wuvt_ref),
                               (q_out, k_out, v_out, lat_out, kpe_out), (cos_ref, sin_ref, cost_ref, sint_ref))

    def raw_group(k):
        def run():
            raw_scr[:, HG_W * k:HG_W * (k + 1)] = _dot(t["hbp"], whg_ref[:, HG_W * k:HG_W * (k + 1)])
        return run

    chunks = [start] + [raw_group(k) for k in range(4)] + [lambda n=n: t["mla"][n]() for n in range(5)]

    def next_chunk(_level):
        if chunks:
            chunks.pop(0)()

    st = _hgrn_units_math(lambda u, k: raw_scr[UNIT * u:UNIT * (u + 1), HG_W * k:HG_W * (k + 1)], n_units, st,
                          lbl_ref, gain_ref, tin_ref, lv_ref, unperm_ref, hg_out, next_chunk)
    while chunks:
        chunks.pop(0)()
    for hd in range(HG_HEADS):
        st_scr[hd] = st[hd]
        st_out[0, hd] = st[hd]


def _unit_perm(n):
    i = np.arange(n)
    src = UNIT * (i // UNIT) + _unit_time()[i % UNIT]
    return jnp.asarray(src[:, None] == i[None, :], BF16)


def _inproj_args(x, mod, gmix, wts, rope):
    (win, wkx, wkpt, qng, kvng, wuqn, wuqr, wuk, wuv) = wts
    return [x, mod, gmix, win, wkx, wkpt, qng, kvng, wuqn, wuqr, wuk, wuv.T, *rope]


def _inproj_out_shapes(B, L):
    return [((B, L, MLA_HEADS * QK_W), BF16), ((B, L, MLA_HEADS * QK_W), BF16), ((B, MLA_HEADS * MLA_V, L), BF16),
            ((B, L, KV_LORA), F32), ((B, MLA_ROPE, L), F32)]


def _inproj(x, mod, gmix, wts, rope, tm):
    B, L, D = x.shape
    args = _inproj_args(x, mod, gmix, wts, rope)
    row = lambda w: pl.BlockSpec((1, tm, w), lambda b, i: (b, i, 0))
    col = lambda h: pl.BlockSpec((1, h, tm), lambda b, i: (b, 0, i))
    in_specs = [row(D), pl.BlockSpec((1, 6, D), lambda b, i: (b, 0, 0))]
    in_specs += [_const_spec(w.shape) for w in args[2:12]]
    in_specs += [pl.BlockSpec((tm, 128), lambda b, i: (i, 0))] * 2
    in_specs += [pl.BlockSpec((ROPE_HALF, tm), lambda b, i: (0, i))] * 2
    shapes = [((B, L, 4 * HG_W), F32)] + _inproj_out_shapes(B, L)
    out_specs = [row(4 * HG_W), row(MLA_HEADS * QK_W), row(MLA_HEADS * QK_W), col(MLA_HEADS * MLA_V),
                 row(KV_LORA), col(MLA_ROPE)]
    return pl.pallas_call(
        _inproj_body,
        out_shape=[jax.ShapeDtypeStruct(s, dt) for s, dt in shapes],
        grid=(B, L // tm),
        in_specs=in_specs,
        out_specs=out_specs,
        compiler_params=pltpu.CompilerParams(dimension_semantics=("arbitrary", "arbitrary"),
                                             vmem_limit_bytes=VMEM_LIMIT),
        name="in_proj",
    )(*args)


def _inproj_hgrn(x, mod, gmix, wts, rope, lb_logits, gain, s0, tm):
    B, L, D = x.shape
    nt = L // tm
    n = B * nt
    args = _inproj_args(x, mod, gmix, wts, rope)
    tin, lv = _unit_tables()
    perm = _unit_perm(UNIT)
    args += [perm, lb_logits, gain, jnp.swapaxes(s0, 2, 3), tin, lv, perm.T]
    cur = lambda j: jnp.minimum(j, n - 1)
    lag = lambda j: jnp.maximum(j - 1, 0)
    row = lambda w, t: pl.BlockSpec((1, tm, w), lambda j: (t(j) // nt, t(j) % nt, 0))
    col = lambda h: pl.BlockSpec((1, h, tm), lambda j: (cur(j) // nt, 0, cur(j) % nt))
    state = pl.BlockSpec((1, HG_HEADS, HG_DV, HG_DK), lambda j: (lag(j) // nt, 0, 0, 0))
    in_specs = [row(D, cur), pl.BlockSpec((1, 6, D), lambda j: (cur(j) // nt, 0, 0))]
    in_specs += [_const_spec(w.shape) for w in args[2:12]]
    in_specs += [pl.BlockSpec((tm, 128), lambda j: (cur(j) % nt, 0))] * 2
    in_specs += [pl.BlockSpec((ROPE_HALF, tm), lambda j: (0, cur(j) % nt))] * 2
    in_specs += [_const_spec(perm.shape), _const_spec(lb_logits.shape), _const_spec(gain.shape), state,
                 _const_spec(tin.shape), _const_spec(lv.shape), _const_spec(perm.shape)]
    shapes = [((B, L, HG_W), BF16), ((B, HG_HEADS, HG_DV, HG_DK), F32)] + _inproj_out_shapes(B, L)
    out_specs = [row(HG_W, lag), state, row(MLA_HEADS * QK_W, cur), row(MLA_HEADS * QK_W, cur),
                 col(MLA_HEADS * MLA_V), row(KV_LORA, cur), col(MLA_ROPE)]
    ohg, st, *mla = pl.pallas_call(
        functools.partial(_inproj_hgrn_body, nt=nt),
        out_shape=[jax.ShapeDtypeStruct(s, dt) for s, dt in shapes],
        grid=(n + 1,),
        in_specs=in_specs,
        out_specs=out_specs,
        scratch_shapes=[pltpu.VMEM((tm, 4 * HG_W), F32), pltpu.VMEM((HG_HEADS, HG_DV, HG_DK), F32)],
        compiler_params=pltpu.CompilerParams(dimension_semantics=("arbitrary",),
                                             vmem_limit_bytes=VMEM_LIMIT),
        name="in_proj_hgrn",
    )(*args)
    return (ohg, jnp.swapaxes(st, 2, 3), *mla)


def _hgrn_body(raw_ref, lbl_ref, gain_ref, s0_ref, tin_ref, tup_ref, ones_ref,
               o_out, s_out,
               st_scr, b_scr, q_scr, kk_scr, v_scr, eb_scr, qt_scr, kt_scr, vb_scr, p_scr, o_scr, *, th):
    i = pl.program_id(1)

    @pl.when(i == 0)
    def _():
        for h in range(HG_HEADS):
            st_scr[h] = s0_ref[0, h].T

    lbl = lbl_ref[...]
    e = jnp.exp(lbl - jnp.max(lbl, axis=0, keepdims=True))
    lb = (e / jnp.sum(e, axis=0, keepdims=True))[0:1]

    q = raw_ref[0, :, 0:HG_W] * HG_DK ** -0.5
    f = lb + (1.0 - lb) * jax.nn.sigmoid(raw_ref[0, :, HG_W:2 * HG_W])
    kk = 1.0 - f
    v = raw_ref[0, :, 2 * HG_W:3 * HG_W]
    l1, l2, l3 = _split3(jnp.log(f))
    tin = tin_ref[...]
    tup = tup_ref[...]
    b = _dot(tin, l1) + _dot(tin, l2) + _dot(tin, l3)
    c = _dot(tup, l1) + _dot(tup, l2) + _dot(tup, l3)
    eb = jnp.exp(b)
    b_scr[...] = b
    q_scr[...] = q
    kk_scr[...] = kk
    v_scr[...] = v
    eb_scr[...] = eb
    qt_scr[...] = (q * eb).astype(BF16)
    kt_scr[...] = (kk * jnp.exp(c)).astype(BF16)
    vb_scr[...] = v.astype(BF16)

    rowid = lax.broadcasted_iota(jnp.int32, (SUB, HG_W), 0)

    def step(j, carry):
        r0 = pl.multiple_of(j * SUB, SUB)
        bj = b_scr[pl.ds(r0, SUB), :]
        qj = q_scr[pl.ds(r0, SUB), :]
        for s in range(SUB):
            bs = b_scr[pl.ds(r0 + s, 1), :]
            ks = kk_scr[pl.ds(r0 + s, 1), :]
            dec = jnp.exp(jnp.where(rowid >= s, bj - bs, -jnp.inf))
            p_scr[s * SUB:(s + 1) * SUB, :] = (qj * dec * ks).astype(BF16)
        r = _dot(p_scr[...], ones_ref[...])
        od = jnp.zeros((SUB, HG_W), F32)
        for s in range(SUB):
            od = od + r[s * SUB:(s + 1) * SUB, :] * v_scr[pl.ds(r0 + s, 1), :]
        qt = qt_scr[pl.ds(r0, SUB), :]
        kt = kt_scr[pl.ds(r0, SUB), :]
        vb = vb_scr[pl.ds(r0, SUB), :]
        dj = eb_scr[pl.ds(r0 + SUB - 1, 1), :]
        for h in range(HG_HEADS):
            hs = slice(h * HG_DK, (h + 1) * HG_DK)
            st = st_scr[h]
            oi = _dot_nt(qt[:, hs], st.astype(BF16))
            st_scr[h] = st * dj[:, hs] + _dot_tn(vb[:, hs], kt[:, hs])
            o_scr[pl.ds(r0, SUB), hs] = oi + od[:, hs]
        return carry

    lax.fori_loop(0, th // SUB, step, 0)

    gate = raw_ref[0, :, 3 * HG_W:4 * HG_W]
    gate = gate * jax.nn.sigmoid(gate)
    gain = gain_ref[...]
    for h in range(HG_HEADS):
        hs = slice(h * HG_DV, (h + 1) * HG_DV)
        o_out[0, :, hs] = (_rms(o_scr[:, hs], gain) * gate[:, hs]).astype(BF16)

    @pl.when(i == pl.num_programs(1) - 1)
    def _():
        for h in range(HG_HEADS):
            s_out[0, h] = st_scr[h].T


def _step_matrices(th):
    r = np.arange(th)
    same = (r[:, None] // SUB) == (r[None, :] // SUB)
    tin = (same & (r[None, :] <= r[:, None])).astype(np.float32)
    tup = (same & (r[None, :] > r[:, None])).astype(np.float32)
    hd = np.arange(HG_W) // HG_DK
    ones = (hd[:, None] == hd[None, :]).astype(np.float32)
    return jnp.asarray(tin, BF16), jnp.asarray(tup, BF16), jnp.asarray(ones, BF16)


def _hgrn(raw, lb_logits, gain, s0, th):
    B, L, _ = raw.shape
    tin, tup, ones = _step_matrices(th)
    f32s = lambda: pltpu.VMEM((th, HG_W), F32)
    b16s = lambda: pltpu.VMEM((th, HG_W), BF16)
    return pl.pallas_call(
        functools.partial(_hgrn_body, th=th),
        out_shape=[jax.ShapeDtypeStruct((B, L, HG_W), BF16),
                   jax.ShapeDtypeStruct((B, HG_HEADS, HG_DK, HG_DV), F32)],
        grid=(B, L // th),
        in_specs=[pl.BlockSpec((1, th, 4 * HG_W), lambda b, i: (b, i, 0)),
                  _const_spec(lb_logits.shape), _const_spec(gain.shape),
                  pl.BlockSpec((1, HG_HEADS, HG_DK, HG_DV), lambda b, i: (b, 0, 0, 0)),
                  _const_spec(tin.shape), _const_spec(tup.shape), _const_spec(ones.shape)],
        out_specs=[pl.BlockSpec((1, th, HG_W), lambda b, i: (b, i, 0)),
                   pl.BlockSpec((1, HG_HEADS, HG_DK, HG_DV), lambda b, i: (b, 0, 0, 0))],
        scratch_shapes=[pltpu.VMEM((HG_HEADS, HG_DV, HG_DK), F32),
                        f32s(), f32s(), f32s(), f32s(), f32s(),
                        b16s(), b16s(), b16s(),
                        pltpu.VMEM((SUB * SUB, HG_W), BF16), f32s()],
        compiler_params=pltpu.CompilerParams(dimension_semantics=("arbitrary", "arbitrary"),
                                             vmem_limit_bytes=VMEM_LIMIT),
        name="hgrn_scan",
    )(raw, lb_logits, gain, s0, tin, tup, ones)


def _unit_time():
    i = np.arange(UNIT)
    return SLABS * (i % SLAB_ROWS) + i // SLAB_ROWS


def _unit_tables():
    t = _unit_time()
    tin = (t[None, :] <= t[:, None]).astype(np.float32)
    rows = []
    for c in (1, 2, 4, 8):
        g = np.arange(SLAB_ROWS)
        t_bnd = SLABS * (2 * c * (g // (2 * c)) + c - 1) + SLABS - 1
        rows.append((t[None, :] <= t_bnd[:, None]).astype(np.float32))
    tin_ext = np.concatenate([tin] + rows, axis=0)
    x = t[:, None] ^ t[None, :]
    lv = np.where(x > 0, np.floor(np.log2(np.maximum(x, 1))), DIAG_LEVEL).astype(np.int32)
    lv = np.where(t[None, :] <= t[:, None], lv, -1)
    return jnp.asarray(tin_ext, BF16), jnp.asarray(np.concatenate([lv, lv], axis=1), jnp.int32)


def _pair_blockdiag(a, b):
    z = jnp.zeros_like(a)
    return jnp.concatenate([jnp.concatenate([a, z], axis=1), jnp.concatenate([z, b], axis=1)], axis=0)


def _hgrn_units_math(field, nu, st, lbl_ref, gain_ref, tin_ref, lv_ref, unperm_ref, o_out, after_level):
    lbl = lbl_ref[...]
    e = jnp.exp(lbl - jnp.max(lbl, axis=0, keepdims=True))
    lb = (e / jnp.sum(e, axis=0, keepdims=True))[0:1]
    tin = tin_ref[...]
    lv = lv_ref[...]
    gain = gain_ref[...]

    units = range(nu)
    pairs = range(HG_HEADS // 2)

    def pair_cols(x, hp):
        c0 = 2 * HG_DK * hp
        return x[:, c0:c0 + HG_DK], x[:, c0 + HG_DK:c0 + 2 * HG_DK]

    gates = [field(u, 3) for u in units]
    q = [field(u, 0) * HG_DK ** -0.5 for u in units]
    f = [lb + (1.0 - lb) * jax.nn.sigmoid(field(u, 1)) for u in units]
    kk = [1.0 - f[u] for u in units]
    lsplit = [_split3(jnp.log(f[u]) * np.float32(1.0 / np.log(2.0))) for u in units]
    bx = [_dot(tin, lsplit[u][0]) + _dot(tin, lsplit[u][1]) + _dot(tin, lsplit[u][2]) for u in units]
    b = [bx[u][0:UNIT] for u in units]
    bs = [[b[u][SLAB_ROWS * p:SLAB_ROWS * (p + 1)] for p in range(SLABS)] for u in units]
    vb = [field(u, 2).astype(BF16) for u in units]
    qb = [q[u].astype(BF16) for u in units]
    kkb = [kk[u].astype(BF16) for u in units]

    def slab_decay(u, level, p):
        if level < 3:
            half = 1 << level
            ref = p - p % (2 * half) + half - 1
            if p == ref:
                return jnp.ones_like(bs[u][p])
            return jnp.exp2(bs[u][p] - bs[u][ref] if p > ref else bs[u][ref] - bs[u][p])
        r = bx[u][UNIT + SLAB_ROWS * (level - 3):UNIT + SLAB_ROWS * (level - 2)]
        if level == N_LEVELS - 1:
            h = SLAB_ROWS // 2
            return jnp.concatenate([jnp.exp2(r[:h] - bs[u][p][:h]), jnp.exp2(bs[u][p][h:] - r[h:])], axis=0)
        return jnp.exp2(-jnp.abs(bs[u][p] - r))

    acc = [[jnp.zeros((UNIT, 2 * HG_DK), F32) for _ in pairs] for _ in units]
    for level in range(N_LEVELS + 1):
        here = lv == level
        for u in units:
            if level == DIAG_LEVEL:
                qt, kt = qb[u], kkb[u]
            else:
                dec = jnp.concatenate([slab_decay(u, level, p) for p in range(SLABS)], axis=0).astype(BF16)
                qt, kt = qb[u] * dec, kkb[u] * dec
            for hp in pairs:
                a2 = _dot_nt(qt[:, 2 * HG_DK * hp:2 * HG_DK * (hp + 1)], _pair_blockdiag(*pair_cols(kt, hp)))
                acc[u][hp] = jnp.where(here, a2, acc[u][hp])
        after_level(level)

    b_end = [b[u][UNIT - 1:UNIT] for u in units]
    qi = [(q[u] * jnp.exp2(b[u])).astype(BF16) for u in units]
    ks = [(kk[u] * jnp.exp2(b_end[u] - b[u])).astype(BF16) for u in units]
    d = [jnp.exp2(b_end[u]) for u in units]
    o_intra = [[_dot(acc[u][hp].astype(BF16), _pair_blockdiag(*pair_cols(vb[u], hp))) for hp in pairs]
               for u in units]
    ds = [[_dot_tn(vb[u][:, h * HG_DV:(h + 1) * HG_DV], ks[u][:, h * HG_DK:(h + 1) * HG_DK])
           for h in range(HG_HEADS)] for u in units]
    o_inter = []
    for u in units:
        o_inter.append([_dot_nt(qi[u][:, 2 * HG_DK * hp:2 * HG_DK * (hp + 1)],
                                _pair_blockdiag(st[2 * hp].astype(BF16), st[2 * hp + 1].astype(BF16)))
                        for hp in pairs])
        st = [st[h] * d[u][:, h * HG_DK:(h + 1) * HG_DK] + ds[u][h] for h in range(HG_HEADS)]
    res = []
    for u in units:
        o = jnp.concatenate([o_intra[u][hp] + o_inter[u][hp] for hp in pairs], axis=1)
        gate = gates[u] * jax.nn.sigmoid(gates[u])
        r = jnp.concatenate([_rms(o[:, h * HG_DV:(h + 1) * HG_DV], gain) for h in range(HG_HEADS)], axis=1)
        res.append((r * gate).astype(BF16))
    for u in units:
        o_out[0, UNIT * u:UNIT * (u + 1), :] = _dot(unperm_ref[...], res[u]).astype(BF16)
    return st


def _attn_body(q_ref, k_ref, vt_ref, o_ref, acc_scr, *, tq, nq):
    for qi in range(nq):
        q = q_ref[0, qi * tq:(qi + 1) * tq, :]
        nblk = qi + 1
        scores = lambda i: _dot_nt(k_ref[0, i * tq:(i + 1) * tq, :], q)
        pending = [scores(i) for i in range(min(ATTN_LOOKAHEAD, nblk))]
        m = jnp.full((1, tq), -jnp.inf, F32)
        l = jnp.zeros((1, tq), F32)
        for ki in range(nblk):
            st = pending.pop(0)
            if ki + ATTN_LOOKAHEAD < nblk:
                pending.append(scores(ki + ATTN_LOOKAHEAD))
            if ki == nblk - 1:
                kc = lax.broadcasted_iota(jnp.int32, st.shape, 0) // CHUNK
                qc = lax.broadcasted_iota(jnp.int32, st.shape, 1) // CHUNK
                st = jnp.where(kc <= qc, st, -jnp.inf)
            m_new = jnp.maximum(m, jnp.max(st, axis=0, keepdims=True))
            p = jnp.exp2(st - m_new)
            alpha = jnp.exp2(m - m_new)
            l = alpha * l + jnp.sum(p, axis=0, keepdims=True)
            pv = _dot(vt_ref[0, :, ki * tq:(ki + 1) * tq], p.astype(BF16))
            if ki == 0:
                acc_scr[qi % 2] = pv
            else:
                acc_scr[qi % 2] = alpha * acc_scr[qi % 2] + pv
            m = m_new
        o_ref[0, qi * tq:(qi + 1) * tq, :] = (acc_scr[qi % 2] / l).T.astype(BF16)


def _attn(q, k, vt, tq):
    B, L, _ = q.shape
    return pl.pallas_call(
        functools.partial(_attn_body, tq=tq, nq=L // tq),
        out_shape=jax.ShapeDtypeStruct((B, L, MLA_HEADS * MLA_V), BF16),
        grid=(B, MLA_HEADS),
        in_specs=[pl.BlockSpec((1, L, QK_W), lambda b, h: (b, 0, h)),
                  pl.BlockSpec((1, L, QK_W), lambda b, h: (b, 0, h)),
                  pl.BlockSpec((1, MLA_V, L), lambda b, h: (b, h, 0))],
        out_specs=pl.BlockSpec((1, L, MLA_V), lambda b, h: (b, 0, h)),
        scratch_shapes=[pltpu.VMEM((2, MLA_V, tq), F32)],
        compiler_params=pltpu.CompilerParams(dimension_semantics=("arbitrary",) * 2,
                                             vmem_limit_bytes=VMEM_LIMIT),
        name="mla_attn_prompt",
    )(q, k, vt)


def _attn_cache_body(q_ref, kn_ref, vn_ref, clat_ref, ckpe_ref, wuk_ref, wuv_ref, place_ref, o_ref):
    n = q_ref.shape[1]
    latc = clat_ref[0].astype(BF16)
    kpec = ckpe_ref[0].astype(BF16)
    qlat, qrope, s_new = [], [], []
    for h in range(MLA_HEADS):
        qh = q_ref[0, :, QK_W * h:QK_W * (h + 1)]
        ws = slice(MLA_NOPE * h, MLA_NOPE * (h + 1))
        qlat.append(_dot_nt(qh[:, 0:MLA_NOPE], wuk_ref[:, ws]).astype(BF16))
        qrope.append(_dot_nt(qh[:, MLA_NOPE:QK_W], place_ref[:, ws]).astype(BF16))
        s_new.append(_dot_nt(qh, kn_ref[0, :, QK_W * h:QK_W * (h + 1)]))
    s_new = jnp.concatenate(s_new, axis=0)
    s_c = (_dot_nt(jnp.concatenate(qlat, axis=0), latc)
           + _dot(jnp.concatenate(qrope, axis=0), kpec))
    m = jnp.maximum(jnp.max(s_c, axis=-1, keepdims=True), jnp.max(s_new, axis=-1, keepdims=True))
    p_c = jnp.exp2(s_c - m)
    p_new = jnp.exp2(s_new - m)
    l = jnp.sum(p_c, axis=-1, keepdims=True) + jnp.sum(p_new, axis=-1, keepdims=True)
    o_lat = _dot(p_c.astype(BF16), latc).astype(BF16)
    p_new = p_new.astype(BF16)
    for h in range(MLA_HEADS):
        ws = slice(MLA_V * h, MLA_V * (h + 1))
        rs = slice(n * h, n * (h + 1))
        acc = _dot(o_lat[rs], wuv_ref[:, ws]) + _dot_nt(p_new[rs], vn_ref[0, ws, :])
        o_ref[0, :, ws] = (acc / l[rs]).astype(BF16)


def _rope_place():
    p = np.zeros((MLA_ROPE, MLA_HEADS * MLA_NOPE), np.float32)
    for h in range(MLA_HEADS):
        for i in range(ROPE_HALF):
            p[i, MLA_NOPE * h + ROPE_HALF * h + i] = 1.0
            p[ROPE_HALF + i, MLA_NOPE * h + ROPE_HALF * ((h + 1) % MLA_HEADS) + i] = 1.0
    return jnp.asarray(p, BF16)


def _attn_cache(q, kn, vn, clat, ckpe, wuk, wuv):
    B, L, _ = q.shape
    P = clat.shape[1]
    place = _rope_place()
    bspec = lambda n, w: pl.BlockSpec((1, n, w), lambda b: (b, 0, 0))
    return pl.pallas_call(
        _attn_cache_body,
        out_shape=jax.ShapeDtypeStruct((B, L, MLA_HEADS * MLA_V), BF16),
        grid=(B,),
        in_specs=[bspec(L, MLA_HEADS * QK_W), bspec(L, MLA_HEADS * QK_W), bspec(MLA_HEADS * MLA_V, L),
                  bspec(P, KV_LORA), bspec(MLA_ROPE, P),
                  _const_spec(wuk.shape), _const_spec(wuv.shape), _const_spec(place.shape)],
        out_specs=bspec(L, MLA_HEADS * MLA_V),
        compiler_params=pltpu.CompilerParams(dimension_semantics=("arbitrary",),
                                             vmem_limit_bytes=VMEM_LIMIT),
        name="mla_attn_cache",
    )(q, kn, vn, clat, ckpe, wuk, wuv, place)


def _ffn_body(x_ref, ohg_ref, omla_ref, mod_ref, conv0_ref, wout_ref, gffn_ref, wup_ref, cw_ref, cb_ref,
              wdn_ref, gfin_ref, y_ref, conv_out, carry_scr, a_scr, u_scr, *, tm, ft):
    @pl.when(pl.program_id(1) == 0)
    def _():
        carry_scr[...] = conv0_ref[0]

    mod = mod_ref[0]
    g1 = mod[2:3]
    sh2 = mod[3:4]
    sc2 = mod[4:5]
    g2 = mod[5:6]
    o = _dot(ohg_ref[0], wout_ref[0:HG_W, :]) + _dot(omla_ref[0], wout_ref[HG_W:2 * HG_W, :])
    x1 = x_ref[0] + g1 * o
    h2 = (_rms(x1, gffn_ref[...]) * (1.0 + sc2) + sh2).astype(BF16)
    for j in range(D_FF // ft):
        cs = slice(j * ft, (j + 1) * ft)
        a = _dot(h2, wup_ref[:, cs])
        v = _dot(h2, wup_ref[:, D_FF + j * ft:D_FF + (j + 1) * ft])
        a_scr[0:CARRY_ROWS, :] = carry_scr[:, cs]
        a_scr[CARRY_ROWS:CARRY_ROWS + tm, :] = a
        carry_scr[:, cs] = a[tm - CARRY_ROWS:tm]
        cw = cw_ref[:, cs]
        conv = (cb_ref[:, cs] + cw[0:1] * a_scr[CARRY_ROWS - 2:CARRY_ROWS - 2 + tm, :]
                + cw[1:2] * a_scr[CARRY_ROWS - 1:CARRY_ROWS - 1 + tm, :] + cw[2:3] * a)
        g = 0.5 * conv * (1.0 + lax.erf(conv * np.float32(np.sqrt(0.5))))
        u_scr[:, cs] = (g * v).astype(BF16)
        conv_out[0, :, cs] = a[tm - (CONV_W - 1):tm]

    x2 = x1 + g2 * _dot(u_scr[...], wdn_ref[...])
    y_ref[0] = _rms(x2, gfin_ref[...])


def _ffn(x, ohg, omla, mod, conv0, wout, gffn, wup, cw, cb, wdn, gfin, tm, ft):
    B, L, D = x.shape
    row = lambda w: pl.BlockSpec((1, tm, w), lambda b, i: (b, i, 0))
    once = lambda a: pl.BlockSpec(a.shape, lambda b, i: (0,) * a.ndim, pipeline_mode=pl.Buffered(1))
    return pl.pallas_call(
        functools.partial(_ffn_body, tm=tm, ft=ft),
        out_shape=[jax.ShapeDtypeStruct((B, L, D), F32),
                   jax.ShapeDtypeStruct((B, CONV_W - 1, D_FF), F32)],
        grid=(B, L // tm),
        in_specs=[row(D), row(HG_W), row(MLA_HEADS * MLA_V),
                  pl.BlockSpec((1, 6, D), lambda b, i: (b, 0, 0)),
                  pl.BlockSpec((1, CARRY_ROWS, D_FF), lambda b, i: (b, 0, 0)),
                  once(wout), once(gffn), once(wup), once(cw), once(cb), once(wdn), once(gfin)],
        out_specs=[row(D), pl.BlockSpec((1, CONV_W - 1, D_FF), lambda b, i: (b, 0, 0))],
        scratch_shapes=[pltpu.VMEM((CARRY_ROWS, D_FF), F32), pltpu.VMEM((CARRY_ROWS + tm, ft), F32),
                        pltpu.VMEM((tm, D_FF), BF16)],
        compiler_params=pltpu.CompilerParams(dimension_semantics=("arbitrary", "arbitrary"),
                                             vmem_limit_bytes=VMEM_LIMIT),
        name="out_ffn",
    )(x, ohg, omla, mod, conv0, wout, gffn, wup, cw, cb, wdn, gfin)


def _prep_weights(w_in, mla_q_norm_gain, mla_kv_norm_gain, w_uq, w_uk, w_uv):
    win = w_in.astype(BF16)
    kp = win[:, 4 * HG_W + Q_LORA + KV_LORA:]
    wkx = jnp.concatenate([jnp.tile(kp[:, :ROPE_HALF], (1, MLA_HEADS)),
                           jnp.tile(kp[:, ROPE_HALF:], (1, MLA_HEADS))], axis=1)
    uq = w_uq.reshape(Q_LORA, MLA_HEADS, MLA_NOPE + MLA_ROPE)
    wuqn = uq[:, :, :MLA_NOPE].reshape(Q_LORA, MLA_HEADS * MLA_NOPE).astype(BF16)
    x1a = uq[:, :, MLA_NOPE:MLA_NOPE + ROPE_HALF].reshape(Q_LORA, MLA_HEADS * ROPE_HALF)
    x2a = uq[:, :, MLA_NOPE + ROPE_HALF:].reshape(Q_LORA, MLA_HEADS * ROPE_HALF)
    wuqr = jnp.concatenate([x1a, x2a, jnp.roll(x1a, ROPE_HALF, axis=1), jnp.roll(x2a, ROPE_HALF, axis=1)],
                           axis=1).astype(BF16)
    wuk = w_uk.reshape(KV_LORA, MLA_HEADS * MLA_NOPE).astype(BF16)
    wuv = w_uv.reshape(KV_LORA, MLA_HEADS * MLA_V).astype(BF16)
    return (win, wkx, kp.T, mla_q_norm_gain.reshape(1, -1), mla_kv_norm_gain.reshape(1, -1), wuqn, wuqr, wuk, wuv)


def _rope_tables(pos):
    inv_freq = ROPE_THETA ** (-jnp.arange(ROPE_HALF, dtype=F32) / ROPE_HALF)
    ang = pos.astype(F32)[:, None] * inv_freq[None, :]
    cos, sin = jnp.cos(ang), jnp.sin(ang)
    return jnp.tile(cos, (1, MLA_HEADS)), jnp.tile(sin, (1, MLA_HEADS)), cos.T, sin.T


def _tile(n, pref):
    return pref if n % pref == 0 else n


def kernel(x_prompt, x_sample, c_prompt, c_sample, cache_kv_latent, cache_k_rope, state_hgrn, state_ffn_conv, w_ada, b_ada, norm_mix_gain, w_in, hg_lb_logits, hg_norm_gain, mla_q_norm_gain, mla_kv_norm_gain, w_uq, w_uk, w_uv, w_out, norm_ffn_gain, w_up, conv_w, conv_b, w_down, final_norm_gain):
    assert w_ada.shape[0] == 1, "single-layer trunk"
    B, L, D = x_prompt.shape
    Bs, Ls, _ = x_sample.shape
    past = cache_kv_latent.shape[2]

    mod = _ada(jnp.concatenate([c_prompt, c_sample], axis=0), w_ada[0], b_ada)
    mod = mod.reshape(B + Bs, 6, D)
    wts = _prep_weights(w_in[0], mla_q_norm_gain[0], mla_kv_norm_gain[0], w_uq[0], w_uk[0], w_uv[0])
    gmix = norm_mix_gain
    wout = w_out[0].astype(BF16)
    wup = w_up[0].astype(BF16)
    wdn = w_down[0].astype(BF16)
    gfin = final_norm_gain.reshape(1, D)
    ffn_w = (wout, norm_ffn_gain, wup, conv_w[0], conv_b, wdn, gfin)

    def layer(x, mod_x, pos, s0, conv0, cache):
        n, l, _ = x.shape
        rope = _rope_tables(pos)
        tm = _tile(l, 512)
        if tm % UNIT == 0:
            ohg, s_new, q, k, v, lat, kpe = _inproj_hgrn(x, mod_x, gmix, wts, rope, hg_lb_logits, hg_norm_gain,
                                                         s0, tm)
        else:
            raw, q, k, v, lat, kpe = _inproj(x, mod_x, gmix, wts, rope, tm)
            ohg, s_new = _hgrn(raw, hg_lb_logits, hg_norm_gain, s0, _tile(l, 256))
        if cache is None:
            omla = _attn(q, k, v, _tile(l, 512))
        else:
            omla = _attn_cache(q, k, v, cache[0], cache[1], wts[7], wts[8])
        y, conv_new = _ffn(x, ohg, omla, mod_x, conv0, *ffn_w, _tile(l, 512), FF_TILE)
        return y, lat[None], jnp.swapaxes(kpe, 1, 2)[None], s_new[None], conv_new[None]

    zeros_state = jnp.zeros((B, HG_HEADS, HG_DK, HG_DV), F32)
    zeros_conv = jnp.zeros((B, CARRY_ROWS, D_FF), F32)
    conv0_s = jnp.pad(state_ffn_conv[0], ((0, 0), (CARRY_ROWS - (CONV_W - 1), 0), (0, 0)))
    yp, latp, kpep, hgp, cvp = layer(x_prompt, mod[:B], jnp.arange(L), zeros_state, zeros_conv, None)
    ys, lats, kpes, hgs, cvs = layer(x_sample, mod[B:], past + jnp.arange(Ls), state_hgrn[0], conv0_s,
                                     (cache_kv_latent[0], jnp.swapaxes(cache_k_rope[0], 1, 2)))
    return (yp, ys, latp, kpep, hgp, cvp, lats, kpes, hgs, cvs)
```

```python
import functools

import numpy as np
import jax
import jax.numpy as jnp
from jax import lax
from jax.experimental import pallas as pl
from jax.experimental.pallas import tpu as pltpu

F32 = jnp.float32
BF16 = jnp.bfloat16

D_MODEL = 1024
CHUNK = 64
HG_HEADS = 4
HG_DK = 128
HG_DV = 128
HG_W = HG_HEADS * HG_DK
MLA_HEADS = 4
MLA_NOPE = 128
MLA_ROPE = 64
ROPE_HALF = MLA_ROPE // 2
MLA_V = 128
Q_LORA = 384
KV_LORA = 256
ROPE_THETA = 10000.0
MLA_SCALE = (MLA_NOPE + MLA_ROPE) ** -0.5
Q_SCALE = MLA_SCALE * float(np.log2(np.e))
QK_W = 2 * MLA_NOPE
D_FF = 2816
CONV_W = 3
EPS = 1e-6

SUB = 16
FF_TILE = D_FF
ATTN_LOOKAHEAD = 4
CARRY_ROWS = 8
VMEM_LIMIT = 56 * 1024 * 1024
UNIT = 128
SLABS = 8
SLAB_ROWS = UNIT // SLABS
N_LEVELS = 7
DIAG_LEVEL = N_LEVELS


def _dot(a, b):
    return jnp.dot(a, b, preferred_element_type=F32)


def _dot_nt(a, b):
    return lax.dot_general(a, b, (((1,), (1,)), ((), ())), preferred_element_type=F32)


def _dot_tn(a, b):
    return lax.dot_general(a, b, (((0,), (0,)), ((), ())), preferred_element_type=F32)


def _split3(x):
    x1 = x.astype(BF16)
    r1 = x - x1.astype(F32)
    x2 = r1.astype(BF16)
    r2 = r1 - x2.astype(F32)
    return x1, x2, r2.astype(BF16)


def _rms(x, g):
    return x * lax.rsqrt(jnp.mean(x * x, axis=-1, keepdims=True) + EPS) * g


def _const_spec(shape):
    n = len(shape)
    return pl.BlockSpec(shape, lambda *_: (0,) * n)


def _ada_body(c_ref, w_ref, b_ref, o_ref):
    c = c_ref[...]
    s = c * jax.nn.sigmoid(c)
    s1, s2, _ = _split3(s)
    w = w_ref[...]
    w1 = w.astype(BF16)
    w2 = (w - w1.astype(F32)).astype(BF16)
    o_ref[...] = _dot(s1, w1) + _dot(s1, w2) + _dot(s2, w1) + b_ref[...]


def _ada(c, w_ada, b_ada):
    n, d = c.shape
    nout = w_ada.shape[1]
    tn = 1024
    return pl.pallas_call(
        _ada_body,
        out_shape=jax.ShapeDtypeStruct((n, nout), F32),
        grid=(nout // tn,),
        in_specs=[pl.BlockSpec((n, d), lambda j: (0, 0)),
                  pl.BlockSpec((d, tn), lambda j: (0, j)),
                  pl.BlockSpec((1, tn), lambda j: (0, j))],
        out_specs=pl.BlockSpec((n, tn), lambda j: (0, j)),
        compiler_params=pltpu.CompilerParams(dimension_semantics=("arbitrary",)),
        name="adaln_mod",
    )(c, w_ada, b_ada)


def _rope_slots(r1, r2, nope, out_ref, slot):
    for h in range(MLA_HEADS):
        xh = jnp.where(slot == h, r1, jnp.where(slot == (h + 1) % MLA_HEADS, r2, 0.0))
        out_ref[0, h, :, 0:MLA_NOPE] = nope[:, MLA_NOPE * h:MLA_NOPE * (h + 1)].astype(BF16)
        out_ref[0, h, :, MLA_NOPE:QK_W] = xh.astype(BF16)


def _mla_stages(hb, refs, outs, rope_refs):
    (wcq_ref, wckv_ref, wkx_ref, wkpt_ref, qng_ref, kvng_ref, wuqn_ref, wuqr_ref, wuk_ref, wuvt_ref) = refs
    q_out, k_out, v_out, lat_out, kpe_out = outs
    cos_ref, sin_ref, cost_ref, sint_ref = rope_refs
    t = {}

    def project():
        t["cq"] = _dot(hb, wcq_ref[...])
        t["ckv"] = _dot(hb, wckv_ref[...])
        t["kx"] = _dot(hb, wkx_ref[...])
        t["kxt"] = _dot_nt(wkpt_ref[...], hb)

    def q_up():
        cq = _rms(t["cq"], qng_ref[...]).astype(BF16)
        t["qn"] = _dot(cq, wuqn_ref[...]) * Q_SCALE
        t["qr"] = _dot(cq, wuqr_ref[...]) * Q_SCALE

    def kv_up():
        lat = _rms(t["ckv"], kvng_ref[...])
        lat_out[0] = lat
        latb = lat.astype(BF16)
        t["kn"] = _dot(latb, wuk_ref[...])
        v_out[0] = _dot_nt(wuvt_ref[...], latb).astype(BF16)

    def slot_of(cos):
        return lax.broadcasted_iota(jnp.int32, cos.shape, 1) // ROPE_HALF

    def q_rope():
        cos, sin, qr = cos_ref[...], sin_ref[...], t["qr"]
        qr1 = qr[:, 0:128] * cos - qr[:, 128:256] * sin
        qr2 = qr[:, 384:512] * cos + qr[:, 256:384] * sin
        _rope_slots(qr1, qr2, t["qn"], q_out, slot_of(cos))

    def k_rope():
        cos, sin, kx = cos_ref[...], sin_ref[...], t["kx"]
        kr1 = kx[:, 0:128] * cos - kx[:, 128:256] * sin
        kr2 = kx[:, 128:256] * cos + kx[:, 0:128] * sin
        _rope_slots(kr1, kr2, t["kn"], k_out, slot_of(cos))
        kxt = t["kxt"]
        x1t, x2t = kxt[0:ROPE_HALF], kxt[ROPE_HALF:MLA_ROPE]
        cost, sint = cost_ref[...], sint_ref[...]
        kpe_out[0] = jnp.concatenate([x1t * cost - x2t * sint, x2t * cost + x1t * sint], axis=0)

    return [project, q_up, kv_up, q_rope, k_rope]


def _normed_input(x_ref, mod_ref, gmix_ref):
    mod = mod_ref[0]
    return (_rms(x_ref[0], gmix_ref[...]) * (1.0 + mod[1:2]) + mod[0:1]).astype(BF16)


def _win_views(win_ref):
    hgw = 4 * HG_W
    return (win_ref.at[:, 0:hgw], win_ref.at[:, hgw:hgw + Q_LORA],
            win_ref.at[:, hgw + Q_LORA:hgw + Q_LORA + KV_LORA])


def _inproj_body(x_ref, mod_ref, gmix_ref, win_ref, wkx_ref, wkpt_ref, qng_ref, kvng_ref,
                 wuqn_ref, wuqr_ref, wuk_ref, wuvt_ref, cos_ref, sin_ref, cost_ref, sint_ref,
                 hg_out, q_out, k_out, v_out, lat_out, kpe_out):
    whg_ref, wcq_ref, wckv_ref = _win_views(win_ref)
    hb = _normed_input(x_ref, mod_ref, gmix_ref)
    hg_out[0] = _dot(hb, whg_ref[...])
    for stage in _mla_stages(hb, (wcq_ref, wckv_ref, wkx_ref, wkpt_ref, qng_ref, kvng_ref, wuqn_ref, wuqr_ref,
                                  wuk_ref, wuvt_ref),
                             (q_out, k_out, v_out, lat_out, kpe_out), (cos_ref, sin_ref, cost_ref, sint_ref)):
        stage()


def _inproj_hgrn_body(x_ref, mod_ref, gmix_ref, win_ref, wkx_ref, wkpt_ref, qng_ref, kvng_ref,
                      wuqn_ref, wuqr_ref, wuk_ref, wuvt_ref, cos_ref, sin_ref, cost_ref, sint_ref,
                      perm_ref, lbl_ref, gain_ref, s0t_ref, tin_ref, lv_ref, unperm_ref,
                      hg_out, st_out, q_out, k_out, v_out, lat_out, kpe_out, raw_scr, st_scr, *, nt):
    j = pl.program_id(0)

    @pl.when(j == 0)
    def _():
        raw_scr[...] = jnp.zeros(raw_scr.shape, F32)

    first_of_batch = lax.rem(jnp.maximum(j - 1, 0), nt) == 0
    st = [jnp.where(first_of_batch, s0t_ref[0, hd], st_scr[hd]) for hd in range(HG_HEADS)]

    whg_ref, wcq_ref, wckv_ref = _win_views(win_ref)
    n_units = x_ref.shape[1] // UNIT
    t = {}

    def start():
        hb = _normed_input(x_ref, mod_ref, gmix_ref)
        t["hbp"] = jnp.concatenate([_dot(perm_ref[...], hb[UNIT * u:UNIT * (u + 1)]).astype(BF16)
                                    for u in range(n_units)], axis=0)
        t["mla"] = _mla_stages(hb, (wcq_ref, wckv_ref, wkx_ref, wkpt_ref, qng_ref, kvng_ref, wuqn_ref,
                                    wuqr_ref, wuk_ref, wuvt_ref),
                               (q_out, k_out, v_out, lat_out, kpe_out), (cos_ref, sin_ref, cost_ref, sint_ref))

    def raw_group(k):
        def run():
            raw_scr[:, HG_W * k:HG_W * (k + 1)] = _dot(t["hbp"], whg_ref[:, HG_W * k:HG_W * (k + 1)])
        return run

    chunks = [start] + [raw_group(k) for k in range(4)] + [lambda n=n: t["mla"][n]() for n in range(5)]

    def next_chunk(_level):
        if chunks:
            chunks.pop(0)()

    st = _hgrn_units_math(lambda u, k: raw_scr[UNIT * u:UNIT * (u + 1), HG_W * k:HG_W * (k + 1)], n_units, st,
                          lbl_ref, gain_ref, tin_ref, lv_ref, unperm_ref, hg_out, next_chunk)
    while chunks:
        chunks.pop(0)()
    for hd in range(HG_HEADS):
        st_scr[hd] = st[hd]
        st_out[0, hd] = st[hd]


def _unit_perm(n):
    i = np.arange(n)
    src = UNIT * (i // UNIT) + _unit_time()[i % UNIT]
    return jnp.asarray(src[:, None] == i[None, :], BF16)


def _inproj_args(x, mod, gmix, wts, rope):
    (win, wkx, wkpt, qng, kvng, wuqn, wuqr, wuk, wuv) = wts
    return [x, mod, gmix, win, wkx, wkpt, qng, kvng, wuqn, wuqr, wuk, wuv.T, *rope]


def _inproj_out_shapes(B, L):
    return [((B, MLA_HEADS, L, QK_W), BF16), ((B, MLA_HEADS, L, QK_W), BF16), ((B, MLA_HEADS * MLA_V, L), BF16),
            ((B, L, KV_LORA), F32), ((B, MLA_ROPE, L), F32)]


def _inproj(x, mod, gmix, wts, rope, tm):
    B, L, D = x.shape
    args = _inproj_args(x, mod, gmix, wts, rope)
    row = lambda w: pl.BlockSpec((1, tm, w), lambda b, i: (b, i, 0))
    col = lambda h: pl.BlockSpec((1, h, tm), lambda b, i: (b, 0, i))
    in_specs = [row(D), pl.BlockSpec((1, 6, D), lambda b, i: (b, 0, 0))]
    in_specs += [_const_spec(w.shape) for w in args[2:12]]
    in_specs += [pl.BlockSpec((tm, 128), lambda b, i: (i, 0))] * 2
    in_specs += [pl.BlockSpec((ROPE_HALF, tm), lambda b, i: (0, i))] * 2
    shapes = [((B, L, 4 * HG_W), F32)] + _inproj_out_shapes(B, L)
    heads = pl.BlockSpec((1, MLA_HEADS, tm, QK_W), lambda b, i: (b, 0, i, 0))
    out_specs = [row(4 * HG_W), heads, heads, col(MLA_HEADS * MLA_V),
                 row(KV_LORA), col(MLA_ROPE)]
    return pl.pallas_call(
        _inproj_body,
        out_shape=[jax.ShapeDtypeStruct(s, dt) for s, dt in shapes],
        grid=(B, L // tm),
        in_specs=in_specs,
        out_specs=out_specs,
        compiler_params=pltpu.CompilerParams(dimension_semantics=("arbitrary", "arbitrary"),
                                             vmem_limit_bytes=VMEM_LIMIT),
        name="in_proj",
    )(*args)


def _inproj_hgrn(x, mod, gmix, wts, rope, lb_logits, gain, s0, tm):
    B, L, D = x.shape
    nt = L // tm
    n = B * nt
    args = _inproj_args(x, mod, gmix, wts, rope)
    tin, lv = _unit_tables()
    perm = _unit_perm(UNIT)
    args += [perm, lb_logits, gain, jnp.swapaxes(s0, 2, 3), tin, lv, perm.T]
    cur = lambda j: jnp.minimum(j, n - 1)
    lag = lambda j: jnp.maximum(j - 1, 0)
    row = lambda w, t: pl.BlockSpec((1, tm, w), lambda j: (t(j) // nt, t(j) % nt, 0))
    col = lambda h: pl.BlockSpec((1, h, tm), lambda j: (cur(j) // nt, 0, cur(j) % nt))
    state = pl.BlockSpec((1, HG_HEADS, HG_DV, HG_DK), lambda j: (lag(j) // nt, 0, 0, 0))
    in_specs = [row(D, cur), pl.BlockSpec((1, 6, D), lambda j: (cur(j) // nt, 0, 0))]
    in_specs += [_const_spec(w.shape) for w in args[2:12]]
    in_specs += [pl.BlockSpec((tm, 128), lambda j: (cur(j) % nt, 0))] * 2
    in_specs += [pl.BlockSpec((ROPE_HALF, tm), lambda j: (0, cur(j) % nt))] * 2
    in_specs += [_const_spec(perm.shape), _const_spec(lb_logits.shape), _const_spec(gain.shape), state,
                 _const_spec(tin.shape), _const_spec(lv.shape), _const_spec(perm.shape)]
    shapes = [((B, L, HG_W), BF16), ((B, HG_HEADS, HG_DV, HG_DK), F32)] + _inproj_out_shapes(B, L)
    heads = pl.BlockSpec((1, MLA_HEADS, tm, QK_W), lambda j: (cur(j) // nt, 0, cur(j) % nt, 0))
    out_specs = [row(HG_W, lag), state, heads, heads,
                 col(MLA_HEADS * MLA_V), row(KV_LORA, cur), col(MLA_ROPE)]
    ohg, st, *mla = pl.pallas_call(
        functools.partial(_inproj_hgrn_body, nt=nt),
        out_shape=[jax.ShapeDtypeStruct(s, dt) for s, dt in shapes],
        grid=(n + 1,),
        in_specs=in_specs,
        out_specs=out_specs,
        scratch_shapes=[pltpu.VMEM((tm, 4 * HG_W), F32), pltpu.VMEM((HG_HEADS, HG_DV, HG_DK), F32)],
        compiler_params=pltpu.CompilerParams(dimension_semantics=("arbitrary",),
                                             vmem_limit_bytes=VMEM_LIMIT),
        name="in_proj_hgrn",
    )(*args)
    return (ohg, jnp.swapaxes(st, 2, 3), *mla)


def _hgrn_body(raw_ref, lbl_ref, gain_ref, s0_ref, tin_ref, tup_ref, ones_ref,
               o_out, s_out,
               st_scr, b_scr, q_scr, kk_scr, v_scr, eb_scr, qt_scr, kt_scr, vb_scr, p_scr, o_scr, *, th):
    i = pl.program_id(1)

    @pl.when(i == 0)
    def _():
        for h in range(HG_HEADS):
            st_scr[h] = s0_ref[0, h].T

    lbl = lbl_ref[...]
    e = jnp.exp(lbl - jnp.max(lbl, axis=0, keepdims=True))
    lb = (e / jnp.sum(e, axis=0, keepdims=True))[0:1]

    q = raw_ref[0, :, 0:HG_W] * HG_DK ** -0.5
    f = lb + (1.0 - lb) * jax.nn.sigmoid(raw_ref[0, :, HG_W:2 * HG_W])
    kk = 1.0 - f
    v = raw_ref[0, :, 2 * HG_W:3 * HG_W]
    l1, l2, l3 = _split3(jnp.log(f))
    tin = tin_ref[...]
    tup = tup_ref[...]
    b = _dot(tin, l1) + _dot(tin, l2) + _dot(tin, l3)
    c = _dot(tup, l1) + _dot(tup, l2) + _dot(tup, l3)
    eb = jnp.exp(b)
    b_scr[...] = b
    q_scr[...] = q
    kk_scr[...] = kk
    v_scr[...] = v
    eb_scr[...] = eb
    qt_scr[...] = (q * eb).astype(BF16)
    kt_scr[...] = (kk * jnp.exp(c)).astype(BF16)
    vb_scr[...] = v.astype(BF16)

    rowid = lax.broadcasted_iota(jnp.int32, (SUB, HG_W), 0)

    def step(j, carry):
        r0 = pl.multiple_of(j * SUB, SUB)
        bj = b_scr[pl.ds(r0, SUB), :]
        qj = q_scr[pl.ds(r0, SUB), :]
        for s in range(SUB):
            bs = b_scr[pl.ds(r0 + s, 1), :]
            ks = kk_scr[pl.ds(r0 + s, 1), :]
            dec = jnp.exp(jnp.where(rowid >= s, bj - bs, -jnp.inf))
            p_scr[s * SUB:(s + 1) * SUB, :] = (qj * dec * ks).astype(BF16)
        r = _dot(p_scr[...], ones_ref[...])
        od = jnp.zeros((SUB, HG_W), F32)
        for s in range(SUB):
            od = od + r[s * SUB:(s + 1) * SUB, :] * v_scr[pl.ds(r0 + s, 1), :]
        qt = qt_scr[pl.ds(r0, SUB), :]
        kt = kt_scr[pl.ds(r0, SUB), :]
        vb = vb_scr[pl.ds(r0, SUB), :]
        dj = eb_scr[pl.ds(r0 + SUB - 1, 1), :]
        for h in range(HG_HEADS):
            hs = slice(h * HG_DK, (h + 1) * HG_DK)
            st = st_scr[h]
            oi = _dot_nt(qt[:, hs], st.astype(BF16))
            st_scr[h] = st * dj[:, hs] + _dot_tn(vb[:, hs], kt[:, hs])
            o_scr[pl.ds(r0, SUB), hs] = oi + od[:, hs]
        return carry

    lax.fori_loop(0, th // SUB, step, 0)

    gate = raw_ref[0, :, 3 * HG_W:4 * HG_W]
    gate = gate * jax.nn.sigmoid(gate)
    gain = gain_ref[...]
    for h in range(HG_HEADS):
        hs = slice(h * HG_DV, (h + 1) * HG_DV)
        o_out[0, :, hs] = (_rms(o_scr[:, hs], gain) * gate[:, hs]).astype(BF16)

    @pl.when(i == pl.num_programs(1) - 1)
    def _():
        for h in range(HG_HEADS):
            s_out[0, h] = st_scr[h].T


def _step_matrices(th):
    r = np.arange(th)
    same = (r[:, None] // SUB) == (r[None, :] // SUB)
    tin = (same & (r[None, :] <= r[:, None])).astype(np.float32)
    tup = (same & (r[None, :] > r[:, None])).astype(np.float32)
    hd = np.arange(HG_W) // HG_DK
    ones = (hd[:, None] == hd[None, :]).astype(np.float32)
    return jnp.asarray(tin, BF16), jnp.asarray(tup, BF16), jnp.asarray(ones, BF16)


def _hgrn(raw, lb_logits, gain, s0, th):
    B, L, _ = raw.shape
    tin, tup, ones = _step_matrices(th)
    f32s = lambda: pltpu.VMEM((th, HG_W), F32)
    b16s = lambda: pltpu.VMEM((th, HG_W), BF16)
    return pl.pallas_call(
        functools.partial(_hgrn_body, th=th),
        out_shape=[jax.ShapeDtypeStruct((B, L, HG_W), BF16),
                   jax.ShapeDtypeStruct((B, HG_HEADS, HG_DK, HG_DV), F32)],
        grid=(B, L // th),
        in_specs=[pl.BlockSpec((1, th, 4 * HG_W), lambda b, i: (b, i, 0)),
                  _const_spec(lb_logits.shape), _const_spec(gain.shape),
                  pl.BlockSpec((1, HG_HEADS, HG_DK, HG_DV), lambda b, i: (b, 0, 0, 0)),
                  _const_spec(tin.shape), _const_spec(tup.shape), _const_spec(ones.shape)],
        out_specs=[pl.BlockSpec((1, th, HG_W), lambda b, i: (b, i, 0)),
                   pl.BlockSpec((1, HG_HEADS, HG_DK, HG_DV), lambda b, i: (b, 0, 0, 0))],
        scratch_shapes=[pltpu.VMEM((HG_HEADS, HG_DV, HG_DK), F32),
                        f32s(), f32s(), f32s(), f32s(), f32s(),
                        b16s(), b16s(), b16s(),
                        pltpu.VMEM((SUB * SUB, HG_W), BF16), f32s()],
        compiler_params=pltpu.CompilerParams(dimension_semantics=("arbitrary", "arbitrary"),
                                             vmem_limit_bytes=VMEM_LIMIT),
        name="hgrn_scan",
    )(raw, lb_logits, gain, s0, tin, tup, ones)


def _unit_time():
    i = np.arange(UNIT)
    return SLABS * (i % SLAB_ROWS) + i // SLAB_ROWS


def _unit_tables():
    t = _unit_time()
    tin = (t[None, :] <= t[:, None]).astype(np.float32)
    rows = []
    for c in (1, 2, 4, 8):
        g = np.arange(SLAB_ROWS)
        t_bnd = SLABS * (2 * c * (g // (2 * c)) + c - 1) + SLABS - 1
        rows.append((t[None, :] <= t_bnd[:, None]).astype(np.float32))
    tin_ext = np.concatenate([tin] + rows, axis=0)
    x = t[:, None] ^ t[None, :]
    lv = np.where(x > 0, np.floor(np.log2(np.maximum(x, 1))), DIAG_LEVEL).astype(np.int32)
    lv = np.where(t[None, :] <= t[:, None], lv, -1)
    return jnp.asarray(tin_ext, BF16), jnp.asarray(np.concatenate([lv, lv], axis=1), jnp.int32)


def _pair_blockdiag(a, b):
    z = jnp.zeros_like(a)
    return jnp.concatenate([jnp.concatenate([a, z], axis=1), jnp.concatenate([z, b], axis=1)], axis=0)


def _hgrn_units_math(field, nu, st, lbl_ref, gain_ref, tin_ref, lv_ref, unperm_ref, o_out, after_level):
    lbl = lbl_ref[...]
    e = jnp.exp(lbl - jnp.max(lbl, axis=0, keepdims=True))
    lb = (e / jnp.sum(e, axis=0, keepdims=True))[0:1]
    tin = tin_ref[...]
    lv = lv_ref[...]
    gain = gain_ref[...]

    units = range(nu)
    pairs = range(HG_HEADS // 2)

    def pair_cols(x, hp):
        c0 = 2 * HG_DK * hp
        return x[:, c0:c0 + HG_DK], x[:, c0 + HG_DK:c0 + 2 * HG_DK]

    gates = [field(u, 3) for u in units]
    q = [field(u, 0) * HG_DK ** -0.5 for u in units]
    f = [lb + (1.0 - lb) * jax.nn.sigmoid(field(u, 1)) for u in units]
    kk = [1.0 - f[u] for u in units]
    lsplit = [_split3(jnp.log(f[u]) * np.float32(1.0 / np.log(2.0))) for u in units]
    bx = [_dot(tin, lsplit[u][0]) + _dot(tin, lsplit[u][1]) + _dot(tin, lsplit[u][2]) for u in units]
    b = [bx[u][0:UNIT] for u in units]
    bs = [[b[u][SLAB_ROWS * p:SLAB_ROWS * (p + 1)] for p in range(SLABS)] for u in units]
    vb = [field(u, 2).astype(BF16) for u in units]
    qb = [q[u].astype(BF16) for u in units]
    kkb = [kk[u].astype(BF16) for u in units]

    def slab_decay(u, level, p):
        if level < 3:
            half = 1 << level
            ref = p - p % (2 * half) + half - 1
            if p == ref:
                return jnp.ones_like(bs[u][p])
            return jnp.exp2(bs[u][p] - bs[u][ref] if p > ref else bs[u][ref] - bs[u][p])
        r = bx[u][UNIT + SLAB_ROWS * (level - 3):UNIT + SLAB_ROWS * (level - 2)]
        if level == N_LEVELS - 1:
            h = SLAB_ROWS // 2
            return jnp.concatenate([jnp.exp2(r[:h] - bs[u][p][:h]), jnp.exp2(bs[u][p][h:] - r[h:])], axis=0)
        return jnp.exp2(-jnp.abs(bs[u][p] - r))

    acc = [[jnp.zeros((UNIT, 2 * HG_DK), F32) for _ in pairs] for _ in units]
    for level in range(N_LEVELS + 1):
        here = lv == level
        for u in units:
            if level == DIAG_LEVEL:
                qt, kt = qb[u], kkb[u]
            else:
                dec = jnp.concatenate([slab_decay(u, level, p) for p in range(SLABS)], axis=0).astype(BF16)
                qt, kt = qb[u] * dec, kkb[u] * dec
            for hp in pairs:
                a2 = _dot_nt(qt[:, 2 * HG_DK * hp:2 * HG_DK * (hp + 1)], _pair_blockdiag(*pair_cols(kt, hp)))
                acc[u][hp] = jnp.where(here, a2, acc[u][hp])
        after_level(level)

    b_end = [b[u][UNIT - 1:UNIT] for u in units]
    qi = [(q[u] * jnp.exp2(b[u])).astype(BF16) for u in units]
    ks = [(kk[u] * jnp.exp2(b_end[u] - b[u])).astype(BF16) for u in units]
    d = [jnp.exp2(b_end[u]) for u in units]
    o_intra = [[_dot(acc[u][hp].astype(BF16), _pair_blockdiag(*pair_cols(vb[u], hp))) for hp in pairs]
               for u in units]
    ds = [[_dot_tn(vb[u][:, h * HG_DV:(h + 1) * HG_DV], ks[u][:, h * HG_DK:(h + 1) * HG_DK])
           for h in range(HG_HEADS)] for u in units]
    o_inter = []
    for u in units:
        o_inter.append([_dot_nt(qi[u][:, 2 * HG_DK * hp:2 * HG_DK * (hp + 1)],
                                _pair_blockdiag(st[2 * hp].astype(BF16), st[2 * hp + 1].astype(BF16)))
                        for hp in pairs])
        st = [st[h] * d[u][:, h * HG_DK:(h + 1) * HG_DK] + ds[u][h] for h in range(HG_HEADS)]
    res = []
    for u in units:
        o = jnp.concatenate([o_intra[u][hp] + o_inter[u][hp] for hp in pairs], axis=1)
        gate = gates[u] * jax.nn.sigmoid(gates[u])
        r = jnp.concatenate([_rms(o[:, h * HG_DV:(h + 1) * HG_DV], gain) for h in range(HG_HEADS)], axis=1)
        res.append((r * gate).astype(BF16))
    for u in units:
        o_out[0, UNIT * u:UNIT * (u + 1), :] = _dot(unperm_ref[...], res[u]).astype(BF16)
    return st


def _attn_body(q_ref, k_ref, vt_ref, o_ref, acc_scr, *, tq, nq):
    for qi in range(nq):
        q = q_ref[0, 0, qi * tq:(qi + 1) * tq, :]
        nblk = qi + 1
        scores = lambda i: _dot_nt(k_ref[0, 0, i * tq:(i + 1) * tq, :], q)
        pending = [scores(i) for i in range(min(ATTN_LOOKAHEAD, nblk))]
        m = jnp.full((1, tq), -jnp.inf, F32)
        l = jnp.zeros((1, tq), F32)
        for ki in range(nblk):
            st = pending.pop(0)
            if ki + ATTN_LOOKAHEAD < nblk:
                pending.append(scores(ki + ATTN_LOOKAHEAD))
            if ki == nblk - 1:
                kc = lax.broadcasted_iota(jnp.int32, st.shape, 0) // CHUNK
                qc = lax.broadcasted_iota(jnp.int32, st.shape, 1) // CHUNK
                st = jnp.where(kc <= qc, st, -jnp.inf)
            m_new = jnp.maximum(m, jnp.max(st, axis=0, keepdims=True))
            p = jnp.exp2(st - m_new)
            alpha = jnp.exp2(m - m_new)
            l = alpha * l + jnp.sum(p, axis=0, keepdims=True)
            pv = _dot(vt_ref[0, :, ki * tq:(ki + 1) * tq], p.astype(BF16))
            if ki == 0:
                acc_scr[qi % 2] = pv
            else:
                acc_scr[qi % 2] = alpha * acc_scr[qi % 2] + pv
            m = m_new
        o_ref[0, 0, qi * tq:(qi + 1) * tq, :] = (acc_scr[qi % 2] / l).T.astype(BF16)


def _attn(q, k, vt, tq):
    B, _, L, _ = q.shape
    per_head = lambda w: pl.BlockSpec((1, 1, L, w), lambda b, h: (b, h, 0, 0))
    return pl.pallas_call(
        functools.partial(_attn_body, tq=tq, nq=L // tq),
        out_shape=jax.ShapeDtypeStruct((B, MLA_HEADS, L, MLA_V), BF16),
        grid=(B, MLA_HEADS),
        in_specs=[per_head(QK_W), per_head(QK_W), pl.BlockSpec((1, MLA_V, L), lambda b, h: (b, h, 0))],
        out_specs=per_head(MLA_V),
        scratch_shapes=[pltpu.VMEM((2, MLA_V, tq), F32)],
        compiler_params=pltpu.CompilerParams(dimension_semantics=("arbitrary",) * 2,
                                             vmem_limit_bytes=VMEM_LIMIT),
        name="mla_attn_prompt",
    )(q, k, vt)


def _attn_cache_body(q_ref, kn_ref, vn_ref, clat_ref, ckpe_ref, wuk_ref, wuv_ref, place_ref, o_ref):
    n = q_ref.shape[2]
    latc = clat_ref[0].astype(BF16)
    kpec = ckpe_ref[0].astype(BF16)
    qlat, qrope, s_new = [], [], []
    for h in range(MLA_HEADS):
        qh = q_ref[0, h]
        ws = slice(MLA_NOPE * h, MLA_NOPE * (h + 1))
        qlat.append(_dot_nt(qh[:, 0:MLA_NOPE], wuk_ref[:, ws]).astype(BF16))
        qrope.append(_dot_nt(qh[:, MLA_NOPE:QK_W], place_ref[:, ws]).astype(BF16))
        s_new.append(_dot_nt(qh, kn_ref[0, h]))
    s_new = jnp.concatenate(s_new, axis=0)
    s_c = (_dot_nt(jnp.concatenate(qlat, axis=0), latc)
           + _dot(jnp.concatenate(qrope, axis=0), kpec))
    m = jnp.maximum(jnp.max(s_c, axis=-1, keepdims=True), jnp.max(s_new, axis=-1, keepdims=True))
    p_c = jnp.exp2(s_c - m)
    p_new = jnp.exp2(s_new - m)
    l = jnp.sum(p_c, axis=-1, keepdims=True) + jnp.sum(p_new, axis=-1, keepdims=True)
    o_lat = _dot(p_c.astype(BF16), latc).astype(BF16)
    p_new = p_new.astype(BF16)
    for h in range(MLA_HEADS):
        ws = slice(MLA_V * h, MLA_V * (h + 1))
        rs = slice(n * h, n * (h + 1))
        acc = _dot(o_lat[rs], wuv_ref[:, ws]) + _dot_nt(p_new[rs], vn_ref[0, ws, :])
        o_ref[0, h] = (acc / l[rs]).astype(BF16)


def _rope_place():
    p = np.zeros((MLA_ROPE, MLA_HEADS * MLA_NOPE), np.float32)
    for h in range(MLA_HEADS):
        for i in range(ROPE_HALF):
            p[i, MLA_NOPE * h + ROPE_HALF * h + i] = 1.0
            p[ROPE_HALF + i, MLA_NOPE * h + ROPE_HALF * ((h + 1) % MLA_HEADS) + i] = 1.0
    return jnp.asarray(p, BF16)


def _attn_cache(q, kn, vn, clat, ckpe, wuk, wuv):
    B, _, L, _ = q.shape
    P = clat.shape[1]
    place = _rope_place()
    bspec = lambda n, w: pl.BlockSpec((1, n, w), lambda b: (b, 0, 0))
    heads = lambda w: pl.BlockSpec((1, MLA_HEADS, L, w), lambda b: (b, 0, 0, 0))
    return pl.pallas_call(
        _attn_cache_body,
        out_shape=jax.ShapeDtypeStruct((B, MLA_HEADS, L, MLA_V), BF16),
        grid=(B,),
        in_specs=[heads(QK_W), heads(QK_W), bspec(MLA_HEADS * MLA_V, L),
                  bspec(P, KV_LORA), bspec(MLA_ROPE, P),
                  _const_spec(wuk.shape), _const_spec(wuv.shape), _const_spec(place.shape)],
        out_specs=heads(MLA_V),
        compiler_params=pltpu.CompilerParams(dimension_semantics=("arbitrary",),
                                             vmem_limit_bytes=VMEM_LIMIT),
        name="mla_attn_cache",
    )(q, kn, vn, clat, ckpe, wuk, wuv, place)


def _ffn_body(x_ref, ohg_ref, omla_ref, mod_ref, conv0_ref, wout_ref, gffn_ref, wup_ref, cw_ref, cb_ref,
              wdn_ref, gfin_ref, y_ref, conv_out, carry_scr, a_scr, u_scr, *, tm, ft):
    @pl.when(pl.program_id(1) == 0)
    def _():
        carry_scr[...] = conv0_ref[0]

    mod = mod_ref[0]
    g1 = mod[2:3]
    sh2 = mod[3:4]
    sc2 = mod[4:5]
    g2 = mod[5:6]
    omla = jnp.concatenate([omla_ref[0, h] for h in range(MLA_HEADS)], axis=1)
    o = _dot(ohg_ref[0], wout_ref[0:HG_W, :]) + _dot(omla, wout_ref[HG_W:2 * HG_W, :])
    x1 = x_ref[0] + g1 * o
    h2 = (_rms(x1, gffn_ref[...]) * (1.0 + sc2) + sh2).astype(BF16)
    for j in range(D_FF // ft):
        cs = slice(j * ft, (j + 1) * ft)
        a = _dot(h2, wup_ref[:, cs])
        v = _dot(h2, wup_ref[:, D_FF + j * ft:D_FF + (j + 1) * ft])
        a_scr[0:CARRY_ROWS, :] = carry_scr[:, cs]
        a_scr[CARRY_ROWS:CARRY_ROWS + tm, :] = a
        carry_scr[:, cs] = a[tm - CARRY_ROWS:tm]
        cw = cw_ref[:, cs]
        conv = (cb_ref[:, cs] + cw[0:1] * a_scr[CARRY_ROWS - 2:CARRY_ROWS - 2 + tm, :]
                + cw[1:2] * a_scr[CARRY_ROWS - 1:CARRY_ROWS - 1 + tm, :] + cw[2:3] * a)
        g = 0.5 * conv * (1.0 + lax.erf(conv * np.float32(np.sqrt(0.5))))
        u_scr[:, cs] = (g * v).astype(BF16)
        conv_out[0, :, cs] = a[tm - (CONV_W - 1):tm]

    x2 = x1 + g2 * _dot(u_scr[...], wdn_ref[...])
    y_ref[0] = _rms(x2, gfin_ref[...])


def _ffn(x, ohg, omla, mod, conv0, wout, gffn, wup, cw, cb, wdn, gfin, tm, ft):
    B, L, D = x.shape
    row = lambda w: pl.BlockSpec((1, tm, w), lambda b, i: (b, i, 0))
    once = lambda a: pl.BlockSpec(a.shape, lambda b, i: (0,) * a.ndim, pipeline_mode=pl.Buffered(1))
    return pl.pallas_call(
        functools.partial(_ffn_body, tm=tm, ft=ft),
        out_shape=[jax.ShapeDtypeStruct((B, L, D), F32),
                   jax.ShapeDtypeStruct((B, CONV_W - 1, D_FF), F32)],
        grid=(B, L // tm),
        in_specs=[row(D), row(HG_W), pl.BlockSpec((1, MLA_HEADS, tm, MLA_V), lambda b, i: (b, 0, i, 0)),
                  pl.BlockSpec((1, 6, D), lambda b, i: (b, 0, 0)),
                  pl.BlockSpec((1, CARRY_ROWS, D_FF), lambda b, i: (b, 0, 0)),
                  once(wout), once(gffn), once(wup), once(cw), once(cb), once(wdn), once(gfin)],
        out_specs=[row(D), pl.BlockSpec((1, CONV_W - 1, D_FF), lambda b, i: (b, 0, 0))],
        scratch_shapes=[pltpu.VMEM((CARRY_ROWS, D_FF), F32), pltpu.VMEM((CARRY_ROWS + tm, ft), F32),
                        pltpu.VMEM((tm, D_FF), BF16)],
        compiler_params=pltpu.CompilerParams(dimension_semantics=("arbitrary", "arbitrary"),
                                             vmem_limit_bytes=VMEM_LIMIT),
        name="out_ffn",
    )(x, ohg, omla, mod, conv0, wout, gffn, wup, cw, cb, wdn, gfin)


def _prep_weights(w_in, mla_q_norm_gain, mla_kv_norm_gain, w_uq, w_uk, w_uv):
    win = w_in.astype(BF16)
    kp = win[:, 4 * HG_W + Q_LORA + KV_LORA:]
    wkx = jnp.concatenate([jnp.tile(kp[:, :ROPE_HALF], (1, MLA_HEADS)),
                           jnp.tile(kp[:, ROPE_HALF:], (1, MLA_HEADS))], axis=1)
    uq = w_uq.reshape(Q_LORA, MLA_HEADS, MLA_NOPE + MLA_ROPE)
    wuqn = uq[:, :, :MLA_NOPE].reshape(Q_LORA, MLA_HEADS * MLA_NOPE).astype(BF16)
    x1a = uq[:, :, MLA_NOPE:MLA_NOPE + ROPE_HALF].reshape(Q_LORA, MLA_HEADS * ROPE_HALF)
    x2a = uq[:, :, MLA_NOPE + ROPE_HALF:].reshape(Q_LORA, MLA_HEADS * ROPE_HALF)
    wuqr = jnp.concatenate([x1a, x2a, jnp.roll(x1a, ROPE_HALF, axis=1), jnp.roll(x2a, ROPE_HALF, axis=1)],
                           axis=1).astype(BF16)
    wuk = w_uk.reshape(KV_LORA, MLA_HEADS * MLA_NOPE).astype(BF16)
    wuv = w_uv.reshape(KV_LORA, MLA_HEADS * MLA_V).astype(BF16)
    return (win, wkx, kp.T, mla_q_norm_gain.reshape(1, -1), mla_kv_norm_gain.reshape(1, -1), wuqn, wuqr, wuk, wuv)


def _rope_tables(pos):
    inv_freq = ROPE_THETA ** (-jnp.arange(ROPE_HALF, dtype=F32) / ROPE_HALF)
    ang = pos.astype(F32)[:, None] * inv_freq[None, :]
    cos, sin = jnp.cos(ang), jnp.sin(ang)
    return jnp.tile(cos, (1, MLA_HEADS)), jnp.tile(sin, (1, MLA_HEADS)), cos.T, sin.T


def _tile(n, pref):
    return pref if n % pref == 0 else n


def kernel(x_prompt, x_sample, c_prompt, c_sample, cache_kv_latent, cache_k_rope, state_hgrn, state_ffn_conv, w_ada, b_ada, norm_mix_gain, w_in, hg_lb_logits, hg_norm_gain, mla_q_norm_gain, mla_kv_norm_gain, w_uq, w_uk, w_uv, w_out, norm_ffn_gain, w_up, conv_w, conv_b, w_down, final_norm_gain):
    assert w_ada.shape[0] == 1, "single-layer trunk"
    B, L, D = x_prompt.shape
    Bs, Ls, _ = x_sample.shape
    past = cache_kv_latent.shape[2]

    mod = _ada(jnp.concatenate([c_prompt, c_sample], axis=0), w_ada[0], b_ada)
    mod = mod.reshape(B + Bs, 6, D)
    wts = _prep_weights(w_in[0], mla_q_norm_gain[0], mla_kv_norm_gain[0], w_uq[0], w_uk[0], w_uv[0])
    gmix = norm_mix_gain
    wout = w_out[0].astype(BF16)
    wup = w_up[0].astype(BF16)
    wdn = w_down[0].astype(BF16)
    gfin = final_norm_gain.reshape(1, D)
    ffn_w = (wout, norm_ffn_gain, wup, conv_w[0], conv_b, wdn, gfin)

    def layer(x, mod_x, pos, s0, conv0, cache):
        n, l, _ = x.shape
        rope = _rope_tables(pos)
        tm = _tile(l, 512)
        if tm % UNIT == 0:
            ohg, s_new, q, k, v, lat, kpe = _inproj_hgrn(x, mod_x, gmix, wts, rope, hg_lb_logits, hg_norm_gain,
                                                         s0, tm)
        else:
            raw, q, k, v, lat, kpe = _inproj(x, mod_x, gmix, wts, rope, tm)
            ohg, s_new = _hgrn(raw, hg_lb_logits, hg_norm_gain, s0, _tile(l, 256))
        if cache is None:
            omla = _attn(q, k, v, _tile(l, 512))
        else:
            omla = _attn_cache(q, k, v, cache[0], cache[1], wts[7], wts[8])
        y, conv_new = _ffn(x, ohg, omla, mod_x, conv0, *ffn_w, _tile(l, 512), FF_TILE)
        return y, lat[None], jnp.swapaxes(kpe, 1, 2)[None], s_new[None], conv_new[None]

    zeros_state = jnp.zeros((B, HG_HEADS, HG_DK, HG_DV), F32)
    zeros_conv = jnp.zeros((B, CARRY_ROWS, D_FF), F32)
    conv0_s = jnp.pad(state_ffn_conv[0], ((0, 0), (CARRY_ROWS - (CONV_W - 1), 0), (0, 0)))
    yp, latp, kpep, hgp, cvp = layer(x_prompt, mod[:B], jnp.arange(L), zeros_state, zeros_conv, None)
    ys, lats, kpes, hgs, cvs = layer(x_sample, mod[B:], past + jnp.arange(Ls), state_hgrn[0], conv0_s,
                                     (cache_kv_latent[0], jnp.swapaxes(cache_k_rope[0], 1, 2)))
    return (yp, ys, latp, kpep, hgp, cvp, lats, kpes, hgs, cvs)
```

```python
import functools

import numpy as np
import jax
import jax.numpy as jnp
from jax import lax
from jax.experimental import pallas as pl
from jax.experimental.pallas import tpu as pltpu

F32 = jnp.float32
BF16 = jnp.bfloat16

D_MODEL = 1024
CHUNK = 64
HG_HEADS = 4
HG_DK = 128
HG_DV = 128
HG_W = HG_HEADS * HG_DK
MLA_HEADS = 4
MLA_NOPE = 128
MLA_ROPE = 64
ROPE_HALF = MLA_ROPE // 2
MLA_V = 128
Q_LORA = 384
KV_LORA = 256
ROPE_THETA = 10000.0
MLA_SCALE = (MLA_NOPE + MLA_ROPE) ** -0.5
Q_SCALE = MLA_SCALE * float(np.log2(np.e))
QK_W = 2 * MLA_NOPE
D_FF = 2816
CONV_W = 3
EPS = 1e-6

SUB = 16
FF_TILE = D_FF
ATTN_LOOKAHEAD = 4
CARRY_ROWS = 8
VMEM_LIMIT = 56 * 1024 * 1024
UNIT = 128
SLABS = 8
SLAB_ROWS = UNIT // SLABS
N_LEVELS = 7
DIAG_LEVEL = N_LEVELS


def _dot(a, b):
    return jnp.dot(a, b, preferred_element_type=F32)


def _dot_nt(a, b):
    return lax.dot_general(a, b, (((1,), (1,)), ((), ())), preferred_element_type=F32)


def _dot_tn(a, b):
    return lax.dot_general(a, b, (((0,), (0,)), ((), ())), preferred_element_type=F32)


def _split3(x):
    x1 = x.astype(BF16)
    r1 = x - x1.astype(F32)
    x2 = r1.astype(BF16)
    r2 = r1 - x2.astype(F32)
    return x1, x2, r2.astype(BF16)


def _rms(x, g):
    return x * lax.rsqrt(jnp.mean(x * x, axis=-1, keepdims=True) + EPS) * g


def _const_spec(shape):
    n = len(shape)
    return pl.BlockSpec(shape, lambda *_: (0,) * n)


def _ada_body(c_ref, w_ref, b_ref, o_ref):
    c = c_ref[...]
    s = c * jax.nn.sigmoid(c)
    s1, s2, _ = _split3(s)
    w = w_ref[...]
    w1 = w.astype(BF16)
    w2 = (w - w1.astype(F32)).astype(BF16)
    o_ref[...] = _dot(s1, w1) + _dot(s1, w2) + _dot(s2, w1) + b_ref[...]


def _ada(c, w_ada, b_ada):
    n, d = c.shape
    nout = w_ada.shape[1]
    tn = 1024
    return pl.pallas_call(
        _ada_body,
        out_shape=jax.ShapeDtypeStruct((n, nout), F32),
        grid=(nout // tn,),
        in_specs=[pl.BlockSpec((n, d), lambda j: (0, 0)),
                  pl.BlockSpec((d, tn), lambda j: (0, j)),
                  pl.BlockSpec((1, tn), lambda j: (0, j))],
        out_specs=pl.BlockSpec((n, tn), lambda j: (0, j)),
        compiler_params=pltpu.CompilerParams(dimension_semantics=("arbitrary",)),
        name="adaln_mod",
    )(c, w_ada, b_ada)


def _rope_slots(r1, r2, nope, out_ref, slot):
    for h in range(MLA_HEADS):
        xh = jnp.where(slot == h, r1, jnp.where(slot == (h + 1) % MLA_HEADS, r2, 0.0))
        out_ref[0, h, :, 0:MLA_NOPE] = nope[:, MLA_NOPE * h:MLA_NOPE * (h + 1)].astype(BF16)
        out_ref[0, h, :, MLA_NOPE:QK_W] = xh.astype(BF16)


def _mla_stages(hb, refs, outs, rope_refs, q_transposed):
    (wcq_ref, wckv_ref, wkx_ref, wkpt_ref, qng_ref, kvng_ref, wuqn_ref, wuqr_ref, wuk_ref, wuvt_ref) = refs
    q_out, k_out, v_out, lat_out, kpe_out = outs
    cos_ref, sin_ref, cost_ref, sint_ref = rope_refs
    t = {}

    def project():
        t["cq"] = _dot(hb, wcq_ref[...])
        t["ckv"] = _dot(hb, wckv_ref[...])
        t["kx"] = _dot(hb, wkx_ref[...])
        t["kxt"] = _dot_nt(wkpt_ref[...], hb)

    def q_up():
        cq = _rms(t["cq"], qng_ref[...]).astype(BF16)
        if q_transposed:
            t["qn"] = _dot_nt(wuqn_ref[...], cq) * Q_SCALE
            t["qr"] = _dot_nt(wuqr_ref[...], cq) * Q_SCALE
        else:
            t["qn"] = _dot(cq, wuqn_ref[...]) * Q_SCALE
            t["qr"] = _dot(cq, wuqr_ref[...]) * Q_SCALE

    def kv_up():
        lat = _rms(t["ckv"], kvng_ref[...])
        lat_out[0] = lat
        latb = lat.astype(BF16)
        t["kn"] = _dot(latb, wuk_ref[...])
        v_out[0] = _dot_nt(wuvt_ref[...], latb).astype(BF16)

    def slot_of(cos):
        return lax.broadcasted_iota(jnp.int32, cos.shape, 1) // ROPE_HALF

    def q_rope():
        qr, qn = t["qr"], t["qn"]
        if q_transposed:
            cos = jnp.concatenate([cost_ref[...]] * MLA_HEADS, axis=0)
            sin = jnp.concatenate([sint_ref[...]] * MLA_HEADS, axis=0)
            qr1 = qr[0:128] * cos - qr[128:256] * sin
            qr2 = qr[384:512] * cos + qr[256:384] * sin
            slot = lax.broadcasted_iota(jnp.int32, cos.shape, 0) // ROPE_HALF
            for h in range(MLA_HEADS):
                xh = jnp.where(slot == h, qr1, jnp.where(slot == (h + 1) % MLA_HEADS, qr2, 0.0))
                q_out[0, h, 0:MLA_NOPE, :] = qn[MLA_NOPE * h:MLA_NOPE * (h + 1)].astype(BF16)
                q_out[0, h, MLA_NOPE:QK_W, :] = xh.astype(BF16)
            return
        cos, sin = cos_ref[...], sin_ref[...]
        qr1 = qr[:, 0:128] * cos - qr[:, 128:256] * sin
        qr2 = qr[:, 384:512] * cos + qr[:, 256:384] * sin
        _rope_slots(qr1, qr2, qn, q_out, slot_of(cos))

    def k_rope():
        cos, sin, kx = cos_ref[...], sin_ref[...], t["kx"]
        kr1 = kx[:, 0:128] * cos - kx[:, 128:256] * sin
        kr2 = kx[:, 128:256] * cos + kx[:, 0:128] * sin
        _rope_slots(kr1, kr2, t["kn"], k_out, slot_of(cos))
        kxt = t["kxt"]
        x1t, x2t = kxt[0:ROPE_HALF], kxt[ROPE_HALF:MLA_ROPE]
        cost, sint = cost_ref[...], sint_ref[...]
        kpe_out[0] = jnp.concatenate([x1t * cost - x2t * sint, x2t * cost + x1t * sint], axis=0)

    return [project, q_up, kv_up, q_rope, k_rope]


def _normed_input(x_ref, mod_ref, gmix_ref):
    mod = mod_ref[0]
    return (_rms(x_ref[0], gmix_ref[...]) * (1.0 + mod[1:2]) + mod[0:1]).astype(BF16)


def _win_views(win_ref):
    hgw = 4 * HG_W
    return (win_ref.at[:, 0:hgw], win_ref.at[:, hgw:hgw + Q_LORA],
            win_ref.at[:, hgw + Q_LORA:hgw + Q_LORA + KV_LORA])


def _inproj_body(x_ref, mod_ref, gmix_ref, win_ref, wkx_ref, wkpt_ref, qng_ref, kvng_ref,
                 wuqn_ref, wuqr_ref, wuk_ref, wuvt_ref, cos_ref, sin_ref, cost_ref, sint_ref,
                 hg_out, q_out, k_out, v_out, lat_out, kpe_out):
    whg_ref, wcq_ref, wckv_ref = _win_views(win_ref)
    hb = _normed_input(x_ref, mod_ref, gmix_ref)
    hg_out[0] = _dot(hb, whg_ref[...])
    for stage in _mla_stages(hb, (wcq_ref, wckv_ref, wkx_ref, wkpt_ref, qng_ref, kvng_ref, wuqn_ref, wuqr_ref,
                                  wuk_ref, wuvt_ref),
                             (q_out, k_out, v_out, lat_out, kpe_out), (cos_ref, sin_ref, cost_ref, sint_ref),
                             False):
        stage()


def _inproj_hgrn_body(x_ref, mod_ref, gmix_ref, win_ref, wkx_ref, wkpt_ref, qng_ref, kvng_ref,
                      wuqn_ref, wuqr_ref, wuk_ref, wuvt_ref, cos_ref, sin_ref, cost_ref, sint_ref,
                      perm_ref, lbl_ref, gain_ref, s0t_ref, tin_ref, lv_ref, unperm_ref,
                      hg_out, st_out, q_out, k_out, v_out, lat_out, kpe_out, raw_scr, st_scr, *, nt):
    j = pl.program_id(0)

    @pl.when(j == 0)
    def _():
        raw_scr[...] = jnp.zeros(raw_scr.shape, F32)

    first_of_batch = lax.rem(jnp.maximum(j - 1, 0), nt) == 0
    st = [jnp.where(first_of_batch, s0t_ref[0, hd], st_scr[hd]) for hd in range(HG_HEADS)]

    whg_ref, wcq_ref, wckv_ref = _win_views(win_ref)
    n_units = x_ref.shape[1] // UNIT
    t = {}

    def start():
        hb = _normed_input(x_ref, mod_ref, gmix_ref)
        t["hbp"] = jnp.concatenate([_dot(perm_ref[...], hb[UNIT * u:UNIT * (u + 1)]).astype(BF16)
                                    for u in range(n_units)], axis=0)
        t["mla"] = _mla_stages(hb, (wcq_ref, wckv_ref, wkx_ref, wkpt_ref, qng_ref, kvng_ref, wuqn_ref,
                                    wuqr_ref, wuk_ref, wuvt_ref),
                               (q_out, k_out, v_out, lat_out, kpe_out), (cos_ref, sin_ref, cost_ref, sint_ref),
                               True)

    def raw_group(k):
        def run():
            raw_scr[:, HG_W * k:HG_W * (k + 1)] = _dot(t["hbp"], whg_ref[:, HG_W * k:HG_W * (k + 1)])
        return run

    chunks = [start] + [raw_group(k) for k in range(4)] + [lambda n=n: t["mla"][n]() for n in range(5)]

    def next_chunk(_level):
        if chunks:
            chunks.pop(0)()

    st = _hgrn_units_math(lambda u, k: raw_scr[UNIT * u:UNIT * (u + 1), HG_W * k:HG_W * (k + 1)], n_units, st,
                          lbl_ref, gain_ref, tin_ref, lv_ref, unperm_ref, hg_out, next_chunk)
    while chunks:
        chunks.pop(0)()
    for hd in range(HG_HEADS):
        st_scr[hd] = st[hd]
        st_out[0, hd] = st[hd]


def _unit_perm(n):
    i = np.arange(n)
    src = UNIT * (i // UNIT) + _unit_time()[i % UNIT]
    return jnp.asarray(src[:, None] == i[None, :], BF16)


def _inproj_args(x, mod, gmix, wts, rope, q_transposed):
    (win, wkx, wkpt, qng, kvng, wuqn, wuqr, wuk, wuv) = wts
    if q_transposed:
        wuqn, wuqr = wuqn.T, wuqr.T
    return [x, mod, gmix, win, wkx, wkpt, qng, kvng, wuqn, wuqr, wuk, wuv.T, *rope]


def _inproj_out_shapes(B, L, q_transposed):
    q_shape = (B, MLA_HEADS, QK_W, L) if q_transposed else (B, MLA_HEADS, L, QK_W)
    return [(q_shape, BF16), ((B, MLA_HEADS, L, QK_W), BF16), ((B, MLA_HEADS * MLA_V, L), BF16),
            ((B, L, KV_LORA), F32), ((B, MLA_ROPE, L), F32)]


def _inproj(x, mod, gmix, wts, rope, tm):
    B, L, D = x.shape
    args = _inproj_args(x, mod, gmix, wts, rope, False)
    row = lambda w: pl.BlockSpec((1, tm, w), lambda b, i: (b, i, 0))
    col = lambda h: pl.BlockSpec((1, h, tm), lambda b, i: (b, 0, i))
    in_specs = [row(D), pl.BlockSpec((1, 6, D), lambda b, i: (b, 0, 0))]
    in_specs += [_const_spec(w.shape) for w in args[2:12]]
    in_specs += [pl.BlockSpec((tm, 128), lambda b, i: (i, 0))] * 2
    in_specs += [pl.BlockSpec((ROPE_HALF, tm), lambda b, i: (0, i))] * 2
    shapes = [((B, L, 4 * HG_W), F32)] + _inproj_out_shapes(B, L, False)
    heads = pl.BlockSpec((1, MLA_HEADS, tm, QK_W), lambda b, i: (b, 0, i, 0))
    out_specs = [row(4 * HG_W), heads, heads, col(MLA_HEADS * MLA_V),
                 row(KV_LORA), col(MLA_ROPE)]
    return pl.pallas_call(
        _inproj_body,
        out_shape=[jax.ShapeDtypeStruct(s, dt) for s, dt in shapes],
        grid=(B, L // tm),
        in_specs=in_specs,
        out_specs=out_specs,
        compiler_params=pltpu.CompilerParams(dimension_semantics=("arbitrary", "arbitrary"),
                                             vmem_limit_bytes=VMEM_LIMIT),
        name="in_proj",
    )(*args)


def _inproj_hgrn(x, mod, gmix, wts, rope, lb_logits, gain, s0, tm):
    B, L, D = x.shape
    nt = L // tm
    n = B * nt
    args = _inproj_args(x, mod, gmix, wts, rope, True)
    tin, lv = _unit_tables()
    perm = _unit_perm(UNIT)
    args += [perm, lb_logits, gain, jnp.swapaxes(s0, 2, 3), tin, lv, perm.T]
    cur = lambda j: jnp.minimum(j, n - 1)
    lag = lambda j: jnp.maximum(j - 1, 0)
    row = lambda w, t: pl.BlockSpec((1, tm, w), lambda j: (t(j) // nt, t(j) % nt, 0))
    col = lambda h: pl.BlockSpec((1, h, tm), lambda j: (cur(j) // nt, 0, cur(j) % nt))
    state = pl.BlockSpec((1, HG_HEADS, HG_DV, HG_DK), lambda j: (lag(j) // nt, 0, 0, 0))
    in_specs = [row(D, cur), pl.BlockSpec((1, 6, D), lambda j: (cur(j) // nt, 0, 0))]
    in_specs += [_const_spec(w.shape) for w in args[2:12]]
    in_specs += [pl.BlockSpec((tm, 128), lambda j: (cur(j) % nt, 0))] * 2
    in_specs += [pl.BlockSpec((ROPE_HALF, tm), lambda j: (0, cur(j) % nt))] * 2
    in_specs += [_const_spec(perm.shape), _const_spec(lb_logits.shape), _const_spec(gain.shape), state,
                 _const_spec(tin.shape), _const_spec(lv.shape), _const_spec(perm.shape)]
    shapes = [((B, L, HG_W), BF16), ((B, HG_HEADS, HG_DV, HG_DK), F32)] + _inproj_out_shapes(B, L, True)
    heads = pl.BlockSpec((1, MLA_HEADS, tm, QK_W), lambda j: (cur(j) // nt, 0, cur(j) % nt, 0))
    heads_t = pl.BlockSpec((1, MLA_HEADS, QK_W, tm), lambda j: (cur(j) // nt, 0, 0, cur(j) % nt))
    out_specs = [row(HG_W, lag), state, heads_t, heads,
                 col(MLA_HEADS * MLA_V), row(KV_LORA, cur), col(MLA_ROPE)]
    ohg, st, *mla = pl.pallas_call(
        functools.partial(_inproj_hgrn_body, nt=nt),
        out_shape=[jax.ShapeDtypeStruct(s, dt) for s, dt in shapes],
        grid=(n + 1,),
        in_specs=in_specs,
        out_specs=out_specs,
        scratch_shapes=[pltpu.VMEM((tm, 4 * HG_W), F32), pltpu.VMEM((HG_HEADS, HG_DV, HG_DK), F32)],
        compiler_params=pltpu.CompilerParams(dimension_semantics=("arbitrary",),
                                             vmem_limit_bytes=VMEM_LIMIT),
        name="in_proj_hgrn",
    )(*args)
    return (ohg, jnp.swapaxes(st, 2, 3), *mla)


def _hgrn_body(raw_ref, lbl_ref, gain_ref, s0_ref, tin_ref, tup_ref, ones_ref,
               o_out, s_out,
               st_scr, b_scr, q_scr, kk_scr, v_scr, eb_scr, qt_scr, kt_scr, vb_scr, p_scr, o_scr, *, th):
    i = pl.program_id(1)

    @pl.when(i == 0)
    def _():
        for h in range(HG_HEADS):
            st_scr[h] = s0_ref[0, h].T

    lbl = lbl_ref[...]
    e = jnp.exp(lbl - jnp.max(lbl, axis=0, keepdims=True))
    lb = (e / jnp.sum(e, axis=0, keepdims=True))[0:1]

    q = raw_ref[0, :, 0:HG_W] * HG_DK ** -0.5
    f = lb + (1.0 - lb) * jax.nn.sigmoid(raw_ref[0, :, HG_W:2 * HG_W])
    kk = 1.0 - f
    v = raw_ref[0, :, 2 * HG_W:3 * HG_W]
    l1, l2, l3 = _split3(jnp.log(f))
    tin = tin_ref[...]
    tup = tup_ref[...]
    b = _dot(tin, l1) + _dot(tin, l2) + _dot(tin, l3)
    c = _dot(tup, l1) + _dot(tup, l2) + _dot(tup, l3)
    eb = jnp.exp(b)
    b_scr[...] = b
    q_scr[...] = q
    kk_scr[...] = kk
    v_scr[...] = v
    eb_scr[...] = eb
    qt_scr[...] = (q * eb).astype(BF16)
    kt_scr[...] = (kk * jnp.exp(c)).astype(BF16)
    vb_scr[...] = v.astype(BF16)

    rowid = lax.broadcasted_iota(jnp.int32, (SUB, HG_W), 0)

    def step(j, carry):
        r0 = pl.multiple_of(j * SUB, SUB)
        bj = b_scr[pl.ds(r0, SUB), :]
        qj = q_scr[pl.ds(r0, SUB), :]
        for s in range(SUB):
            bs = b_scr[pl.ds(r0 + s, 1), :]
            ks = kk_scr[pl.ds(r0 + s, 1), :]
            dec = jnp.exp(jnp.where(rowid >= s, bj - bs, -jnp.inf))
            p_scr[s * SUB:(s + 1) * SUB, :] = (qj * dec * ks).astype(BF16)
        r = _dot(p_scr[...], ones_ref[...])
        od = jnp.zeros((SUB, HG_W), F32)
        for s in range(SUB):
            od = od + r[s * SUB:(s + 1) * SUB, :] * v_scr[pl.ds(r0 + s, 1), :]
        qt = qt_scr[pl.ds(r0, SUB), :]
        kt = kt_scr[pl.ds(r0, SUB), :]
        vb = vb_scr[pl.ds(r0, SUB), :]
        dj = eb_scr[pl.ds(r0 + SUB - 1, 1), :]
        for h in range(HG_HEADS):
            hs = slice(h * HG_DK, (h + 1) * HG_DK)
            st = st_scr[h]
            oi = _dot_nt(qt[:, hs], st.astype(BF16))
            st_scr[h] = st * dj[:, hs] + _dot_tn(vb[:, hs], kt[:, hs])
            o_scr[pl.ds(r0, SUB), hs] = oi + od[:, hs]
        return carry

    lax.fori_loop(0, th // SUB, step, 0)

    gate = raw_ref[0, :, 3 * HG_W:4 * HG_W]
    gate = gate * jax.nn.sigmoid(gate)
    gain = gain_ref[...]
    for h in range(HG_HEADS):
        hs = slice(h * HG_DV, (h + 1) * HG_DV)
        o_out[0, :, hs] = (_rms(o_scr[:, hs], gain) * gate[:, hs]).astype(BF16)

    @pl.when(i == pl.num_programs(1) - 1)
    def _():
        for h in range(HG_HEADS):
            s_out[0, h] = st_scr[h].T


def _step_matrices(th):
    r = np.arange(th)
    same = (r[:, None] // SUB) == (r[None, :] // SUB)
    tin = (same & (r[None, :] <= r[:, None])).astype(np.float32)
    tup = (same & (r[None, :] > r[:, None])).astype(np.float32)
    hd = np.arange(HG_W) // HG_DK
    ones = (hd[:, None] == hd[None, :]).astype(np.float32)
    return jnp.asarray(tin, BF16), jnp.asarray(tup, BF16), jnp.asarray(ones, BF16)


def _hgrn(raw, lb_logits, gain, s0, th):
    B, L, _ = raw.shape
    tin, tup, ones = _step_matrices(th)
    f32s = lambda: pltpu.VMEM((th, HG_W), F32)
    b16s = lambda: pltpu.VMEM((th, HG_W), BF16)
    return pl.pallas_call(
        functools.partial(_hgrn_body, th=th),
        out_shape=[jax.ShapeDtypeStruct((B, L, HG_W), BF16),
                   jax.ShapeDtypeStruct((B, HG_HEADS, HG_DK, HG_DV), F32)],
        grid=(B, L // th),
        in_specs=[pl.BlockSpec((1, th, 4 * HG_W), lambda b, i: (b, i, 0)),
                  _const_spec(lb_logits.shape), _const_spec(gain.shape),
                  pl.BlockSpec((1, HG_HEADS, HG_DK, HG_DV), lambda b, i: (b, 0, 0, 0)),
                  _const_spec(tin.shape), _const_spec(tup.shape), _const_spec(ones.shape)],
        out_specs=[pl.BlockSpec((1, th, HG_W), lambda b, i: (b, i, 0)),
                   pl.BlockSpec((1, HG_HEADS, HG_DK, HG_DV), lambda b, i: (b, 0, 0, 0))],
        scratch_shapes=[pltpu.VMEM((HG_HEADS, HG_DV, HG_DK), F32),
                        f32s(), f32s(), f32s(), f32s(), f32s(),
                        b16s(), b16s(), b16s(),
                        pltpu.VMEM((SUB * SUB, HG_W), BF16), f32s()],
        compiler_params=pltpu.CompilerParams(dimension_semantics=("arbitrary", "arbitrary"),
                                             vmem_limit_bytes=VMEM_LIMIT),
        name="hgrn_scan",
    )(raw, lb_logits, gain, s0, tin, tup, ones)


def _unit_time():
    i = np.arange(UNIT)
    return SLABS * (i % SLAB_ROWS) + i // SLAB_ROWS


def _unit_tables():
    t = _unit_time()
    tin = (t[None, :] <= t[:, None]).astype(np.float32)
    rows = []
    for c in (1, 2, 4, 8):
        g = np.arange(SLAB_ROWS)
        t_bnd = SLABS * (2 * c * (g // (2 * c)) + c - 1) + SLABS - 1
        rows.append((t[None, :] <= t_bnd[:, None]).astype(np.float32))
    tin_ext = np.concatenate([tin] + rows, axis=0)
    x = t[:, None] ^ t[None, :]
    lv = np.where(x > 0, np.floor(np.log2(np.maximum(x, 1))), DIAG_LEVEL).astype(np.int32)
    lv = np.where(t[None, :] <= t[:, None], lv, -1)
    return jnp.asarray(tin_ext, BF16), jnp.asarray(np.concatenate([lv, lv], axis=1), jnp.int32)


def _pair_blockdiag(a, b):
    z = jnp.zeros_like(a)
    return jnp.concatenate([jnp.concatenate([a, z], axis=1), jnp.concatenate([z, b], axis=1)], axis=0)


def _hgrn_units_math(field, nu, st, lbl_ref, gain_ref, tin_ref, lv_ref, unperm_ref, o_out, after_level):
    lbl = lbl_ref[...]
    e = jnp.exp(lbl - jnp.max(lbl, axis=0, keepdims=True))
    lb = (e / jnp.sum(e, axis=0, keepdims=True))[0:1]
    tin = tin_ref[...]
    lv = lv_ref[...]
    gain = gain_ref[...]

    units = range(nu)
    pairs = range(HG_HEADS // 2)

    def pair_cols(x, hp):
        c0 = 2 * HG_DK * hp
        return x[:, c0:c0 + HG_DK], x[:, c0 + HG_DK:c0 + 2 * HG_DK]

    gates = [field(u, 3) for u in units]
    q = [field(u, 0) * HG_DK ** -0.5 for u in units]
    f = [lb + (1.0 - lb) * jax.nn.sigmoid(field(u, 1)) for u in units]
    kk = [1.0 - f[u] for u in units]
    lsplit = [_split3(jnp.log(f[u]) * np.float32(1.0 / np.log(2.0))) for u in units]
    bx = [_dot(tin, lsplit[u][0]) + _dot(tin, lsplit[u][1]) + _dot(tin, lsplit[u][2]) for u in units]
    b = [bx[u][0:UNIT] for u in units]
    bs = [[b[u][SLAB_ROWS * p:SLAB_ROWS * (p + 1)] for p in range(SLABS)] for u in units]
    vb = [field(u, 2).astype(BF16) for u in units]
    qb = [q[u].astype(BF16) for u in units]
    kkb = [kk[u].astype(BF16) for u in units]

    def slab_decay(u, level, p):
        if level < 3:
            half = 1 << level
            ref = p - p % (2 * half) + half - 1
            if p == ref:
                return jnp.ones_like(bs[u][p])
            return jnp.exp2(bs[u][p] - bs[u][ref] if p > ref else bs[u][ref] - bs[u][p])
        r = bx[u][UNIT + SLAB_ROWS * (level - 3):UNIT + SLAB_ROWS * (level - 2)]
        if level == N_LEVELS - 1:
            h = SLAB_ROWS // 2
            return jnp.concatenate([jnp.exp2(r[:h] - bs[u][p][:h]), jnp.exp2(bs[u][p][h:] - r[h:])], axis=0)
        return jnp.exp2(-jnp.abs(bs[u][p] - r))

    acc = [[jnp.zeros((UNIT, 2 * HG_DK), F32) for _ in pairs] for _ in units]
    for level in range(N_LEVELS + 1):
        here = lv == level
        for u in units:
            if level == DIAG_LEVEL:
                qt, kt = qb[u], kkb[u]
            else:
                dec = jnp.concatenate([slab_decay(u, level, p) for p in range(SLABS)], axis=0).astype(BF16)
                qt, kt = qb[u] * dec, kkb[u] * dec
            for hp in pairs:
                a2 = _dot_nt(qt[:, 2 * HG_DK * hp:2 * HG_DK * (hp + 1)], _pair_blockdiag(*pair_cols(kt, hp)))
                acc[u][hp] = jnp.where(here, a2, acc[u][hp])
        after_level(level)

    b_end = [b[u][UNIT - 1:UNIT] for u in units]
    qi = [(q[u] * jnp.exp2(b[u])).astype(BF16) for u in units]
    ks = [(kk[u] * jnp.exp2(b_end[u] - b[u])).astype(BF16) for u in units]
    d = [jnp.exp2(b_end[u]) for u in units]
    o_intra = [[_dot(acc[u][hp].astype(BF16), _pair_blockdiag(*pair_cols(vb[u], hp))) for hp in pairs]
               for u in units]
    ds = [[_dot_tn(vb[u][:, h * HG_DV:(h + 1) * HG_DV], ks[u][:, h * HG_DK:(h + 1) * HG_DK])
           for h in range(HG_HEADS)] for u in units]
    o_inter = []
    for u in units:
        o_inter.append([_dot_nt(qi[u][:, 2 * HG_DK * hp:2 * HG_DK * (hp + 1)],
                                _pair_blockdiag(st[2 * hp].astype(BF16), st[2 * hp + 1].astype(BF16)))
                        for hp in pairs])
        st = [st[h] * d[u][:, h * HG_DK:(h + 1) * HG_DK] + ds[u][h] for h in range(HG_HEADS)]
    res = []
    for u in units:
        o = jnp.concatenate([o_intra[u][hp] + o_inter[u][hp] for hp in pairs], axis=1)
        gate = gates[u] * jax.nn.sigmoid(gates[u])
        r = jnp.concatenate([_rms(o[:, h * HG_DV:(h + 1) * HG_DV], gain) for h in range(HG_HEADS)], axis=1)
        res.append((r * gate).astype(BF16))
    for u in units:
        o_out[0, UNIT * u:UNIT * (u + 1), :] = _dot(unperm_ref[...], res[u]).astype(BF16)
    return st


def _attn_body(qt_ref, k_ref, vt_ref, o_ref, acc_scr, *, tq, nq):
    for qi in range(nq):
        qt = qt_ref[0, 0, :, qi * tq:(qi + 1) * tq]
        nblk = qi + 1
        scores = lambda i: _dot(k_ref[0, 0, i * tq:(i + 1) * tq, :], qt)
        pending = [scores(i) for i in range(min(ATTN_LOOKAHEAD, nblk))]
        m = jnp.full((1, tq), -jnp.inf, F32)
        l = jnp.zeros((1, tq), F32)
        for ki in range(nblk):
            st = pending.pop(0)
            if ki + ATTN_LOOKAHEAD < nblk:
                pending.append(scores(ki + ATTN_LOOKAHEAD))
            if ki == nblk - 1:
                kc = lax.broadcasted_iota(jnp.int32, st.shape, 0) // CHUNK
                qc = lax.broadcasted_iota(jnp.int32, st.shape, 1) // CHUNK
                st = jnp.where(kc <= qc, st, -jnp.inf)
            m_new = jnp.maximum(m, jnp.max(st, axis=0, keepdims=True))
            p = jnp.exp2(st - m_new)
            alpha = jnp.exp2(m - m_new)
            l = alpha * l + jnp.sum(p, axis=0, keepdims=True)
            pv = _dot(vt_ref[0, :, ki * tq:(ki + 1) * tq], p.astype(BF16))
            if ki == 0:
                acc_scr[qi % 2] = pv
            else:
                acc_scr[qi % 2] = alpha * acc_scr[qi % 2] + pv
            m = m_new
        o_ref[0, 0, qi * tq:(qi + 1) * tq, :] = (acc_scr[qi % 2] / l).T.astype(BF16)


def _attn(qt, k, vt, tq):
    B, _, L, _ = k.shape
    per_head = lambda w: pl.BlockSpec((1, 1, L, w), lambda b, h: (b, h, 0, 0))
    return pl.pallas_call(
        functools.partial(_attn_body, tq=tq, nq=L // tq),
        out_shape=jax.ShapeDtypeStruct((B, MLA_HEADS, L, MLA_V), BF16),
        grid=(B, MLA_HEADS),
        in_specs=[pl.BlockSpec((1, 1, QK_W, L), lambda b, h: (b, h, 0, 0)), per_head(QK_W),
                  pl.BlockSpec((1, MLA_V, L), lambda b, h: (b, h, 0))],
        out_specs=per_head(MLA_V),
        scratch_shapes=[pltpu.VMEM((2, MLA_V, tq), F32)],
        compiler_params=pltpu.CompilerParams(dimension_semantics=("arbitrary",) * 2,
                                             vmem_limit_bytes=VMEM_LIMIT),
        name="mla_attn_prompt",
    )(qt, k, vt)


def _attn_cache_body(q_ref, kn_ref, vn_ref, clat_ref, ckpe_ref, wuk_ref, wuv_ref, place_ref, o_ref):
    n = q_ref.shape[2]
    latc = clat_ref[0].astype(BF16)
    kpec = ckpe_ref[0].astype(BF16)
    qlat, qrope, s_new = [], [], []
    for h in range(MLA_HEADS):
        qh = q_ref[0, h]
        ws = slice(MLA_NOPE * h, MLA_NOPE * (h + 1))
        qlat.append(_dot_nt(qh[:, 0:MLA_NOPE], wuk_ref[:, ws]).astype(BF16))
        qrope.append(_dot_nt(qh[:, MLA_NOPE:QK_W], place_ref[:, ws]).astype(BF16))
        s_new.append(_dot_nt(qh, kn_ref[0, h]))
    s_new = jnp.concatenate(s_new, axis=0)
    s_c = (_dot_nt(jnp.concatenate(qlat, axis=0), latc)
           + _dot(jnp.concatenate(qrope, axis=0), kpec))
    m = jnp.maximum(jnp.max(s_c, axis=-1, keepdims=True), jnp.max(s_new, axis=-1, keepdims=True))
    p_c = jnp.exp2(s_c - m)
    p_new = jnp.exp2(s_new - m)
    l = jnp.sum(p_c, axis=-1, keepdims=True) + jnp.sum(p_new, axis=-1, keepdims=True)
    o_lat = _dot(p_c.astype(BF16), latc).astype(BF16)
    p_new = p_new.astype(BF16)
    for h in range(MLA_HEADS):
        ws = slice(MLA_V * h, MLA_V * (h + 1))
        rs = slice(n * h, n * (h + 1))
        acc = _dot(o_lat[rs], wuv_ref[:, ws]) + _dot_nt(p_new[rs], vn_ref[0, ws, :])
        o_ref[0, h] = (acc / l[rs]).astype(BF16)


def _rope_place():
    p = np.zeros((MLA_ROPE, MLA_HEADS * MLA_NOPE), np.float32)
    for h in range(MLA_HEADS):
        for i in range(ROPE_HALF):
            p[i, MLA_NOPE * h + ROPE_HALF * h + i] = 1.0
            p[ROPE_HALF + i, MLA_NOPE * h + ROPE_HALF * ((h + 1) % MLA_HEADS) + i] = 1.0
    return jnp.asarray(p, BF16)


def _attn_cache(q, kn, vn, clat, ckpe, wuk, wuv):
    B, _, L, _ = q.shape
    P = clat.shape[1]
    place = _rope_place()
    bspec = lambda n, w: pl.BlockSpec((1, n, w), lambda b: (b, 0, 0))
    heads = lambda w: pl.BlockSpec((1, MLA_HEADS, L, w), lambda b: (b, 0, 0, 0))
    return pl.pallas_call(
        _attn_cache_body,
        out_shape=jax.ShapeDtypeStruct((B, MLA_HEADS, L, MLA_V), BF16),
        grid=(B,),
        in_specs=[heads(QK_W), heads(QK_W), bspec(MLA_HEADS * MLA_V, L),
                  bspec(P, KV_LORA), bspec(MLA_ROPE, P),
                  _const_spec(wuk.shape), _const_spec(wuv.shape), _const_spec(place.shape)],
        out_specs=heads(MLA_V),
        compiler_params=pltpu.CompilerParams(dimension_semantics=("arbitrary",),
                                             vmem_limit_bytes=VMEM_LIMIT),
        name="mla_attn_cache",
    )(q, kn, vn, clat, ckpe, wuk, wuv, place)


def _ffn_body(x_ref, ohg_ref, omla_ref, mod_ref, conv0_ref, wout_ref, gffn_ref, wup_ref, cw_ref, cb_ref,
              wdn_ref, gfin_ref, y_ref, conv_out, carry_scr, a_scr, u_scr, *, tm, ft):
    @pl.when(pl.program_id(1) == 0)
    def _():
        carry_scr[...] = conv0_ref[0]

    mod = mod_ref[0]
    g1 = mod[2:3]
    sh2 = mod[3:4]
    sc2 = mod[4:5]
    g2 = mod[5:6]
    omla = jnp.concatenate([omla_ref[0, h] for h in range(MLA_HEADS)], axis=1)
    o = _dot(ohg_ref[0], wout_ref[0:HG_W, :]) + _dot(omla, wout_ref[HG_W:2 * HG_W, :])
    x1 = x_ref[0] + g1 * o
    h2 = (_rms(x1, gffn_ref[...]) * (1.0 + sc2) + sh2).astype(BF16)
    for j in range(D_FF // ft):
        cs = slice(j * ft, (j + 1) * ft)
        a = _dot(h2, wup_ref[:, cs])
        v = _dot(h2, wup_ref[:, D_FF + j * ft:D_FF + (j + 1) * ft])
        a_scr[0:CARRY_ROWS, :] = carry_scr[:, cs]
        a_scr[CARRY_ROWS:CARRY_ROWS + tm, :] = a
        carry_scr[:, cs] = a[tm - CARRY_ROWS:tm]
        cw = cw_ref[:, cs]
        conv = (cb_ref[:, cs] + cw[0:1] * a_scr[CARRY_ROWS - 2:CARRY_ROWS - 2 + tm, :]
                + cw[1:2] * a_scr[CARRY_ROWS - 1:CARRY_ROWS - 1 + tm, :] + cw[2:3] * a)
        g = 0.5 * conv * (1.0 + lax.erf(conv * np.float32(np.sqrt(0.5))))
        u_scr[:, cs] = (g * v).astype(BF16)
        conv_out[0, :, cs] = a[tm - (CONV_W - 1):tm]

    x2 = x1 + g2 * _dot(u_scr[...], wdn_ref[...])
    y_ref[0] = _rms(x2, gfin_ref[...])


def _ffn(x, ohg, omla, mod, conv0, wout, gffn, wup, cw, cb, wdn, gfin, tm, ft):
    B, L, D = x.shape
    row = lambda w: pl.BlockSpec((1, tm, w), lambda b, i: (b, i, 0))
    once = lambda a: pl.BlockSpec(a.shape, lambda b, i: (0,) * a.ndim, pipeline_mode=pl.Buffered(1))
    return pl.pallas_call(
        functools.partial(_ffn_body, tm=tm, ft=ft),
        out_shape=[jax.ShapeDtypeStruct((B, L, D), F32),
                   jax.ShapeDtypeStruct((B, CONV_W - 1, D_FF), F32)],
        grid=(B, L // tm),
        in_specs=[row(D), row(HG_W), pl.BlockSpec((1, MLA_HEADS, tm, MLA_V), lambda b, i: (b, 0, i, 0)),
                  pl.BlockSpec((1, 6, D), lambda b, i: (b, 0, 0)),
                  pl.BlockSpec((1, CARRY_ROWS, D_FF), lambda b, i: (b, 0, 0)),
                  once(wout), once(gffn), once(wup), once(cw), once(cb), once(wdn), once(gfin)],
        out_specs=[row(D), pl.BlockSpec((1, CONV_W - 1, D_FF), lambda b, i: (b, 0, 0))],
        scratch_shapes=[pltpu.VMEM((CARRY_ROWS, D_FF), F32), pltpu.VMEM((CARRY_ROWS + tm, ft), F32),
                        pltpu.VMEM((tm, D_FF), BF16)],
        compiler_params=pltpu.CompilerParams(dimension_semantics=("arbitrary", "arbitrary"),
                                             vmem_limit_bytes=VMEM_LIMIT),
        name="out_ffn",
    )(x, ohg, omla, mod, conv0, wout, gffn, wup, cw, cb, wdn, gfin)


def _prep_weights(w_in, mla_q_norm_gain, mla_kv_norm_gain, w_uq, w_uk, w_uv):
    win = w_in.astype(BF16)
    kp = win[:, 4 * HG_W + Q_LORA + KV_LORA:]
    wkx = jnp.concatenate([jnp.tile(kp[:, :ROPE_HALF], (1, MLA_HEADS)),
                           jnp.tile(kp[:, ROPE_HALF:], (1, MLA_HEADS))], axis=1)
    uq = w_uq.reshape(Q_LORA, MLA_HEADS, MLA_NOPE + MLA_ROPE)
    wuqn = uq[:, :, :MLA_NOPE].reshape(Q_LORA, MLA_HEADS * MLA_NOPE).astype(BF16)
    x1a = uq[:, :, MLA_NOPE:MLA_NOPE + ROPE_HALF].reshape(Q_LORA, MLA_HEADS * ROPE_HALF)
    x2a = uq[:, :, MLA_NOPE + ROPE_HALF:].reshape(Q_LORA, MLA_HEADS * ROPE_HALF)
    wuqr = jnp.concatenate([x1a, x2a, jnp.roll(x1a, ROPE_HALF, axis=1), jnp.roll(x2a, ROPE_HALF, axis=1)],
                           axis=1).astype(BF16)
    wuk = w_uk.reshape(KV_LORA, MLA_HEADS * MLA_NOPE).astype(BF16)
    wuv = w_uv.reshape(KV_LORA, MLA_HEADS * MLA_V).astype(BF16)
    return (win, wkx, kp.T, mla_q_norm_gain.reshape(1, -1), mla_kv_norm_gain.reshape(1, -1), wuqn, wuqr, wuk, wuv)


def _rope_tables(pos):
    inv_freq = ROPE_THETA ** (-jnp.arange(ROPE_HALF, dtype=F32) / ROPE_HALF)
    ang = pos.astype(F32)[:, None] * inv_freq[None, :]
    cos, sin = jnp.cos(ang), jnp.sin(ang)
    return jnp.tile(cos, (1, MLA_HEADS)), jnp.tile(sin, (1, MLA_HEADS)), cos.T, sin.T


def _tile(n, pref):
    return pref if n % pref == 0 else n


def kernel(x_prompt, x_sample, c_prompt, c_sample, cache_kv_latent, cache_k_rope, state_hgrn, state_ffn_conv, w_ada, b_ada, norm_mix_gain, w_in, hg_lb_logits, hg_norm_gain, mla_q_norm_gain, mla_kv_norm_gain, w_uq, w_uk, w_uv, w_out, norm_ffn_gain, w_up, conv_w, conv_b, w_down, final_norm_gain):
    assert w_ada.shape[0] == 1, "single-layer trunk"
    B, L, D = x_prompt.shape
    Bs, Ls, _ = x_sample.shape
    past = cache_kv_latent.shape[2]

    mod = _ada(jnp.concatenate([c_prompt, c_sample], axis=0), w_ada[0], b_ada)
    mod = mod.reshape(B + Bs, 6, D)
    wts = _prep_weights(w_in[0], mla_q_norm_gain[0], mla_kv_norm_gain[0], w_uq[0], w_uk[0], w_uv[0])
    gmix = norm_mix_gain
    wout = w_out[0].astype(BF16)
    wup = w_up[0].astype(BF16)
    wdn = w_down[0].astype(BF16)
    gfin = final_norm_gain.reshape(1, D)
    ffn_w = (wout, norm_ffn_gain, wup, conv_w[0], conv_b, wdn, gfin)

    def layer(x, mod_x, pos, s0, conv0, cache):
        n, l, _ = x.shape
        rope = _rope_tables(pos)
        tm = _tile(l, 512)
        if tm % UNIT == 0:
            ohg, s_new, q, k, v, lat, kpe = _inproj_hgrn(x, mod_x, gmix, wts, rope, hg_lb_logits, hg_norm_gain,
                                                         s0, tm)
        else:
            raw, q, k, v, lat, kpe = _inproj(x, mod_x, gmix, wts, rope, tm)
            ohg, s_new = _hgrn(raw, hg_lb_logits, hg_norm_gain, s0, _tile(l, 256))
        if cache is None:
            omla = _attn(q, k, v, _tile(l, 512))
        else:
            omla = _attn_cache(q, k, v, cache[0], cache[1], wts[7], wts[8])
        y, conv_new = _ffn(x, ohg, omla, mod_x, conv0, *ffn_w, _tile(l, 512), FF_TILE)
        return y, lat[None], jnp.swapaxes(kpe, 1, 2)[None], s_new[None], conv_new[None]

    zeros_state = jnp.zeros((B, HG_HEADS, HG_DK, HG_DV), F32)
    zeros_conv = jnp.zeros((B, CARRY_ROWS, D_FF), F32)
    conv0_s = jnp.pad(state_ffn_conv[0], ((0, 0), (CARRY_ROWS - (CONV_W - 1), 0), (0, 0)))
    yp, latp, kpep, hgp, cvp = layer(x_prompt, mod[:B], jnp.arange(L), zeros_state, zeros_conv, None)
    ys, lats, kpes, hgs, cvs = layer(x_sample, mod[B:], past + jnp.arange(Ls), state_hgrn[0], conv0_s,
                                     (cache_kv_latent[0], jnp.swapaxes(cache_k_rope[0], 1, 2)))
    return (yp, ys, latp, kpep, hgp, cvp, lats, kpes, hgs, cvs)
```

```python
import functools

import numpy as np
import jax
import jax.numpy as jnp
from jax import lax
from jax.experimental import pallas as pl
from jax.experimental.pallas import tpu as pltpu

F32 = jnp.float32
BF16 = jnp.bfloat16

D_MODEL = 1024
CHUNK = 64
HG_HEADS = 4
HG_DK = 128
HG_DV = 128
HG_W = HG_HEADS * HG_DK
MLA_HEADS = 4
MLA_NOPE = 128
MLA_ROPE = 64
ROPE_HALF = MLA_ROPE // 2
MLA_V = 128
Q_LORA = 384
KV_LORA = 256
ROPE_THETA = 10000.0
MLA_SCALE = (MLA_NOPE + MLA_ROPE) ** -0.5
Q_SCALE = MLA_SCALE * float(np.log2(np.e))
QK_W = 2 * MLA_NOPE
D_FF = 2816
CONV_W = 3
EPS = 1e-6

SUB = 16
FF_TILE = D_FF
ATTN_LOOKAHEAD = 4
CARRY_ROWS = 8
VMEM_LIMIT = 56 * 1024 * 1024
UNIT = 128
SLABS = 8
SLAB_ROWS = UNIT // SLABS
N_LEVELS = 7
DIAG_LEVEL = N_LEVELS


def _dot(a, b):
    return jnp.dot(a, b, preferred_element_type=F32)


def _dot_nt(a, b):
    return lax.dot_general(a, b, (((1,), (1,)), ((), ())), preferred_element_type=F32)


def _dot_tn(a, b):
    return lax.dot_general(a, b, (((0,), (0,)), ((), ())), preferred_element_type=F32)


def _split3(x):
    x1 = x.astype(BF16)
    r1 = x - x1.astype(F32)
    x2 = r1.astype(BF16)
    r2 = r1 - x2.astype(F32)
    return x1, x2, r2.astype(BF16)


def _split2(x):
    x1 = x.astype(BF16)
    return x1, (x - x1.astype(F32)).astype(BF16)


def _rms(x, g):
    return x * lax.rsqrt(jnp.mean(x * x, axis=-1, keepdims=True) + EPS) * g


def _const_spec(shape):
    n = len(shape)
    return pl.BlockSpec(shape, lambda *_: (0,) * n)


def _ada_body(c_ref, w_ref, b_ref, o_ref):
    c = c_ref[...]
    s = c * jax.nn.sigmoid(c)
    s1, s2, _ = _split3(s)
    w = w_ref[...]
    w1 = w.astype(BF16)
    w2 = (w - w1.astype(F32)).astype(BF16)
    o_ref[...] = _dot(s1, w1) + _dot(s1, w2) + _dot(s2, w1) + b_ref[...]


def _ada(c, w_ada, b_ada):
    n, d = c.shape
    nout = w_ada.shape[1]
    tn = 1024
    return pl.pallas_call(
        _ada_body,
        out_shape=jax.ShapeDtypeStruct((n, nout), F32),
        grid=(nout // tn,),
        in_specs=[pl.BlockSpec((n, d), lambda j: (0, 0)),
                  pl.BlockSpec((d, tn), lambda j: (0, j)),
                  pl.BlockSpec((1, tn), lambda j: (0, j))],
        out_specs=pl.BlockSpec((n, tn), lambda j: (0, j)),
        compiler_params=pltpu.CompilerParams(dimension_semantics=("arbitrary",)),
        name="adaln_mod",
    )(c, w_ada, b_ada)


def _rope_slots(r1, r2, nope, out_ref, slot):
    for h in range(MLA_HEADS):
        xh = jnp.where(slot == h, r1, jnp.where(slot == (h + 1) % MLA_HEADS, r2, 0.0))
        out_ref[0, h, :, 0:MLA_NOPE] = nope[:, MLA_NOPE * h:MLA_NOPE * (h + 1)].astype(BF16)
        out_ref[0, h, :, MLA_NOPE:QK_W] = xh.astype(BF16)


def _mla_stages(hb, refs, outs, rope_refs, q_transposed):
    (wcq_ref, wckv_ref, wkx_ref, wkpt_ref, qng_ref, kvng_ref, wuqn_ref, wuqr_ref, wuk_ref, wuvt_ref) = refs
    q_out, k_out, v_out, lat_out, kpe_out = outs
    cos_ref, sin_ref, cost_ref, sint_ref = rope_refs
    t = {}

    def project():
        t["cq"] = _dot(hb, wcq_ref[...])
        t["ckv"] = _dot(hb, wckv_ref[...])
        t["kx"] = _dot(hb, wkx_ref[...])
        if not q_transposed:
            t["kxt"] = _dot_nt(wkpt_ref[...], hb)

    def q_up():
        cq = _rms(t["cq"], qng_ref[...])
        if q_transposed:
            cq = cq.T.astype(BF16)
            t["qn"] = _dot(wuqn_ref[...], cq) * Q_SCALE
            t["qr"] = _dot(wuqr_ref[...], cq) * Q_SCALE
        else:
            cq = cq.astype(BF16)
            t["qn"] = _dot(cq, wuqn_ref[...]) * Q_SCALE
            t["qr"] = _dot(cq, wuqr_ref[...]) * Q_SCALE

    def kv_up():
        lat = _rms(t["ckv"], kvng_ref[...])
        lat_out[0] = lat
        latb = lat.astype(BF16)
        t["kn"] = _dot(latb, wuk_ref[...])
        if q_transposed:
            v_out[0] = _dot(wuvt_ref[...], lat.T.astype(BF16)).astype(BF16)
        else:
            v_out[0] = _dot_nt(wuvt_ref[...], latb).astype(BF16)

    def slot_of(cos):
        return lax.broadcasted_iota(jnp.int32, cos.shape, 1) // ROPE_HALF

    def q_rope():
        qr, qn = t["qr"], t["qn"]
        if q_transposed:
            cos = jnp.concatenate([cost_ref[...]] * MLA_HEADS, axis=0)
            sin = jnp.concatenate([sint_ref[...]] * MLA_HEADS, axis=0)
            qr1 = qr[0:128] * cos - qr[128:256] * sin
            qr2 = qr[384:512] * cos + qr[256:384] * sin
            slot = lax.broadcasted_iota(jnp.int32, cos.shape, 0) // ROPE_HALF
            for h in range(MLA_HEADS):
                xh = jnp.where(slot == h, qr1, jnp.where(slot == (h + 1) % MLA_HEADS, qr2, 0.0))
                q_out[0, h, 0:MLA_NOPE, :] = qn[MLA_NOPE * h:MLA_NOPE * (h + 1)].astype(BF16)
                q_out[0, h, MLA_NOPE:QK_W, :] = xh.astype(BF16)
            return
        cos, sin = cos_ref[...], sin_ref[...]
        qr1 = qr[:, 0:128] * cos - qr[:, 128:256] * sin
        qr2 = qr[:, 384:512] * cos + qr[:, 256:384] * sin
        _rope_slots(qr1, qr2, qn, q_out, slot_of(cos))

    def k_rope():
        cos, sin, kx = cos_ref[...], sin_ref[...], t["kx"]
        kr1 = kx[:, 0:128] * cos - kx[:, 128:256] * sin
        kr2 = kx[:, 128:256] * cos + kx[:, 0:128] * sin
        _rope_slots(kr1, kr2, t["kn"], k_out, slot_of(cos))
        if q_transposed:
            lane = lax.broadcasted_iota(jnp.int32, cos.shape, 1)
            kpe_out[0] = jnp.where(lane < ROPE_HALF, kr1, kr2).T[0:MLA_ROPE]
            return
        kxt = t["kxt"]
        x1t, x2t = kxt[0:ROPE_HALF], kxt[ROPE_HALF:MLA_ROPE]
        cost, sint = cost_ref[...], sint_ref[...]
        kpe_out[0] = jnp.concatenate([x1t * cost - x2t * sint, x2t * cost + x1t * sint], axis=0)

    return [project, q_up, kv_up, q_rope, k_rope]


def _normed_input(x_ref, mod_ref, gmix_ref):
    mod = mod_ref[0]
    return (_rms(x_ref[0], gmix_ref[...]) * (1.0 + mod[1:2]) + mod[0:1]).astype(BF16)


def _win_views(win_ref):
    hgw = 4 * HG_W
    return (win_ref.at[:, 0:hgw], win_ref.at[:, hgw:hgw + Q_LORA],
            win_ref.at[:, hgw + Q_LORA:hgw + Q_LORA + KV_LORA])


def _inproj_body(x_ref, mod_ref, gmix_ref, win_ref, wkx_ref, wkpt_ref, qng_ref, kvng_ref,
                 wuqn_ref, wuqr_ref, wuk_ref, wuvt_ref, cos_ref, sin_ref, cost_ref, sint_ref,
                 hg_out, q_out, k_out, v_out, lat_out, kpe_out):
    whg_ref, wcq_ref, wckv_ref = _win_views(win_ref)
    hb = _normed_input(x_ref, mod_ref, gmix_ref)
    hg_out[0] = _dot(hb, whg_ref[...])
    for stage in _mla_stages(hb, (wcq_ref, wckv_ref, wkx_ref, wkpt_ref, qng_ref, kvng_ref, wuqn_ref, wuqr_ref,
                                  wuk_ref, wuvt_ref),
                             (q_out, k_out, v_out, lat_out, kpe_out), (cos_ref, sin_ref, cost_ref, sint_ref),
                             False):
        stage()


def _inproj_hgrn_body(x_ref, mod_ref, gmix_ref, win_ref, wkx_ref, wkpt_ref, qng_ref, kvng_ref,
                      wuqn_ref, wuqr_ref, wuk_ref, wuvt_ref, cos_ref, sin_ref, cost_ref, sint_ref,
                      perm_ref, lbl_ref, gain_ref, s0t_ref, tin_ref, lv_ref, unperm_ref,
                      hg_out, st_out, q_out, k_out, v_out, lat_out, kpe_out, raw_scr, st_scr, *, nt):
    j = pl.program_id(0)

    @pl.when(j == 0)
    def _():
        raw_scr[...] = jnp.zeros(raw_scr.shape, F32)

    first_of_batch = lax.rem(jnp.maximum(j - 1, 0), nt) == 0
    st = [jnp.where(first_of_batch, s0t_ref[0, hd], st_scr[hd]) for hd in range(HG_HEADS)]

    whg_ref, wcq_ref, wckv_ref = _win_views(win_ref)
    n_units = x_ref.shape[1] // UNIT
    t = {}

    def start():
        hb = _normed_input(x_ref, mod_ref, gmix_ref)
        t["hbp"] = jnp.concatenate([_dot(perm_ref[...], hb[UNIT * u:UNIT * (u + 1)]).astype(BF16)
                                    for u in range(n_units)], axis=0)
        t["mla"] = _mla_stages(hb, (wcq_ref, wckv_ref, wkx_ref, wkpt_ref, qng_ref, kvng_ref, wuqn_ref,
                                    wuqr_ref, wuk_ref, wuvt_ref),
                               (q_out, k_out, v_out, lat_out, kpe_out), (cos_ref, sin_ref, cost_ref, sint_ref),
                               True)

    def raw_group(k):
        def run():
            raw_scr[:, HG_W * k:HG_W * (k + 1)] = _dot(t["hbp"], whg_ref[:, HG_W * k:HG_W * (k + 1)])
        return run

    chunks = [start] + [raw_group(k) for k in range(4)] + [lambda n=n: t["mla"][n]() for n in range(5)]

    def next_chunk(_level):
        if chunks:
            chunks.pop(0)()

    st = _hgrn_units_math(lambda u, k: raw_scr[UNIT * u:UNIT * (u + 1), HG_W * k:HG_W * (k + 1)], n_units, st,
                          lbl_ref, gain_ref, tin_ref, lv_ref, unperm_ref, hg_out, next_chunk)
    while chunks:
        chunks.pop(0)()
    for hd in range(HG_HEADS):
        st_scr[hd] = st[hd]
        st_out[0, hd] = st[hd]


def _unit_perm(n):
    i = np.arange(n)
    src = UNIT * (i // UNIT) + _unit_time()[i % UNIT]
    return jnp.asarray(src[:, None] == i[None, :], BF16)


def _inproj_args(x, mod, gmix, wts, rope, q_transposed):
    (win, wkx, wkpt, qng, kvng, wuqn, wuqr, wuk, wuv) = wts
    if q_transposed:
        wuqn, wuqr = wuqn.T, wuqr.T
    return [x, mod, gmix, win, wkx, wkpt, qng, kvng, wuqn, wuqr, wuk, wuv.T, *rope]


def _inproj_out_shapes(B, L, q_transposed):
    q_shape = (B, MLA_HEADS, QK_W, L) if q_transposed else (B, MLA_HEADS, L, QK_W)
    return [(q_shape, BF16), ((B, MLA_HEADS, L, QK_W), BF16), ((B, MLA_HEADS * MLA_V, L), BF16),
            ((B, L, KV_LORA), F32), ((B, MLA_ROPE, L), F32)]


def _inproj(x, mod, gmix, wts, rope, tm):
    B, L, D = x.shape
    args = _inproj_args(x, mod, gmix, wts, rope, False)
    row = lambda w: pl.BlockSpec((1, tm, w), lambda b, i: (b, i, 0))
    col = lambda h: pl.BlockSpec((1, h, tm), lambda b, i: (b, 0, i))
    in_specs = [row(D), pl.BlockSpec((1, 6, D), lambda b, i: (b, 0, 0))]
    in_specs += [_const_spec(w.shape) for w in args[2:12]]
    in_specs += [pl.BlockSpec((tm, 128), lambda b, i: (i, 0))] * 2
    in_specs += [pl.BlockSpec((ROPE_HALF, tm), lambda b, i: (0, i))] * 2
    shapes = [((B, L, 4 * HG_W), F32)] + _inproj_out_shapes(B, L, False)
    heads = pl.BlockSpec((1, MLA_HEADS, tm, QK_W), lambda b, i: (b, 0, i, 0))
    out_specs = [row(4 * HG_W), heads, heads, col(MLA_HEADS * MLA_V),
                 row(KV_LORA), col(MLA_ROPE)]
    return pl.pallas_call(
        _inproj_body,
        out_shape=[jax.ShapeDtypeStruct(s, dt) for s, dt in shapes],
        grid=(B, L // tm),
        in_specs=in_specs,
        out_specs=out_specs,
        compiler_params=pltpu.CompilerParams(dimension_semantics=("arbitrary", "arbitrary"),
                                             vmem_limit_bytes=VMEM_LIMIT),
        name="in_proj",
    )(*args)


def _inproj_hgrn(x, mod, gmix, wts, rope, lb_logits, gain, s0, tm):
    B, L, D = x.shape
    nt = L // tm
    n = B * nt
    args = _inproj_args(x, mod, gmix, wts, rope, True)
    tin, lv = _unit_tables()
    perm = _unit_perm(UNIT)
    args += [perm, lb_logits, gain, jnp.swapaxes(s0, 2, 3), tin, lv, perm.T]
    cur = lambda j: jnp.minimum(j, n - 1)
    lag = lambda j: jnp.maximum(j - 1, 0)
    row = lambda w, t: pl.BlockSpec((1, tm, w), lambda j: (t(j) // nt, t(j) % nt, 0))
    col = lambda h: pl.BlockSpec((1, h, tm), lambda j: (cur(j) // nt, 0, cur(j) % nt))
    state = pl.BlockSpec((1, HG_HEADS, HG_DV, HG_DK), lambda j: (lag(j) // nt, 0, 0, 0))
    in_specs = [row(D, cur), pl.BlockSpec((1, 6, D), lambda j: (cur(j) // nt, 0, 0))]
    in_specs += [_const_spec(w.shape) for w in args[2:12]]
    in_specs += [pl.BlockSpec((tm, 128), lambda j: (cur(j) % nt, 0))] * 2
    in_specs += [pl.BlockSpec((ROPE_HALF, tm), lambda j: (0, cur(j) % nt))] * 2
    in_specs += [_const_spec(perm.shape), _const_spec(lb_logits.shape), _const_spec(gain.shape), state,
                 _const_spec(tin.shape), _const_spec(lv.shape), _const_spec(perm.shape)]
    shapes = [((B, L, HG_W), BF16), ((B, HG_HEADS, HG_DV, HG_DK), F32)] + _inproj_out_shapes(B, L, True)
    heads = pl.BlockSpec((1, MLA_HEADS, tm, QK_W), lambda j: (cur(j) // nt, 0, cur(j) % nt, 0))
    heads_t = pl.BlockSpec((1, MLA_HEADS, QK_W, tm), lambda j: (cur(j) // nt, 0, 0, cur(j) % nt))
    out_specs = [row(HG_W, lag), state, heads_t, heads,
                 col(MLA_HEADS * MLA_V), row(KV_LORA, cur), col(MLA_ROPE)]
    ohg, st, *mla = pl.pallas_call(
        functools.partial(_inproj_hgrn_body, nt=nt),
        out_shape=[jax.ShapeDtypeStruct(s, dt) for s, dt in shapes],
        grid=(n + 1,),
        in_specs=in_specs,
        out_specs=out_specs,
        scratch_shapes=[pltpu.VMEM((tm, 4 * HG_W), F32), pltpu.VMEM((HG_HEADS, HG_DV, HG_DK), F32)],
        compiler_params=pltpu.CompilerParams(dimension_semantics=("arbitrary",),
                                             vmem_limit_bytes=VMEM_LIMIT),
        name="in_proj_hgrn",
    )(*args)
    return (ohg, jnp.swapaxes(st, 2, 3), *mla)


def _hgrn_body(raw_ref, lbl_ref, gain_ref, s0_ref, tin_ref, tup_ref, ones_ref,
               o_out, s_out,
               st_scr, b_scr, q_scr, kk_scr, v_scr, eb_scr, qt_scr, kt_scr, vb_scr, p_scr, o_scr, *, th):
    i = pl.program_id(1)

    @pl.when(i == 0)
    def _():
        for h in range(HG_HEADS):
            st_scr[h] = s0_ref[0, h].T

    lbl = lbl_ref[...]
    e = jnp.exp(lbl - jnp.max(lbl, axis=0, keepdims=True))
    lb = (e / jnp.sum(e, axis=0, keepdims=True))[0:1]

    q = raw_ref[0, :, 0:HG_W] * HG_DK ** -0.5
    f = lb + (1.0 - lb) * jax.nn.sigmoid(raw_ref[0, :, HG_W:2 * HG_W])
    kk = 1.0 - f
    v = raw_ref[0, :, 2 * HG_W:3 * HG_W]
    l1, l2, l3 = _split3(jnp.log(f))
    tin = tin_ref[...]
    tup = tup_ref[...]
    b = _dot(tin, l1) + _dot(tin, l2) + _dot(tin, l3)
    c = _dot(tup, l1) + _dot(tup, l2) + _dot(tup, l3)
    eb = jnp.exp(b)
    b_scr[...] = b
    q_scr[...] = q
    kk_scr[...] = kk
    v_scr[...] = v
    eb_scr[...] = eb
    qt_scr[...] = (q * eb).astype(BF16)
    kt_scr[...] = (kk * jnp.exp(c)).astype(BF16)
    vb_scr[...] = v.astype(BF16)

    rowid = lax.broadcasted_iota(jnp.int32, (SUB, HG_W), 0)

    def step(j, carry):
        r0 = pl.multiple_of(j * SUB, SUB)
        bj = b_scr[pl.ds(r0, SUB), :]
        qj = q_scr[pl.ds(r0, SUB), :]
        for s in range(SUB):
            bs = b_scr[pl.ds(r0 + s, 1), :]
            ks = kk_scr[pl.ds(r0 + s, 1), :]
            dec = jnp.exp(jnp.where(rowid >= s, bj - bs, -jnp.inf))
            p_scr[s * SUB:(s + 1) * SUB, :] = (qj * dec * ks).astype(BF16)
        r = _dot(p_scr[...], ones_ref[...])
        od = jnp.zeros((SUB, HG_W), F32)
        for s in range(SUB):
            od = od + r[s * SUB:(s + 1) * SUB, :] * v_scr[pl.ds(r0 + s, 1), :]
        qt = qt_scr[pl.ds(r0, SUB), :]
        kt = kt_scr[pl.ds(r0, SUB), :]
        vb = vb_scr[pl.ds(r0, SUB), :]
        dj = eb_scr[pl.ds(r0 + SUB - 1, 1), :]
        for h in range(HG_HEADS):
            hs = slice(h * HG_DK, (h + 1) * HG_DK)
            st = st_scr[h]
            oi = _dot_nt(qt[:, hs], st.astype(BF16))
            st_scr[h] = st * dj[:, hs] + _dot_tn(vb[:, hs], kt[:, hs])
            o_scr[pl.ds(r0, SUB), hs] = oi + od[:, hs]
        return carry

    lax.fori_loop(0, th // SUB, step, 0)

    gate = raw_ref[0, :, 3 * HG_W:4 * HG_W]
    gate = gate * jax.nn.sigmoid(gate)
    gain = gain_ref[...]
    for h in range(HG_HEADS):
        hs = slice(h * HG_DV, (h + 1) * HG_DV)
        o_out[0, :, hs] = (_rms(o_scr[:, hs], gain) * gate[:, hs]).astype(BF16)

    @pl.when(i == pl.num_programs(1) - 1)
    def _():
        for h in range(HG_HEADS):
            s_out[0, h] = st_scr[h].T


def _step_matrices(th):
    r = np.arange(th)
    same = (r[:, None] // SUB) == (r[None, :] // SUB)
    tin = (same & (r[None, :] <= r[:, None])).astype(np.float32)
    tup = (same & (r[None, :] > r[:, None])).astype(np.float32)
    hd = np.arange(HG_W) // HG_DK
    ones = (hd[:, None] == hd[None, :]).astype(np.float32)
    return jnp.asarray(tin, BF16), jnp.asarray(tup, BF16), jnp.asarray(ones, BF16)


def _hgrn(raw, lb_logits, gain, s0, th):
    B, L, _ = raw.shape
    tin, tup, ones = _step_matrices(th)
    f32s = lambda: pltpu.VMEM((th, HG_W), F32)
    b16s = lambda: pltpu.VMEM((th, HG_W), BF16)
    return pl.pallas_call(
        functools.partial(_hgrn_body, th=th),
        out_shape=[jax.ShapeDtypeStruct((B, L, HG_W), BF16),
                   jax.ShapeDtypeStruct((B, HG_HEADS, HG_DK, HG_DV), F32)],
        grid=(B, L // th),
        in_specs=[pl.BlockSpec((1, th, 4 * HG_W), lambda b, i: (b, i, 0)),
                  _const_spec(lb_logits.shape), _const_spec(gain.shape),
                  pl.BlockSpec((1, HG_HEADS, HG_DK, HG_DV), lambda b, i: (b, 0, 0, 0)),
                  _const_spec(tin.shape), _const_spec(tup.shape), _const_spec(ones.shape)],
        out_specs=[pl.BlockSpec((1, th, HG_W), lambda b, i: (b, i, 0)),
                   pl.BlockSpec((1, HG_HEADS, HG_DK, HG_DV), lambda b, i: (b, 0, 0, 0))],
        scratch_shapes=[pltpu.VMEM((HG_HEADS, HG_DV, HG_DK), F32),
                        f32s(), f32s(), f32s(), f32s(), f32s(),
                        b16s(), b16s(), b16s(),
                        pltpu.VMEM((SUB * SUB, HG_W), BF16), f32s()],
        compiler_params=pltpu.CompilerParams(dimension_semantics=("arbitrary", "arbitrary"),
                                             vmem_limit_bytes=VMEM_LIMIT),
        name="hgrn_scan",
    )(raw, lb_logits, gain, s0, tin, tup, ones)


def _unit_time():
    i = np.arange(UNIT)
    return SLABS * (i % SLAB_ROWS) + i // SLAB_ROWS


def _unit_tables():
    t = _unit_time()
    tin = (t[None, :] <= t[:, None]).astype(np.float32)
    rows = []
    for c in (1, 2, 4, 8):
        g = np.arange(SLAB_ROWS)
        t_bnd = SLABS * (2 * c * (g // (2 * c)) + c - 1) + SLABS - 1
        rows.append((t[None, :] <= t_bnd[:, None]).astype(np.float32))
    tin_ext = np.concatenate([tin] + rows, axis=0)
    x = t[:, None] ^ t[None, :]
    lv = np.where(x > 0, np.floor(np.log2(np.maximum(x, 1))), DIAG_LEVEL).astype(np.int32)
    lv = np.where(t[None, :] <= t[:, None], lv, -1)
    return jnp.asarray(tin_ext, BF16), jnp.asarray(np.concatenate([lv, lv], axis=1), jnp.int32)


def _pair_blockdiag(a, b):
    z = jnp.zeros_like(a)
    return jnp.concatenate([jnp.concatenate([a, z], axis=1), jnp.concatenate([z, b], axis=1)], axis=0)


def _hgrn_units_math(field, nu, st, lbl_ref, gain_ref, tin_ref, lv_ref, unperm_ref, o_out, after_level):
    lbl = lbl_ref[...]
    e = jnp.exp(lbl - jnp.max(lbl, axis=0, keepdims=True))
    lb = (e / jnp.sum(e, axis=0, keepdims=True))[0:1]
    tin = tin_ref[...]
    lv = lv_ref[...]
    gain = gain_ref[...]

    units = range(nu)
    pairs = range(HG_HEADS // 2)

    def pair_cols(x, hp):
        c0 = 2 * HG_DK * hp
        return x[:, c0:c0 + HG_DK], x[:, c0 + HG_DK:c0 + 2 * HG_DK]

    gates = [field(u, 3) for u in units]
    q = [field(u, 0) * HG_DK ** -0.5 for u in units]
    f = [lb + (1.0 - lb) * jax.nn.sigmoid(field(u, 1)) for u in units]
    kk = [1.0 - f[u] for u in units]
    lsplit = [_split2(jnp.log(f[u]) * np.float32(1.0 / np.log(2.0))) for u in units]
    bx = [_dot(tin, lsplit[u][0]) + _dot(tin, lsplit[u][1]) for u in units]
    b = [bx[u][0:UNIT] for u in units]
    bs = [[b[u][SLAB_ROWS * p:SLAB_ROWS * (p + 1)] for p in range(SLABS)] for u in units]
    vb = [field(u, 2).astype(BF16) for u in units]
    qb = [q[u].astype(BF16) for u in units]
    kkb = [kk[u].astype(BF16) for u in units]

    def slab_decay(u, level, p):
        if level < 3:
            half = 1 << level
            ref = p - p % (2 * half) + half - 1
            if p == ref:
                return jnp.ones_like(bs[u][p])
            return jnp.exp2(bs[u][p] - bs[u][ref] if p > ref else bs[u][ref] - bs[u][p])
        r = bx[u][UNIT + SLAB_ROWS * (level - 3):UNIT + SLAB_ROWS * (level - 2)]
        if level == N_LEVELS - 1:
            h = SLAB_ROWS // 2
            return jnp.concatenate([jnp.exp2(r[:h] - bs[u][p][:h]), jnp.exp2(bs[u][p][h:] - r[h:])], axis=0)
        return jnp.exp2(-jnp.abs(bs[u][p] - r))

    acc = [[jnp.zeros((UNIT, 2 * HG_DK), F32) for _ in pairs] for _ in units]
    for level in range(N_LEVELS + 1):
        here = lv == level
        for u in units:
            if level == DIAG_LEVEL:
                qt, kt = qb[u], kkb[u]
            else:
                dec = jnp.concatenate([slab_decay(u, level, p) for p in range(SLABS)], axis=0).astype(BF16)
                qt, kt = qb[u] * dec, kkb[u] * dec
            for hp in pairs:
                a2 = _dot_nt(qt[:, 2 * HG_DK * hp:2 * HG_DK * (hp + 1)], _pair_blockdiag(*pair_cols(kt, hp)))
                acc[u][hp] = jnp.where(here, a2, acc[u][hp])
        after_level(level)

    b_end = [b[u][UNIT - 1:UNIT] for u in units]
    qi = [(q[u] * jnp.exp2(b[u])).astype(BF16) for u in units]
    ks = [(kk[u] * jnp.exp2(b_end[u] - b[u])).astype(BF16) for u in units]
    d = [jnp.exp2(b_end[u]) for u in units]
    o_intra = [[_dot(acc[u][hp].astype(BF16), _pair_blockdiag(*pair_cols(vb[u], hp))) for hp in pairs]
               for u in units]
    ds = [[_dot_tn(vb[u][:, h * HG_DV:(h + 1) * HG_DV], ks[u][:, h * HG_DK:(h + 1) * HG_DK])
           for h in range(HG_HEADS)] for u in units]
    o_inter = []
    for u in units:
        o_inter.append([_dot_nt(qi[u][:, 2 * HG_DK * hp:2 * HG_DK * (hp + 1)],
                                _pair_blockdiag(st[2 * hp].astype(BF16), st[2 * hp + 1].astype(BF16)))
                        for hp in pairs])
        st = [st[h] * d[u][:, h * HG_DK:(h + 1) * HG_DK] + ds[u][h] for h in range(HG_HEADS)]
    res = []
    for u in units:
        o = jnp.concatenate([o_intra[u][hp] + o_inter[u][hp] for hp in pairs], axis=1)
        gate = gates[u] * jax.nn.sigmoid(gates[u])
        r = jnp.concatenate([_rms(o[:, h * HG_DV:(h + 1) * HG_DV], gain) for h in range(HG_HEADS)], axis=1)
        res.append((r * gate).astype(BF16))
    for u in units:
        o_out[0, UNIT * u:UNIT * (u + 1), :] = _dot(unperm_ref[...], res[u]).astype(BF16)
    return st


def _attn_body(qt_ref, k_ref, vt_ref, o_ref, acc_scr, *, tq, nq):
    for qi in range(nq):
        qt = qt_ref[0, 0, :, qi * tq:(qi + 1) * tq]
        nblk = qi + 1
        scores = lambda i: _dot(k_ref[0, 0, i * tq:(i + 1) * tq, :], qt)
        pending = [scores(i) for i in range(min(ATTN_LOOKAHEAD, nblk))]
        m = jnp.full((1, tq), -jnp.inf, F32)
        l = jnp.zeros((1, tq), F32)
        for ki in range(nblk):
            st = pending.pop(0)
            if ki + ATTN_LOOKAHEAD < nblk:
                pending.append(scores(ki + ATTN_LOOKAHEAD))
            if ki == nblk - 1:
                kc = lax.broadcasted_iota(jnp.int32, st.shape, 0) // CHUNK
                qc = lax.broadcasted_iota(jnp.int32, st.shape, 1) // CHUNK
                st = jnp.where(kc <= qc, st, -jnp.inf)
            m_new = jnp.maximum(m, jnp.max(st, axis=0, keepdims=True))
            p = jnp.exp2(st - m_new)
            alpha = jnp.exp2(m - m_new)
            l = alpha * l + jnp.sum(p, axis=0, keepdims=True)
            pv = _dot(vt_ref[0, :, ki * tq:(ki + 1) * tq], p.astype(BF16))
            if ki == 0:
                acc_scr[qi % 2] = pv
            else:
                acc_scr[qi % 2] = alpha * acc_scr[qi % 2] + pv
            m = m_new
        o_ref[0, 0, qi * tq:(qi + 1) * tq, :] = (acc_scr[qi % 2] / l).T.astype(BF16)


def _attn(qt, k, vt, tq):
    B, _, L, _ = k.shape
    per_head = lambda w: pl.BlockSpec((1, 1, L, w), lambda b, h: (b, h, 0, 0))
    return pl.pallas_call(
        functools.partial(_attn_body, tq=tq, nq=L // tq),
        out_shape=jax.ShapeDtypeStruct((B, MLA_HEADS, L, MLA_V), BF16),
        grid=(B, MLA_HEADS),
        in_specs=[pl.BlockSpec((1, 1, QK_W, L), lambda b, h: (b, h, 0, 0)), per_head(QK_W),
                  pl.BlockSpec((1, MLA_V, L), lambda b, h: (b, h, 0))],
        out_specs=per_head(MLA_V),
        scratch_shapes=[pltpu.VMEM((2, MLA_V, tq), F32)],
        compiler_params=pltpu.CompilerParams(dimension_semantics=("arbitrary",) * 2,
                                             vmem_limit_bytes=VMEM_LIMIT),
        name="mla_attn_prompt",
    )(qt, k, vt)


def _attn_cache_body(q_ref, kn_ref, vn_ref, clat_ref, ckpe_ref, wuk_ref, wuv_ref, place_ref, o_ref):
    n = q_ref.shape[2]
    latc = clat_ref[0].astype(BF16)
    kpec = ckpe_ref[0].astype(BF16)
    qlat, qrope, s_new = [], [], []
    for h in range(MLA_HEADS):
        qh = q_ref[0, h]
        ws = slice(MLA_NOPE * h, MLA_NOPE * (h + 1))
        qlat.append(_dot_nt(qh[:, 0:MLA_NOPE], wuk_ref[:, ws]).astype(BF16))
        qrope.append(_dot_nt(qh[:, MLA_NOPE:QK_W], place_ref[:, ws]).astype(BF16))
        s_new.append(_dot_nt(qh, kn_ref[0, h]))
    s_new = jnp.concatenate(s_new, axis=0)
    s_c = (_dot_nt(jnp.concatenate(qlat, axis=0), latc)
           + _dot(jnp.concatenate(qrope, axis=0), kpec))
    m = jnp.maximum(jnp.max(s_c, axis=-1, keepdims=True), jnp.max(s_new, axis=-1, keepdims=True))
    p_c = jnp.exp2(s_c - m)
    p_new = jnp.exp2(s_new - m)
    l = jnp.sum(p_c, axis=-1, keepdims=True) + jnp.sum(p_new, axis=-1, keepdims=True)
    o_lat = _dot(p_c.astype(BF16), latc).astype(BF16)
    p_new = p_new.astype(BF16)
    for h in range(MLA_HEADS):
        ws = slice(MLA_V * h, MLA_V * (h + 1))
        rs = slice(n * h, n * (h + 1))
        acc = _dot(o_lat[rs], wuv_ref[:, ws]) + _dot_nt(p_new[rs], vn_ref[0, ws, :])
        o_ref[0, h] = (acc / l[rs]).astype(BF16)


def _rope_place():
    p = np.zeros((MLA_ROPE, MLA_HEADS * MLA_NOPE), np.float32)
    for h in range(MLA_HEADS):
        for i in range(ROPE_HALF):
            p[i, MLA_NOPE * h + ROPE_HALF * h + i] = 1.0
            p[ROPE_HALF + i, MLA_NOPE * h + ROPE_HALF * ((h + 1) % MLA_HEADS) + i] = 1.0
    return jnp.asarray(p, BF16)


def _attn_cache(q, kn, vn, clat, ckpe, wuk, wuv):
    B, _, L, _ = q.shape
    P = clat.shape[1]
    place = _rope_place()
    bspec = lambda n, w: pl.BlockSpec((1, n, w), lambda b: (b, 0, 0))
    heads = lambda w: pl.BlockSpec((1, MLA_HEADS, L, w), lambda b: (b, 0, 0, 0))
    return pl.pallas_call(
        _attn_cache_body,
        out_shape=jax.ShapeDtypeStruct((B, MLA_HEADS, L, MLA_V), BF16),
        grid=(B,),
        in_specs=[heads(QK_W), heads(QK_W), bspec(MLA_HEADS * MLA_V, L),
                  bspec(P, KV_LORA), bspec(MLA_ROPE, P),
                  _const_spec(wuk.shape), _const_spec(wuv.shape), _const_spec(place.shape)],
        out_specs=heads(MLA_V),
        compiler_params=pltpu.CompilerParams(dimension_semantics=("arbitrary",),
                                             vmem_limit_bytes=VMEM_LIMIT),
        name="mla_attn_cache",
    )(q, kn, vn, clat, ckpe, wuk, wuv, place)


def _ffn_body(x_ref, ohg_ref, omla_ref, mod_ref, conv0_ref, wout_ref, gffn_ref, wup_ref, cw_ref, cb_ref,
              wdn_ref, gfin_ref, y_ref, conv_out, carry_scr, a_scr, u_scr, *, tm, ft):
    @pl.when(pl.program_id(1) == 0)
    def _():
        carry_scr[...] = conv0_ref[0]

    mod = mod_ref[0]
    g1 = mod[2:3]
    sh2 = mod[3:4]
    sc2 = mod[4:5]
    g2 = mod[5:6]
    omla = jnp.concatenate([omla_ref[0, h] for h in range(MLA_HEADS)], axis=1)
    o = _dot(ohg_ref[0], wout_ref[0:HG_W, :]) + _dot(omla, wout_ref[HG_W:2 * HG_W, :])
    x1 = x_ref[0] + g1 * o
    h2 = (_rms(x1, gffn_ref[...]) * (1.0 + sc2) + sh2).astype(BF16)
    for j in range(D_FF // ft):
        cs = slice(j * ft, (j + 1) * ft)
        a = _dot(h2, wup_ref[:, cs])
        v = _dot(h2, wup_ref[:, D_FF + j * ft:D_FF + (j + 1) * ft])
        a_scr[0:CARRY_ROWS, :] = carry_scr[:, cs]
        a_scr[CARRY_ROWS:CARRY_ROWS + tm, :] = a
        carry_scr[:, cs] = a[tm - CARRY_ROWS:tm]
        cw = cw_ref[:, cs]
        conv = (cb_ref[:, cs] + cw[0:1] * a_scr[CARRY_ROWS - 2:CARRY_ROWS - 2 + tm, :]
                + cw[1:2] * a_scr[CARRY_ROWS - 1:CARRY_ROWS - 1 + tm, :] + cw[2:3] * a)
        g = 0.5 * conv * (1.0 + lax.erf(conv * np.float32(np.sqrt(0.5))))
        u_scr[:, cs] = (g * v).astype(BF16)
        conv_out[0, :, cs] = a[tm - (CONV_W - 1):tm]

    x2 = x1 + g2 * _dot(u_scr[...], wdn_ref[...])
    y_ref[0] = _rms(x2, gfin_ref[...])


def _ffn(x, ohg, omla, mod, conv0, wout, gffn, wup, cw, cb, wdn, gfin, tm, ft):
    B, L, D = x.shape
    row = lambda w: pl.BlockSpec((1, tm, w), lambda b, i: (b, i, 0))
    once = lambda a: pl.BlockSpec(a.shape, lambda b, i: (0,) * a.ndim, pipeline_mode=pl.Buffered(1))
    return pl.pallas_call(
        functools.partial(_ffn_body, tm=tm, ft=ft),
        out_shape=[jax.ShapeDtypeStruct((B, L, D), F32),
                   jax.ShapeDtypeStruct((B, CONV_W - 1, D_FF), F32)],
        grid=(B, L // tm),
        in_specs=[row(D), row(HG_W), pl.BlockSpec((1, MLA_HEADS, tm, MLA_V), lambda b, i: (b, 0, i, 0)),
                  pl.BlockSpec((1, 6, D), lambda b, i: (b, 0, 0)),
                  pl.BlockSpec((1, CARRY_ROWS, D_FF), lambda b, i: (b, 0, 0)),
                  once(wout), once(gffn), once(wup), once(cw), once(cb), once(wdn), once(gfin)],
        out_specs=[row(D), pl.BlockSpec((1, CONV_W - 1, D_FF), lambda b, i: (b, 0, 0))],
        scratch_shapes=[pltpu.VMEM((CARRY_ROWS, D_FF), F32), pltpu.VMEM((CARRY_ROWS + tm, ft), F32),
                        pltpu.VMEM((tm, D_FF), BF16)],
        compiler_params=pltpu.CompilerParams(dimension_semantics=("arbitrary", "arbitrary"),
                                             vmem_limit_bytes=VMEM_LIMIT),
        name="out_ffn",
    )(x, ohg, omla, mod, conv0, wout, gffn, wup, cw, cb, wdn, gfin)


def _prep_weights(w_in, mla_q_norm_gain, mla_kv_norm_gain, w_uq, w_uk, w_uv):
    win = w_in.astype(BF16)
    kp = win[:, 4 * HG_W + Q_LORA + KV_LORA:]
    wkx = jnp.concatenate([jnp.tile(kp[:, :ROPE_HALF], (1, MLA_HEADS)),
                           jnp.tile(kp[:, ROPE_HALF:], (1, MLA_HEADS))], axis=1)
    uq = w_uq.reshape(Q_LORA, MLA_HEADS, MLA_NOPE + MLA_ROPE)
    wuqn = uq[:, :, :MLA_NOPE].reshape(Q_LORA, MLA_HEADS * MLA_NOPE).astype(BF16)
    x1a = uq[:, :, MLA_NOPE:MLA_NOPE + ROPE_HALF].reshape(Q_LORA, MLA_HEADS * ROPE_HALF)
    x2a = uq[:, :, MLA_NOPE + ROPE_HALF:].reshape(Q_LORA, MLA_HEADS * ROPE_HALF)
    wuqr = jnp.concatenate([x1a, x2a, jnp.roll(x1a, ROPE_HALF, axis=1), jnp.roll(x2a, ROPE_HALF, axis=1)],
                           axis=1).astype(BF16)
    wuk = w_uk.reshape(KV_LORA, MLA_HEADS * MLA_NOPE).astype(BF16)
    wuv = w_uv.reshape(KV_LORA, MLA_HEADS * MLA_V).astype(BF16)
    return (win, wkx, kp.T, mla_q_norm_gain.reshape(1, -1), mla_kv_norm_gain.reshape(1, -1), wuqn, wuqr, wuk, wuv)


def _rope_tables(pos):
    inv_freq = ROPE_THETA ** (-jnp.arange(ROPE_HALF, dtype=F32) / ROPE_HALF)
    ang = pos.astype(F32)[:, None] * inv_freq[None, :]
    cos, sin = jnp.cos(ang), jnp.sin(ang)
    return jnp.tile(cos, (1, MLA_HEADS)), jnp.tile(sin, (1, MLA_HEADS)), cos.T, sin.T


def _tile(n, pref):
    return pref if n % pref == 0 else n


def kernel(x_prompt, x_sample, c_prompt, c_sample, cache_kv_latent, cache_k_rope, state_hgrn, state_ffn_conv, w_ada, b_ada, norm_mix_gain, w_in, hg_lb_logits, hg_norm_gain, mla_q_norm_gain, mla_kv_norm_gain, w_uq, w_uk, w_uv, w_out, norm_ffn_gain, w_up, conv_w, conv_b, w_down, final_norm_gain):
    assert w_ada.shape[0] == 1, "single-layer trunk"
    B, L, D = x_prompt.shape
    Bs, Ls, _ = x_sample.shape
    past = cache_kv_latent.shape[2]

    mod = _ada(jnp.concatenate([c_prompt, c_sample], axis=0), w_ada[0], b_ada)
    mod = mod.reshape(B + Bs, 6, D)
    wts = _prep_weights(w_in[0], mla_q_norm_gain[0], mla_kv_norm_gain[0], w_uq[0], w_uk[0], w_uv[0])
    gmix = norm_mix_gain
    wout = w_out[0].astype(BF16)
    wup = w_up[0].astype(BF16)
    wdn = w_down[0].astype(BF16)
    gfin = final_norm_gain.reshape(1, D)
    ffn_w = (wout, norm_ffn_gain, wup, conv_w[0], conv_b, wdn, gfin)

    def layer(x, mod_x, pos, s0, conv0, cache):
        n, l, _ = x.shape
        rope = _rope_tables(pos)
        tm = _tile(l, 512)
        if tm % UNIT == 0:
            ohg, s_new, q, k, v, lat, kpe = _inproj_hgrn(x, mod_x, gmix, wts, rope, hg_lb_logits, hg_norm_gain,
                                                         s0, tm)
        else:
            raw, q, k, v, lat, kpe = _inproj(x, mod_x, gmix, wts, rope, tm)
            ohg, s_new = _hgrn(raw, hg_lb_logits, hg_norm_gain, s0, _tile(l, 256))
        if cache is None:
            omla = _attn(q, k, v, _tile(l, 512))
        else:
            omla = _attn_cache(q, k, v, cache[0], cache[1], wts[7], wts[8])
        y, conv_new = _ffn(x, ohg, omla, mod_x, conv0, *ffn_w, _tile(l, 512), FF_TILE)
        return y, lat[None], jnp.swapaxes(kpe, 1, 2)[None], s_new[None], conv_new[None]

    zeros_state = jnp.zeros((B, HG_HEADS, HG_DK, HG_DV), F32)
    zeros_conv = jnp.zeros((B, CARRY_ROWS, D_FF), F32)
    conv0_s = jnp.pad(state_ffn_conv[0], ((0, 0), (CARRY_ROWS - (CONV_W - 1), 0), (0, 0)))
    yp, latp, kpep, hgp, cvp = layer(x_prompt, mod[:B], jnp.arange(L), zeros_state, zeros_conv, None)
    ys, lats, kpes, hgs, cvs = layer(x_sample, mod[B:], past + jnp.arange(Ls), state_hgrn[0], conv0_s,
                                     (cache_kv_latent[0], jnp.swapaxes(cache_k_rope[0], 1, 2)))
    return (yp, ys, latp, kpep, hgp, cvp, lats, kpes, hgs, cvs)
```

```python
import functools

import numpy as np
import jax
import jax.numpy as jnp
from jax import lax
from jax.experimental import pallas as pl
from jax.experimental.pallas import tpu as pltpu

F32 = jnp.float32
BF16 = jnp.bfloat16

D_MODEL = 1024
CHUNK = 64
HG_HEADS = 4
HG_DK = 128
HG_DV = 128
HG_W = HG_HEADS * HG_DK
MLA_HEADS = 4
MLA_NOPE = 128
MLA_ROPE = 64
ROPE_HALF = MLA_ROPE // 2
MLA_V = 128
Q_LORA = 384
KV_LORA = 256
ROPE_THETA = 10000.0
MLA_SCALE = (MLA_NOPE + MLA_ROPE) ** -0.5
Q_SCALE = MLA_SCALE * float(np.log2(np.e))
QK_W = 2 * MLA_NOPE
D_FF = 2816
CONV_W = 3
EPS = 1e-6

SUB = 16
FF_TILE = D_FF
ATTN_LOOKAHEAD = 4
CARRY_ROWS = 8
VMEM_LIMIT = 56 * 1024 * 1024
UNIT = 128
SLABS = 8
SLAB_ROWS = UNIT // SLABS
N_LEVELS = 7
DIAG_LEVEL = N_LEVELS


def _dot(a, b):
    return jnp.dot(a, b, preferred_element_type=F32)


def _dot_nt(a, b):
    return lax.dot_general(a, b, (((1,), (1,)), ((), ())), preferred_element_type=F32)


def _dot_tn(a, b):
    return lax.dot_general(a, b, (((0,), (0,)), ((), ())), preferred_element_type=F32)


def _split3(x):
    x1 = x.astype(BF16)
    r1 = x - x1.astype(F32)
    x2 = r1.astype(BF16)
    r2 = r1 - x2.astype(F32)
    return x1, x2, r2.astype(BF16)


def _split2(x):
    x1 = x.astype(BF16)
    return x1, (x - x1.astype(F32)).astype(BF16)


def _rms(x, g):
    return x * lax.rsqrt(jnp.mean(x * x, axis=-1, keepdims=True) + EPS) * g


def _const_spec(shape):
    n = len(shape)
    return pl.BlockSpec(shape, lambda *_: (0,) * n)


def _ada_body(c_ref, w_ref, b_ref, o_ref):
    c = c_ref[...]
    s = c * jax.nn.sigmoid(c)
    s1, s2, _ = _split3(s)
    w = w_ref[...]
    w1 = w.astype(BF16)
    w2 = (w - w1.astype(F32)).astype(BF16)
    o_ref[...] = _dot(s1, w1) + _dot(s1, w2) + _dot(s2, w1) + b_ref[...]


def _ada(c, w_ada, b_ada):
    n, d = c.shape
    nout = w_ada.shape[1]
    tn = 1024
    return pl.pallas_call(
        _ada_body,
        out_shape=jax.ShapeDtypeStruct((n, nout), F32),
        grid=(nout // tn,),
        in_specs=[pl.BlockSpec((n, d), lambda j: (0, 0)),
                  pl.BlockSpec((d, tn), lambda j: (0, j)),
                  pl.BlockSpec((1, tn), lambda j: (0, j))],
        out_specs=pl.BlockSpec((n, tn), lambda j: (0, j)),
        compiler_params=pltpu.CompilerParams(dimension_semantics=("arbitrary",)),
        name="adaln_mod",
    )(c, w_ada, b_ada)


def _rope_slots(r1, r2, nope, out_ref, slot):
    for h in range(MLA_HEADS):
        xh = jnp.where(slot == h, r1, jnp.where(slot == (h + 1) % MLA_HEADS, r2, 0.0))
        out_ref[0, h, :, 0:MLA_NOPE] = nope[:, MLA_NOPE * h:MLA_NOPE * (h + 1)].astype(BF16)
        out_ref[0, h, :, MLA_NOPE:QK_W] = xh.astype(BF16)


def _mla_stages(hb, refs, outs, rope_refs, q_transposed):
    (wcq_ref, wckv_ref, wkx_ref, wkpt_ref, qng_ref, kvng_ref, wuqn_ref, wuqr_ref, wuk_ref, wuvt_ref) = refs
    q_out, k_out, v_out, lat_out, kpe_out = outs
    cos_ref, sin_ref, cost_ref, sint_ref = rope_refs
    t = {}

    def project():
        t["cq"] = _dot(hb, wcq_ref[...])
        t["ckv"] = _dot(hb, wckv_ref[...])
        t["kx"] = _dot(hb, wkx_ref[...])
        if not q_transposed:
            t["kxt"] = _dot_nt(wkpt_ref[...], hb)

    def q_up():
        cq = _rms(t["cq"], qng_ref[...])
        if q_transposed:
            cq = cq.T.astype(BF16)
            t["qn"] = _dot(wuqn_ref[...], cq) * Q_SCALE
            t["qr"] = _dot(wuqr_ref[...], cq) * Q_SCALE
        else:
            cq = cq.astype(BF16)
            t["qn"] = _dot(cq, wuqn_ref[...]) * Q_SCALE
            t["qr"] = _dot(cq, wuqr_ref[...]) * Q_SCALE

    def kv_up():
        lat = _rms(t["ckv"], kvng_ref[...])
        lat_out[0] = lat
        latb = lat.astype(BF16)
        t["kn"] = _dot(latb, wuk_ref[...])
        if q_transposed:
            v_out[0] = _dot(wuvt_ref[...], lat.T.astype(BF16)).astype(BF16)
        else:
            v_out[0] = _dot_nt(wuvt_ref[...], latb).astype(BF16)

    def slot_of(cos):
        return lax.broadcasted_iota(jnp.int32, cos.shape, 1) // ROPE_HALF

    def q_rope():
        qr, qn = t["qr"], t["qn"]
        if q_transposed:
            cos = jnp.concatenate([cost_ref[...]] * MLA_HEADS, axis=0)
            sin = jnp.concatenate([sint_ref[...]] * MLA_HEADS, axis=0)
            qr1 = qr[0:128] * cos - qr[128:256] * sin
            qr2 = qr[384:512] * cos + qr[256:384] * sin
            slot = lax.broadcasted_iota(jnp.int32, cos.shape, 0) // ROPE_HALF
            for h in range(MLA_HEADS):
                xh = jnp.where(slot == h, qr1, jnp.where(slot == (h + 1) % MLA_HEADS, qr2, 0.0))
                q_out[0, h, 0:MLA_NOPE, :] = qn[MLA_NOPE * h:MLA_NOPE * (h + 1)].astype(BF16)
                q_out[0, h, MLA_NOPE:QK_W, :] = xh.astype(BF16)
            return
        cos, sin = cos_ref[...], sin_ref[...]
        qr1 = qr[:, 0:128] * cos - qr[:, 128:256] * sin
        qr2 = qr[:, 384:512] * cos + qr[:, 256:384] * sin
        _rope_slots(qr1, qr2, qn, q_out, slot_of(cos))

    def k_rope():
        cos, sin, kx = cos_ref[...], sin_ref[...], t["kx"]
        kr1 = kx[:, 0:128] * cos - kx[:, 128:256] * sin
        kr2 = kx[:, 128:256] * cos + kx[:, 0:128] * sin
        _rope_slots(kr1, kr2, t["kn"], k_out, slot_of(cos))
        if q_transposed:
            lane = lax.broadcasted_iota(jnp.int32, cos.shape, 1)
            kpe_out[0] = jnp.where(lane < ROPE_HALF, kr1, kr2).T[0:MLA_ROPE]
            return
        kxt = t["kxt"]
        x1t, x2t = kxt[0:ROPE_HALF], kxt[ROPE_HALF:MLA_ROPE]
        cost, sint = cost_ref[...], sint_ref[...]
        kpe_out[0] = jnp.concatenate([x1t * cost - x2t * sint, x2t * cost + x1t * sint], axis=0)

    return [project, q_up, kv_up, q_rope, k_rope]


def _normed_input(x_ref, mod_ref, gmix_ref):
    mod = mod_ref[0]
    return (_rms(x_ref[0], gmix_ref[...]) * (1.0 + mod[1:2]) + mod[0:1]).astype(BF16)


def _win_views(win_ref):
    hgw = 4 * HG_W
    return (win_ref.at[:, 0:hgw], win_ref.at[:, hgw:hgw + Q_LORA],
            win_ref.at[:, hgw + Q_LORA:hgw + Q_LORA + KV_LORA])


def _inproj_body(x_ref, mod_ref, gmix_ref, win_ref, wkx_ref, wkpt_ref, qng_ref, kvng_ref,
                 wuqn_ref, wuqr_ref, wuk_ref, wuvt_ref, cos_ref, sin_ref, cost_ref, sint_ref,
                 hg_out, q_out, k_out, v_out, lat_out, kpe_out):
    whg_ref, wcq_ref, wckv_ref = _win_views(win_ref)
    hb = _normed_input(x_ref, mod_ref, gmix_ref)
    hg_out[0] = _dot(hb, whg_ref[...])
    for stage in _mla_stages(hb, (wcq_ref, wckv_ref, wkx_ref, wkpt_ref, qng_ref, kvng_ref, wuqn_ref, wuqr_ref,
                                  wuk_ref, wuvt_ref),
                             (q_out, k_out, v_out, lat_out, kpe_out), (cos_ref, sin_ref, cost_ref, sint_ref),
                             False):
        stage()


def _inproj_hgrn_body(x_ref, mod_ref, gmix_ref, win_ref, wkx_ref, wkpt_ref, qng_ref, kvng_ref,
                      wuqn_ref, wuqr_ref, wuk_ref, wuvt_ref, cos_ref, sin_ref, cost_ref, sint_ref,
                      perm_ref, lbl_ref, gain_ref, s0t_ref, tin_ref, lv_ref, unperm_ref,
                      hg_out, st_out, q_out, k_out, v_out, lat_out, kpe_out, raw_scr, st_scr, *, nt):
    j = pl.program_id(0)

    @pl.when(j == 0)
    def _():
        raw_scr[...] = jnp.zeros(raw_scr.shape, F32)

    first_of_batch = lax.rem(jnp.maximum(j - 1, 0), nt) == 0
    st = [jnp.where(first_of_batch, s0t_ref[0, hd], st_scr[hd]) for hd in range(HG_HEADS)]

    whg_ref, wcq_ref, wckv_ref = _win_views(win_ref)
    n_units = x_ref.shape[1] // UNIT
    t = {}

    def start():
        hb = _normed_input(x_ref, mod_ref, gmix_ref)
        t["hbp"] = jnp.concatenate([_dot(perm_ref[...], hb[UNIT * u:UNIT * (u + 1)]).astype(BF16)
                                    for u in range(n_units)], axis=0)
        t["mla"] = _mla_stages(hb, (wcq_ref, wckv_ref, wkx_ref, wkpt_ref, qng_ref, kvng_ref, wuqn_ref,
                                    wuqr_ref, wuk_ref, wuvt_ref),
                               (q_out, k_out, v_out, lat_out, kpe_out), (cos_ref, sin_ref, cost_ref, sint_ref),
                               True)

    def raw_group(k):
        def run():
            raw_scr[:, HG_W * k:HG_W * (k + 1)] = _dot(t["hbp"], whg_ref[:, HG_W * k:HG_W * (k + 1)])
        return run

    chunks = [start] + [raw_group(k) for k in range(4)] + [lambda n=n: t["mla"][n]() for n in range(5)]

    def next_chunk(_level):
        if chunks:
            chunks.pop(0)()

    st = _hgrn_units_math(lambda u, k: raw_scr[UNIT * u:UNIT * (u + 1), HG_W * k:HG_W * (k + 1)], n_units, st,
                          lbl_ref, gain_ref, tin_ref, lv_ref, unperm_ref, hg_out, next_chunk)
    while chunks:
        chunks.pop(0)()
    for hd in range(HG_HEADS):
        st_scr[hd] = st[hd]
        st_out[0, hd] = st[hd]


def _unit_perm(n):
    i = np.arange(n)
    src = UNIT * (i // UNIT) + _unit_time()[i % UNIT]
    return jnp.asarray(src[:, None] == i[None, :], BF16)


def _inproj_args(x, mod, gmix, wts, rope, q_transposed):
    (win, wkx, wkpt, qng, kvng, wuqn, wuqr, wuk, wuv) = wts
    if q_transposed:
        wuqn, wuqr = wuqn.T, wuqr.T
    return [x, mod, gmix, win, wkx, wkpt, qng, kvng, wuqn, wuqr, wuk, wuv.T, *rope]


def _inproj_out_shapes(B, L, q_transposed):
    q_shape = (B, MLA_HEADS, QK_W, L) if q_transposed else (B, MLA_HEADS, L, QK_W)
    return [(q_shape, BF16), ((B, MLA_HEADS, L, QK_W), BF16), ((B, MLA_HEADS * MLA_V, L), BF16),
            ((B, L, KV_LORA), F32), ((B, MLA_ROPE, L), F32)]


def _inproj(x, mod, gmix, wts, rope, tm):
    B, L, D = x.shape
    args = _inproj_args(x, mod, gmix, wts, rope, False)
    row = lambda w: pl.BlockSpec((1, tm, w), lambda b, i: (b, i, 0))
    col = lambda h: pl.BlockSpec((1, h, tm), lambda b, i: (b, 0, i))
    in_specs = [row(D), pl.BlockSpec((1, 6, D), lambda b, i: (b, 0, 0))]
    in_specs += [_const_spec(w.shape) for w in args[2:12]]
    in_specs += [pl.BlockSpec((tm, 128), lambda b, i: (i, 0))] * 2
    in_specs += [pl.BlockSpec((ROPE_HALF, tm), lambda b, i: (0, i))] * 2
    shapes = [((B, L, 4 * HG_W), F32)] + _inproj_out_shapes(B, L, False)
    heads = pl.BlockSpec((1, MLA_HEADS, tm, QK_W), lambda b, i: (b, 0, i, 0))
    out_specs = [row(4 * HG_W), heads, heads, col(MLA_HEADS * MLA_V),
                 row(KV_LORA), col(MLA_ROPE)]
    return pl.pallas_call(
        _inproj_body,
        out_shape=[jax.ShapeDtypeStruct(s, dt) for s, dt in shapes],
        grid=(B, L // tm),
        in_specs=in_specs,
        out_specs=out_specs,
        compiler_params=pltpu.CompilerParams(dimension_semantics=("arbitrary", "arbitrary"),
                                             vmem_limit_bytes=VMEM_LIMIT),
        name="in_proj",
    )(*args)


def _inproj_hgrn(x, mod, gmix, wts, rope, lb_logits, gain, s0, tm):
    B, L, D = x.shape
    nt = L // tm
    n = B * nt
    args = _inproj_args(x, mod, gmix, wts, rope, True)
    tin, lv = _unit_tables()
    perm = _unit_perm(UNIT)
    args += [perm, lb_logits, gain, jnp.swapaxes(s0, 2, 3), tin, lv, perm.T]
    cur = lambda j: jnp.minimum(j, n - 1)
    lag = lambda j: jnp.maximum(j - 1, 0)
    row = lambda w, t: pl.BlockSpec((1, tm, w), lambda j: (t(j) // nt, t(j) % nt, 0))
    col = lambda h: pl.BlockSpec((1, h, tm), lambda j: (cur(j) // nt, 0, cur(j) % nt))
    state = pl.BlockSpec((1, HG_HEADS, HG_DV, HG_DK), lambda j: (lag(j) // nt, 0, 0, 0))
    in_specs = [row(D, cur), pl.BlockSpec((1, 6, D), lambda j: (cur(j) // nt, 0, 0))]
    in_specs += [_const_spec(w.shape) for w in args[2:12]]
    in_specs += [pl.BlockSpec((tm, 128), lambda j: (cur(j) % nt, 0))] * 2
    in_specs += [pl.BlockSpec((ROPE_HALF, tm), lambda j: (0, cur(j) % nt))] * 2
    in_specs += [_const_spec(perm.shape), _const_spec(lb_logits.shape), _const_spec(gain.shape), state,
                 _const_spec(tin.shape), _const_spec(lv.shape), _const_spec(perm.shape)]
    shapes = [((B, L, HG_W), BF16), ((B, HG_HEADS, HG_DV, HG_DK), F32)] + _inproj_out_shapes(B, L, True)
    heads = pl.BlockSpec((1, MLA_HEADS, tm, QK_W), lambda j: (cur(j) // nt, 0, cur(j) % nt, 0))
    heads_t = pl.BlockSpec((1, MLA_HEADS, QK_W, tm), lambda j: (cur(j) // nt, 0, 0, cur(j) % nt))
    out_specs = [row(HG_W, lag), state, heads_t, heads,
                 col(MLA_HEADS * MLA_V), row(KV_LORA, cur), col(MLA_ROPE)]
    ohg, st, *mla = pl.pallas_call(
        functools.partial(_inproj_hgrn_body, nt=nt),
        out_shape=[jax.ShapeDtypeStruct(s, dt) for s, dt in shapes],
        grid=(n + 1,),
        in_specs=in_specs,
        out_specs=out_specs,
        scratch_shapes=[pltpu.VMEM((tm, 4 * HG_W), F32), pltpu.VMEM((HG_HEADS, HG_DV, HG_DK), F32)],
        compiler_params=pltpu.CompilerParams(dimension_semantics=("arbitrary",),
                                             vmem_limit_bytes=VMEM_LIMIT),
        name="in_proj_hgrn",
    )(*args)
    return (ohg, jnp.swapaxes(st, 2, 3), *mla)


def _hgrn_body(raw_ref, lbl_ref, gain_ref, s0_ref, tin_ref, tup_ref, ones_ref,
               o_out, s_out,
               st_scr, b_scr, q_scr, kk_scr, v_scr, eb_scr, qt_scr, kt_scr, vb_scr, p_scr, o_scr, *, th):
    i = pl.program_id(1)

    @pl.when(i == 0)
    def _():
        for h in range(HG_HEADS):
            st_scr[h] = s0_ref[0, h].T

    lbl = lbl_ref[...]
    e = jnp.exp(lbl - jnp.max(lbl, axis=0, keepdims=True))
    lb = (e / jnp.sum(e, axis=0, keepdims=True))[0:1]

    q = raw_ref[0, :, 0:HG_W] * HG_DK ** -0.5
    f = lb + (1.0 - lb) * jax.nn.sigmoid(raw_ref[0, :, HG_W:2 * HG_W])
    kk = 1.0 - f
    v = raw_ref[0, :, 2 * HG_W:3 * HG_W]
    l1, l2, l3 = _split3(jnp.log(f))
    tin = tin_ref[...]
    tup = tup_ref[...]
    b = _dot(tin, l1) + _dot(tin, l2) + _dot(tin, l3)
    c = _dot(tup, l1) + _dot(tup, l2) + _dot(tup, l3)
    eb = jnp.exp(b)
    b_scr[...] = b
    q_scr[...] = q
    kk_scr[...] = kk
    v_scr[...] = v
    eb_scr[...] = eb
    qt_scr[...] = (q * eb).astype(BF16)
    kt_scr[...] = (kk * jnp.exp(c)).astype(BF16)
    vb_scr[...] = v.astype(BF16)

    rowid = lax.broadcasted_iota(jnp.int32, (SUB, HG_W), 0)

    def step(j, carry):
        r0 = pl.multiple_of(j * SUB, SUB)
        bj = b_scr[pl.ds(r0, SUB), :]
        qj = q_scr[pl.ds(r0, SUB), :]
        for s in range(SUB):
            bs = b_scr[pl.ds(r0 + s, 1), :]
            ks = kk_scr[pl.ds(r0 + s, 1), :]
            dec = jnp.exp(jnp.where(rowid >= s, bj - bs, -jnp.inf))
            p_scr[s * SUB:(s + 1) * SUB, :] = (qj * dec * ks).astype(BF16)
        r = _dot(p_scr[...], ones_ref[...])
        od = jnp.zeros((SUB, HG_W), F32)
        for s in range(SUB):
            od = od + r[s * SUB:(s + 1) * SUB, :] * v_scr[pl.ds(r0 + s, 1), :]
        qt = qt_scr[pl.ds(r0, SUB), :]
        kt = kt_scr[pl.ds(r0, SUB), :]
        vb = vb_scr[pl.ds(r0, SUB), :]
        dj = eb_scr[pl.ds(r0 + SUB - 1, 1), :]
        for h in range(HG_HEADS):
            hs = slice(h * HG_DK, (h + 1) * HG_DK)
            st = st_scr[h]
            oi = _dot_nt(qt[:, hs], st.astype(BF16))
            st_scr[h] = st * dj[:, hs] + _dot_tn(vb[:, hs], kt[:, hs])
            o_scr[pl.ds(r0, SUB), hs] = oi + od[:, hs]
        return carry

    lax.fori_loop(0, th // SUB, step, 0)

    gate = raw_ref[0, :, 3 * HG_W:4 * HG_W]
    gate = gate * jax.nn.sigmoid(gate)
    gain = gain_ref[...]
    for h in range(HG_HEADS):
        hs = slice(h * HG_DV, (h + 1) * HG_DV)
        o_out[0, :, hs] = (_rms(o_scr[:, hs], gain) * gate[:, hs]).astype(BF16)

    @pl.when(i == pl.num_programs(1) - 1)
    def _():
        for h in range(HG_HEADS):
            s_out[0, h] = st_scr[h].T


def _step_matrices(th):
    r = np.arange(th)
    same = (r[:, None] // SUB) == (r[None, :] // SUB)
    tin = (same & (r[None, :] <= r[:, None])).astype(np.float32)
    tup = (same & (r[None, :] > r[:, None])).astype(np.float32)
    hd = np.arange(HG_W) // HG_DK
    ones = (hd[:, None] == hd[None, :]).astype(np.float32)
    return jnp.asarray(tin, BF16), jnp.asarray(tup, BF16), jnp.asarray(ones, BF16)


def _hgrn(raw, lb_logits, gain, s0, th):
    B, L, _ = raw.shape
    tin, tup, ones = _step_matrices(th)
    f32s = lambda: pltpu.VMEM((th, HG_W), F32)
    b16s = lambda: pltpu.VMEM((th, HG_W), BF16)
    return pl.pallas_call(
        functools.partial(_hgrn_body, th=th),
        out_shape=[jax.ShapeDtypeStruct((B, L, HG_W), BF16),
                   jax.ShapeDtypeStruct((B, HG_HEADS, HG_DK, HG_DV), F32)],
        grid=(B, L // th),
        in_specs=[pl.BlockSpec((1, th, 4 * HG_W), lambda b, i: (b, i, 0)),
                  _const_spec(lb_logits.shape), _const_spec(gain.shape),
                  pl.BlockSpec((1, HG_HEADS, HG_DK, HG_DV), lambda b, i: (b, 0, 0, 0)),
                  _const_spec(tin.shape), _const_spec(tup.shape), _const_spec(ones.shape)],
        out_specs=[pl.BlockSpec((1, th, HG_W), lambda b, i: (b, i, 0)),
                   pl.BlockSpec((1, HG_HEADS, HG_DK, HG_DV), lambda b, i: (b, 0, 0, 0))],
        scratch_shapes=[pltpu.VMEM((HG_HEADS, HG_DV, HG_DK), F32),
                        f32s(), f32s(), f32s(), f32s(), f32s(),
                        b16s(), b16s(), b16s(),
                        pltpu.VMEM((SUB * SUB, HG_W), BF16), f32s()],
        compiler_params=pltpu.CompilerParams(dimension_semantics=("arbitrary", "arbitrary"),
                                             vmem_limit_bytes=VMEM_LIMIT),
        name="hgrn_scan",
    )(raw, lb_logits, gain, s0, tin, tup, ones)


def _unit_time():
    i = np.arange(UNIT)
    return SLABS * (i % SLAB_ROWS) + i // SLAB_ROWS


def _unit_tables():
    t = _unit_time()
    tin = (t[None, :] <= t[:, None]).astype(np.float32)
    rows = []
    for c in (1, 2, 4, 8):
        g = np.arange(SLAB_ROWS)
        t_bnd = SLABS * (2 * c * (g // (2 * c)) + c - 1) + SLABS - 1
        rows.append((t[None, :] <= t_bnd[:, None]).astype(np.float32))
    tin_ext = np.concatenate([tin] + rows, axis=0)
    x = t[:, None] ^ t[None, :]
    lv = np.where(x > 0, np.floor(np.log2(np.maximum(x, 1))), DIAG_LEVEL).astype(np.int32)
    lv = np.where(t[None, :] <= t[:, None], lv, -1)
    return jnp.asarray(tin_ext, BF16), jnp.asarray(np.concatenate([lv, lv], axis=1), jnp.int32)


def _pair_blockdiag(a, b):
    z = jnp.zeros_like(a)
    return jnp.concatenate([jnp.concatenate([a, z], axis=1), jnp.concatenate([z, b], axis=1)], axis=0)


def _hgrn_units_math(field, nu, st, lbl_ref, gain_ref, tin_ref, lv_ref, unperm_ref, o_out, after_level):
    lbl = lbl_ref[...]
    e = jnp.exp(lbl - jnp.max(lbl, axis=0, keepdims=True))
    lb = (e / jnp.sum(e, axis=0, keepdims=True))[0:1]
    tin = tin_ref[...]
    lv = lv_ref[...]
    gain = gain_ref[...]

    units = range(nu)
    pairs = range(HG_HEADS // 2)

    def pair_cols(x, hp):
        c0 = 2 * HG_DK * hp
        return x[:, c0:c0 + HG_DK], x[:, c0 + HG_DK:c0 + 2 * HG_DK]

    gates = [field(u, 3) for u in units]
    q = [field(u, 0) * HG_DK ** -0.5 for u in units]
    f = [lb + (1.0 - lb) * jax.nn.sigmoid(field(u, 1)) for u in units]
    kk = [1.0 - f[u] for u in units]
    lsplit = [_split2(jnp.log(f[u]) * np.float32(1.0 / np.log(2.0))) for u in units]
    bx = [_dot(tin, lsplit[u][0]) + _dot(tin, lsplit[u][1]) for u in units]
    b = [bx[u][0:UNIT] for u in units]
    bs = [[b[u][SLAB_ROWS * p:SLAB_ROWS * (p + 1)] for p in range(SLABS)] for u in units]
    vb = [field(u, 2).astype(BF16) for u in units]
    qb = [q[u].astype(BF16) for u in units]
    kkb_t = [kk[u].T.astype(BF16) for u in units]

    def slab_decay(u, level, p):
        if level < 3:
            half = 1 << level
            ref = p - p % (2 * half) + half - 1
            if p == ref:
                return jnp.ones_like(bs[u][p])
            return jnp.exp2(bs[u][p] - bs[u][ref] if p > ref else bs[u][ref] - bs[u][p])
        r = bx[u][UNIT + SLAB_ROWS * (level - 3):UNIT + SLAB_ROWS * (level - 2)]
        if level == N_LEVELS - 1:
            h = SLAB_ROWS // 2
            return jnp.concatenate([jnp.exp2(r[:h] - bs[u][p][:h]), jnp.exp2(bs[u][p][h:] - r[h:])], axis=0)
        return jnp.exp2(-jnp.abs(bs[u][p] - r))

    acc = [[jnp.zeros((UNIT, 2 * HG_DK), F32) for _ in pairs] for _ in units]
    for level in range(N_LEVELS + 1):
        here = lv == level
        for u in units:
            if level == DIAG_LEVEL:
                qt, kt_t = qb[u], kkb_t[u]
            else:
                dec = jnp.concatenate([slab_decay(u, level, p) for p in range(SLABS)], axis=0)
                qt, kt_t = qb[u] * dec.astype(BF16), kkb_t[u] * dec.T.astype(BF16)
            for hp in pairs:
                c0 = 2 * HG_DK * hp
                a2 = _dot(qt[:, c0:c0 + 2 * HG_DK], _pair_blockdiag(kt_t[c0:c0 + HG_DK], kt_t[c0 + HG_DK:c0 + 2 * HG_DK]))
                acc[u][hp] = jnp.where(here, a2, acc[u][hp])
        after_level(level)

    b_end = [b[u][UNIT - 1:UNIT] for u in units]
    qi = [(q[u] * jnp.exp2(b[u])).astype(BF16) for u in units]
    ks = [(kk[u] * jnp.exp2(b_end[u] - b[u])).astype(BF16) for u in units]
    d = [jnp.exp2(b_end[u]) for u in units]
    o_intra = [[_dot(acc[u][hp].astype(BF16), _pair_blockdiag(*pair_cols(vb[u], hp))) for hp in pairs]
               for u in units]
    ds = [[_dot_tn(vb[u][:, h * HG_DV:(h + 1) * HG_DV], ks[u][:, h * HG_DK:(h + 1) * HG_DK])
           for h in range(HG_HEADS)] for u in units]
    o_inter = []
    for u in units:
        o_inter.append([_dot_nt(qi[u][:, 2 * HG_DK * hp:2 * HG_DK * (hp + 1)],
                                _pair_blockdiag(st[2 * hp].astype(BF16), st[2 * hp + 1].astype(BF16)))
                        for hp in pairs])
        st = [st[h] * d[u][:, h * HG_DK:(h + 1) * HG_DK] + ds[u][h] for h in range(HG_HEADS)]
    res = []
    for u in units:
        o = jnp.concatenate([o_intra[u][hp] + o_inter[u][hp] for hp in pairs], axis=1)
        gate = gates[u] * jax.nn.sigmoid(gates[u])
        r = jnp.concatenate([_rms(o[:, h * HG_DV:(h + 1) * HG_DV], gain) for h in range(HG_HEADS)], axis=1)
        res.append((r * gate).astype(BF16))
    for u in units:
        o_out[0, UNIT * u:UNIT * (u + 1), :] = _dot(unperm_ref[...], res[u]).astype(BF16)
    return st


def _attn_body(qt_ref, k_ref, vt_ref, o_ref, acc_scr, *, tq, nq):
    for qi in range(nq):
        qt = qt_ref[0, 0, :, qi * tq:(qi + 1) * tq]
        nblk = qi + 1
        scores = lambda i: _dot(k_ref[0, 0, i * tq:(i + 1) * tq, :], qt)
        pending = [scores(i) for i in range(min(ATTN_LOOKAHEAD, nblk))]
        m = jnp.full((1, tq), -jnp.inf, F32)
        l = jnp.zeros((1, tq), F32)
        for ki in range(nblk):
            st = pending.pop(0)
            if ki + ATTN_LOOKAHEAD < nblk:
                pending.append(scores(ki + ATTN_LOOKAHEAD))
            if ki == nblk - 1:
                kc = lax.broadcasted_iota(jnp.int32, st.shape, 0) // CHUNK
                qc = lax.broadcasted_iota(jnp.int32, st.shape, 1) // CHUNK
                st = jnp.where(kc <= qc, st, -jnp.inf)
            m_new = jnp.maximum(m, jnp.max(st, axis=0, keepdims=True))
            p = jnp.exp2(st - m_new)
            alpha = jnp.exp2(m - m_new)
            l = alpha * l + jnp.sum(p, axis=0, keepdims=True)
            pv = _dot(vt_ref[0, :, ki * tq:(ki + 1) * tq], p.astype(BF16))
            if ki == 0:
                acc_scr[qi % 2] = pv
            else:
                acc_scr[qi % 2] = alpha * acc_scr[qi % 2] + pv
            m = m_new
        o_ref[0, 0, qi * tq:(qi + 1) * tq, :] = (acc_scr[qi % 2] / l).T.astype(BF16)


def _attn(qt, k, vt, tq):
    B, _, L, _ = k.shape
    per_head = lambda w: pl.BlockSpec((1, 1, L, w), lambda b, h: (b, h, 0, 0))
    return pl.pallas_call(
        functools.partial(_attn_body, tq=tq, nq=L // tq),
        out_shape=jax.ShapeDtypeStruct((B, MLA_HEADS, L, MLA_V), BF16),
        grid=(B, MLA_HEADS),
        in_specs=[pl.BlockSpec((1, 1, QK_W, L), lambda b, h: (b, h, 0, 0)), per_head(QK_W),
                  pl.BlockSpec((1, MLA_V, L), lambda b, h: (b, h, 0))],
        out_specs=per_head(MLA_V),
        scratch_shapes=[pltpu.VMEM((2, MLA_V, tq), F32)],
        compiler_params=pltpu.CompilerParams(dimension_semantics=("arbitrary",) * 2,
                                             vmem_limit_bytes=VMEM_LIMIT),
        name="mla_attn_prompt",
    )(qt, k, vt)


def _attn_cache_body(q_ref, kn_ref, vn_ref, clat_ref, ckpe_ref, wuk_ref, wuv_ref, place_ref, o_ref):
    n = q_ref.shape[2]
    latc = clat_ref[0].astype(BF16)
    kpec = ckpe_ref[0].astype(BF16)
    qlat, qrope, s_new = [], [], []
    for h in range(MLA_HEADS):
        qh = q_ref[0, h]
        ws = slice(MLA_NOPE * h, MLA_NOPE * (h + 1))
        qlat.append(_dot_nt(qh[:, 0:MLA_NOPE], wuk_ref[:, ws]).astype(BF16))
        qrope.append(_dot_nt(qh[:, MLA_NOPE:QK_W], place_ref[:, ws]).astype(BF16))
        s_new.append(_dot_nt(qh, kn_ref[0, h]))
    s_new = jnp.concatenate(s_new, axis=0)
    s_c = (_dot_nt(jnp.concatenate(qlat, axis=0), latc)
           + _dot(jnp.concatenate(qrope, axis=0), kpec))
    m = jnp.maximum(jnp.max(s_c, axis=-1, keepdims=True), jnp.max(s_new, axis=-1, keepdims=True))
    p_c = jnp.exp2(s_c - m)
    p_new = jnp.exp2(s_new - m)
    l = jnp.sum(p_c, axis=-1, keepdims=True) + jnp.sum(p_new, axis=-1, keepdims=True)
    o_lat = _dot(p_c.astype(BF16), latc).astype(BF16)
    p_new = p_new.astype(BF16)
    for h in range(MLA_HEADS):
        ws = slice(MLA_V * h, MLA_V * (h + 1))
        rs = slice(n * h, n * (h + 1))
        acc = _dot(o_lat[rs], wuv_ref[:, ws]) + _dot_nt(p_new[rs], vn_ref[0, ws, :])
        o_ref[0, h] = (acc / l[rs]).astype(BF16)


def _rope_place():
    p = np.zeros((MLA_ROPE, MLA_HEADS * MLA_NOPE), np.float32)
    for h in range(MLA_HEADS):
        for i in range(ROPE_HALF):
            p[i, MLA_NOPE * h + ROPE_HALF * h + i] = 1.0
            p[ROPE_HALF + i, MLA_NOPE * h + ROPE_HALF * ((h + 1) % MLA_HEADS) + i] = 1.0
    return jnp.asarray(p, BF16)


def _attn_cache(q, kn, vn, clat, ckpe, wuk, wuv):
    B, _, L, _ = q.shape
    P = clat.shape[1]
    place = _rope_place()
    bspec = lambda n, w: pl.BlockSpec((1, n, w), lambda b: (b, 0, 0))
    heads = lambda w: pl.BlockSpec((1, MLA_HEADS, L, w), lambda b: (b, 0, 0, 0))
    return pl.pallas_call(
        _attn_cache_body,
        out_shape=jax.ShapeDtypeStruct((B, MLA_HEADS, L, MLA_V), BF16),
        grid=(B,),
        in_specs=[heads(QK_W), heads(QK_W), bspec(MLA_HEADS * MLA_V, L),
                  bspec(P, KV_LORA), bspec(MLA_ROPE, P),
                  _const_spec(wuk.shape), _const_spec(wuv.shape), _const_spec(place.shape)],
        out_specs=heads(MLA_V),
        compiler_params=pltpu.CompilerParams(dimension_semantics=("arbitrary",),
                                             vmem_limit_bytes=VMEM_LIMIT),
        name="mla_attn_cache",
    )(q, kn, vn, clat, ckpe, wuk, wuv, place)


def _ffn_body(x_ref, ohg_ref, omla_ref, mod_ref, conv0_ref, wout_ref, gffn_ref, wup_ref, cw_ref, cb_ref,
              wdn_ref, gfin_ref, y_ref, conv_out, carry_scr, a_scr, u_scr, *, tm, ft):
    @pl.when(pl.program_id(1) == 0)
    def _():
        carry_scr[...] = conv0_ref[0]

    mod = mod_ref[0]
    g1 = mod[2:3]
    sh2 = mod[3:4]
    sc2 = mod[4:5]
    g2 = mod[5:6]
    omla = jnp.concatenate([omla_ref[0, h] for h in range(MLA_HEADS)], axis=1)
    o = _dot(ohg_ref[0], wout_ref[0:HG_W, :]) + _dot(omla, wout_ref[HG_W:2 * HG_W, :])
    x1 = x_ref[0] + g1 * o
    h2 = (_rms(x1, gffn_ref[...]) * (1.0 + sc2) + sh2).astype(BF16)
    for j in range(D_FF // ft):
        cs = slice(j * ft, (j + 1) * ft)
        a = _dot(h2, wup_ref[:, cs])
        v = _dot(h2, wup_ref[:, D_FF + j * ft:D_FF + (j + 1) * ft])
        a_scr[0:CARRY_ROWS, :] = carry_scr[:, cs]
        a_scr[CARRY_ROWS:CARRY_ROWS + tm, :] = a
        carry_scr[:, cs] = a[tm - CARRY_ROWS:tm]
        cw = cw_ref[:, cs]
        conv = (cb_ref[:, cs] + cw[0:1] * a_scr[CARRY_ROWS - 2:CARRY_ROWS - 2 + tm, :]
                + cw[1:2] * a_scr[CARRY_ROWS - 1:CARRY_ROWS - 1 + tm, :] + cw[2:3] * a)
        g = 0.5 * conv * (1.0 + lax.erf(conv * np.float32(np.sqrt(0.5))))
        u_scr[:, cs] = (g * v).astype(BF16)
        conv_out[0, :, cs] = a[tm - (CONV_W - 1):tm]

    x2 = x1 + g2 * _dot(u_scr[...], wdn_ref[...])
    y_ref[0] = _rms(x2, gfin_ref[...])


def _ffn(x, ohg, omla, mod, conv0, wout, gffn, wup, cw, cb, wdn, gfin, tm, ft):
    B, L, D = x.shape
    row = lambda w: pl.BlockSpec((1, tm, w), lambda b, i: (b, i, 0))
    once = lambda a: pl.BlockSpec(a.shape, lambda b, i: (0,) * a.ndim, pipeline_mode=pl.Buffered(1))
    return pl.pallas_call(
        functools.partial(_ffn_body, tm=tm, ft=ft),
        out_shape=[jax.ShapeDtypeStruct((B, L, D), F32),
                   jax.ShapeDtypeStruct((B, CONV_W - 1, D_FF), F32)],
        grid=(B, L // tm),
        in_specs=[row(D), row(HG_W), pl.BlockSpec((1, MLA_HEADS, tm, MLA_V), lambda b, i: (b, 0, i, 0)),
                  pl.BlockSpec((1, 6, D), lambda b, i: (b, 0, 0)),
                  pl.BlockSpec((1, CARRY_ROWS, D_FF), lambda b, i: (b, 0, 0)),
                  once(wout), once(gffn), once(wup), once(cw), once(cb), once(wdn), once(gfin)],
        out_specs=[row(D), pl.BlockSpec((1, CONV_W - 1, D_FF), lambda b, i: (b, 0, 0))],
        scratch_shapes=[pltpu.VMEM((CARRY_ROWS, D_FF), F32), pltpu.VMEM((CARRY_ROWS + tm, ft), F32),
                        pltpu.VMEM((tm, D_FF), BF16)],
        compiler_params=pltpu.CompilerParams(dimension_semantics=("arbitrary", "arbitrary"),
                                             vmem_limit_bytes=VMEM_LIMIT),
        name="out_ffn",
    )(x, ohg, omla, mod, conv0, wout, gffn, wup, cw, cb, wdn, gfin)


def _prep_weights(w_in, mla_q_norm_gain, mla_kv_norm_gain, w_uq, w_uk, w_uv):
    win = w_in.astype(BF16)
    kp = win[:, 4 * HG_W + Q_LORA + KV_LORA:]
    wkx = jnp.concatenate([jnp.tile(kp[:, :ROPE_HALF], (1, MLA_HEADS)),
                           jnp.tile(kp[:, ROPE_HALF:], (1, MLA_HEADS))], axis=1)
    uq = w_uq.reshape(Q_LORA, MLA_HEADS, MLA_NOPE + MLA_ROPE)
    wuqn = uq[:, :, :MLA_NOPE].reshape(Q_LORA, MLA_HEADS * MLA_NOPE).astype(BF16)
    x1a = uq[:, :, MLA_NOPE:MLA_NOPE + ROPE_HALF].reshape(Q_LORA, MLA_HEADS * ROPE_HALF)
    x2a = uq[:, :, MLA_NOPE + ROPE_HALF:].reshape(Q_LORA, MLA_HEADS * ROPE_HALF)
    wuqr = jnp.concatenate([x1a, x2a, jnp.roll(x1a, ROPE_HALF, axis=1), jnp.roll(x2a, ROPE_HALF, axis=1)],
                           axis=1).astype(BF16)
    wuk = w_uk.reshape(KV_LORA, MLA_HEADS * MLA_NOPE).astype(BF16)
    wuv = w_uv.reshape(KV_LORA, MLA_HEADS * MLA_V).astype(BF16)
    return (win, wkx, kp.T, mla_q_norm_gain.reshape(1, -1), mla_kv_norm_gain.reshape(1, -1), wuqn, wuqr, wuk, wuv)


def _rope_tables(pos):
    inv_freq = ROPE_THETA ** (-jnp.arange(ROPE_HALF, dtype=F32) / ROPE_HALF)
    ang = pos.astype(F32)[:, None] * inv_freq[None, :]
    cos, sin = jnp.cos(ang), jnp.sin(ang)
    return jnp.tile(cos, (1, MLA_HEADS)), jnp.tile(sin, (1, MLA_HEADS)), cos.T, sin.T


def _tile(n, pref):
    return pref if n % pref == 0 else n


def kernel(x_prompt, x_sample, c_prompt, c_sample, cache_kv_latent, cache_k_rope, state_hgrn, state_ffn_conv, w_ada, b_ada, norm_mix_gain, w_in, hg_lb_logits, hg_norm_gain, mla_q_norm_gain, mla_kv_norm_gain, w_uq, w_uk, w_uv, w_out, norm_ffn_gain, w_up, conv_w, conv_b, w_down, final_norm_gain):
    assert w_ada.shape[0] == 1, "single-layer trunk"
    B, L, D = x_prompt.shape
    Bs, Ls, _ = x_sample.shape
    past = cache_kv_latent.shape[2]

    mod = _ada(jnp.concatenate([c_prompt, c_sample], axis=0), w_ada[0], b_ada)
    mod = mod.reshape(B + Bs, 6, D)
    wts = _prep_weights(w_in[0], mla_q_norm_gain[0], mla_kv_norm_gain[0], w_uq[0], w_uk[0], w_uv[0])
    gmix = norm_mix_gain
    wout = w_out[0].astype(BF16)
    wup = w_up[0].astype(BF16)
    wdn = w_down[0].astype(BF16)
    gfin = final_norm_gain.reshape(1, D)
    ffn_w = (wout, norm_ffn_gain, wup, conv_w[0], conv_b, wdn, gfin)

    def layer(x, mod_x, pos, s0, conv0, cache):
        n, l, _ = x.shape
        rope = _rope_tables(pos)
        tm = _tile(l, 512)
        if tm % UNIT == 0:
            ohg, s_new, q, k, v, lat, kpe = _inproj_hgrn(x, mod_x, gmix, wts, rope, hg_lb_logits, hg_norm_gain,
                                                         s0, tm)
        else:
            raw, q, k, v, lat, kpe = _inproj(x, mod_x, gmix, wts, rope, tm)
            ohg, s_new = _hgrn(raw, hg_lb_logits, hg_norm_gain, s0, _tile(l, 256))
        if cache is None:
            omla = _attn(q, k, v, _tile(l, 512))
        else:
            omla = _attn_cache(q, k, v, cache[0], cache[1], wts[7], wts[8])
        y, conv_new = _ffn(x, ohg, omla, mod_x, conv0, *ffn_w, _tile(l, 512), FF_TILE)
        return y, lat[None], jnp.swapaxes(kpe, 1, 2)[None], s_new[None], conv_new[None]

    zeros_state = jnp.zeros((B, HG_HEADS, HG_DK, HG_DV), F32)
    zeros_conv = jnp.zeros((B, CARRY_ROWS, D_FF), F32)
    conv0_s = jnp.pad(state_ffn_conv[0], ((0, 0), (CARRY_ROWS - (CONV_W - 1), 0), (0, 0)))
    yp, latp, kpep, hgp, cvp = layer(x_prompt, mod[:B], jnp.arange(L), zeros_state, zeros_conv, None)
    ys, lats, kpes, hgs, cvs = layer(x_sample, mod[B:], past + jnp.arange(Ls), state_hgrn[0], conv0_s,
                                     (cache_kv_latent[0], jnp.swapaxes(cache_k_rope[0], 1, 2)))
    return (yp, ys, latp, kpep, hgp, cvp, lats, kpes, hgs, cvs)
```

```python
import functools

import numpy as np
import jax
import jax.numpy as jnp
from jax import lax
from jax.experimental import pallas as pl
from jax.experimental.pallas import tpu as pltpu

F32 = jnp.float32
BF16 = jnp.bfloat16

D_MODEL = 1024
CHUNK = 64
HG_HEADS = 4
HG_DK = 128
HG_DV = 128
HG_W = HG_HEADS * HG_DK
MLA_HEADS = 4
MLA_NOPE = 128
MLA_ROPE = 64
ROPE_HALF = MLA_ROPE // 2
MLA_V = 128
Q_LORA = 384
KV_LORA = 256
ROPE_THETA = 10000.0
MLA_SCALE = (MLA_NOPE + MLA_ROPE) ** -0.5
Q_SCALE = MLA_SCALE * float(np.log2(np.e))
QK_W = 2 * MLA_NOPE
D_FF = 2816
CONV_W = 3
EPS = 1e-6

SUB = 16
FF_TILE = D_FF
ATTN_LOOKAHEAD = 4
CARRY_ROWS = 8
VMEM_LIMIT = 56 * 1024 * 1024
UNIT = 128
SLABS = 8
SLAB_ROWS = UNIT // SLABS
N_LEVELS = 7
DIAG_LEVEL = N_LEVELS


def _dot(a, b):
    return jnp.dot(a, b, preferred_element_type=F32)


def _dot_nt(a, b):
    return lax.dot_general(a, b, (((1,), (1,)), ((), ())), preferred_element_type=F32)


def _dot_tn(a, b):
    return lax.dot_general(a, b, (((0,), (0,)), ((), ())), preferred_element_type=F32)


def _split3(x):
    x1 = x.astype(BF16)
    r1 = x - x1.astype(F32)
    x2 = r1.astype(BF16)
    r2 = r1 - x2.astype(F32)
    return x1, x2, r2.astype(BF16)


def _split2(x):
    x1 = x.astype(BF16)
    return x1, (x - x1.astype(F32)).astype(BF16)


def _rms(x, g):
    return x * lax.rsqrt(jnp.mean(x * x, axis=-1, keepdims=True) + EPS) * g


def _const_spec(shape):
    n = len(shape)
    return pl.BlockSpec(shape, lambda *_: (0,) * n)


def _ada_body(c_ref, w_ref, b_ref, o_ref):
    c = c_ref[...]
    s = c * jax.nn.sigmoid(c)
    s1, s2, _ = _split3(s)
    w = w_ref[...]
    w1 = w.astype(BF16)
    w2 = (w - w1.astype(F32)).astype(BF16)
    o_ref[...] = _dot(s1, w1) + _dot(s1, w2) + _dot(s2, w1) + b_ref[...]


def _ada(c, w_ada, b_ada):
    n, d = c.shape
    nout = w_ada.shape[1]
    tn = 2048
    return pl.pallas_call(
        _ada_body,
        out_shape=jax.ShapeDtypeStruct((n, nout), F32),
        grid=(nout // tn,),
        in_specs=[pl.BlockSpec((n, d), lambda j: (0, 0)),
                  pl.BlockSpec((d, tn), lambda j: (0, j)),
                  pl.BlockSpec((1, tn), lambda j: (0, j))],
        out_specs=pl.BlockSpec((n, tn), lambda j: (0, j)),
        compiler_params=pltpu.CompilerParams(dimension_semantics=("arbitrary",)),
        name="adaln_mod",
    )(c, w_ada, b_ada)


def _rope_slots(r1, r2, nope, out_ref, slot):
    for h in range(MLA_HEADS):
        xh = jnp.where(slot == h, r1, jnp.where(slot == (h + 1) % MLA_HEADS, r2, 0.0))
        out_ref[0, h, :, 0:MLA_NOPE] = nope[:, MLA_NOPE * h:MLA_NOPE * (h + 1)].astype(BF16)
        out_ref[0, h, :, MLA_NOPE:QK_W] = xh.astype(BF16)


def _mla_stages(hb, refs, outs, rope_refs, q_transposed):
    (wcq_ref, wckv_ref, wkx_ref, wkpt_ref, qng_ref, kvng_ref, wuqn_ref, wuqr_ref, wuk_ref, wuvt_ref) = refs
    q_out, k_out, v_out, lat_out, kpe_out = outs
    cos_ref, sin_ref, cost_ref, sint_ref = rope_refs
    t = {}

    def project():
        t["cq"] = _dot(hb, wcq_ref[...])
        t["ckv"] = _dot(hb, wckv_ref[...])
        t["kx"] = _dot(hb, wkx_ref[...])
        if not q_transposed:
            t["kxt"] = _dot_nt(wkpt_ref[...], hb)

    def q_up():
        cq = _rms(t["cq"], qng_ref[...])
        if q_transposed:
            cq = cq.T.astype(BF16)
            t["qn"] = _dot(wuqn_ref[...], cq) * Q_SCALE
            t["qr"] = _dot(wuqr_ref[...], cq) * Q_SCALE
        else:
            cq = cq.astype(BF16)
            t["qn"] = _dot(cq, wuqn_ref[...]) * Q_SCALE
            t["qr"] = _dot(cq, wuqr_ref[...]) * Q_SCALE

    def kv_up():
        lat = _rms(t["ckv"], kvng_ref[...])
        lat_out[0] = lat
        latb = lat.astype(BF16)
        t["kn"] = _dot(latb, wuk_ref[...])
        if q_transposed:
            v_out[0] = _dot(wuvt_ref[...], lat.T.astype(BF16)).astype(BF16)
        else:
            v_out[0] = _dot_nt(wuvt_ref[...], latb).astype(BF16)

    def slot_of(cos):
        return lax.broadcasted_iota(jnp.int32, cos.shape, 1) // ROPE_HALF

    def q_rope():
        qr, qn = t["qr"], t["qn"]
        if q_transposed:
            cos = jnp.concatenate([cost_ref[...]] * MLA_HEADS, axis=0)
            sin = jnp.concatenate([sint_ref[...]] * MLA_HEADS, axis=0)
            qr1 = qr[0:128] * cos - qr[128:256] * sin
            qr2 = qr[384:512] * cos + qr[256:384] * sin
            slot = lax.broadcasted_iota(jnp.int32, cos.shape, 0) // ROPE_HALF
            for h in range(MLA_HEADS):
                xh = jnp.where(slot == h, qr1, jnp.where(slot == (h + 1) % MLA_HEADS, qr2, 0.0))
                q_out[0, h, 0:MLA_NOPE, :] = qn[MLA_NOPE * h:MLA_NOPE * (h + 1)].astype(BF16)
                q_out[0, h, MLA_NOPE:QK_W, :] = xh.astype(BF16)
            return
        cos, sin = cos_ref[...], sin_ref[...]
        qr1 = qr[:, 0:128] * cos - qr[:, 128:256] * sin
        qr2 = qr[:, 384:512] * cos + qr[:, 256:384] * sin
        _rope_slots(qr1, qr2, qn, q_out, slot_of(cos))

    def k_rope():
        cos, sin, kx = cos_ref[...], sin_ref[...], t["kx"]
        kr1 = kx[:, 0:128] * cos - kx[:, 128:256] * sin
        kr2 = kx[:, 128:256] * cos + kx[:, 0:128] * sin
        _rope_slots(kr1, kr2, t["kn"], k_out, slot_of(cos))
        if q_transposed:
            lane = lax.broadcasted_iota(jnp.int32, cos.shape, 1)
            kpe_out[0] = jnp.where(lane < ROPE_HALF, kr1, kr2).T[0:MLA_ROPE]
            return
        kxt = t["kxt"]
        x1t, x2t = kxt[0:ROPE_HALF], kxt[ROPE_HALF:MLA_ROPE]
        cost, sint = cost_ref[...], sint_ref[...]
        kpe_out[0] = jnp.concatenate([x1t * cost - x2t * sint, x2t * cost + x1t * sint], axis=0)

    return [project, q_up, kv_up, q_rope, k_rope]


def _normed_input(x_ref, mod_ref, gmix_ref):
    mod = mod_ref[0]
    return (_rms(x_ref[0], gmix_ref[...]) * (1.0 + mod[1:2]) + mod[0:1]).astype(BF16)


def _win_views(win_ref):
    hgw = 4 * HG_W
    return (win_ref.at[:, 0:hgw], win_ref.at[:, hgw:hgw + Q_LORA],
            win_ref.at[:, hgw + Q_LORA:hgw + Q_LORA + KV_LORA])


def _inproj_body(x_ref, mod_ref, gmix_ref, win_ref, wkx_ref, wkpt_ref, qng_ref, kvng_ref,
                 wuqn_ref, wuqr_ref, wuk_ref, wuvt_ref, cos_ref, sin_ref, cost_ref, sint_ref,
                 hg_out, q_out, k_out, v_out, lat_out, kpe_out):
    whg_ref, wcq_ref, wckv_ref = _win_views(win_ref)
    hb = _normed_input(x_ref, mod_ref, gmix_ref)
    hg_out[0] = _dot(hb, whg_ref[...])
    for stage in _mla_stages(hb, (wcq_ref, wckv_ref, wkx_ref, wkpt_ref, qng_ref, kvng_ref, wuqn_ref, wuqr_ref,
                                  wuk_ref, wuvt_ref),
                             (q_out, k_out, v_out, lat_out, kpe_out), (cos_ref, sin_ref, cost_ref, sint_ref),
                             False):
        stage()


def _inproj_hgrn_body(x_ref, mod_ref, gmix_ref, win_ref, wkx_ref, wkpt_ref, qng_ref, kvng_ref,
                      wuqn_ref, wuqr_ref, wuk_ref, wuvt_ref, cos_ref, sin_ref, cost_ref, sint_ref,
                      perm_ref, lbl_ref, gain_ref, s0t_ref, tin_ref, lv_ref, unperm_ref,
                      hg_out, st_out, q_out, k_out, v_out, lat_out, kpe_out, raw_scr, st_scr, *, nt):
    j = pl.program_id(0)

    @pl.when(j == 0)
    def _():
        raw_scr[...] = jnp.zeros(raw_scr.shape, F32)

    first_of_batch = lax.rem(jnp.maximum(j - 1, 0), nt) == 0
    st = [jnp.where(first_of_batch, s0t_ref[0, hd], st_scr[hd]) for hd in range(HG_HEADS)]

    whg_ref, wcq_ref, wckv_ref = _win_views(win_ref)
    n_units = x_ref.shape[1] // UNIT
    t = {}

    def start():
        hb = _normed_input(x_ref, mod_ref, gmix_ref)
        t["hbp"] = jnp.concatenate([_dot(perm_ref[...], hb[UNIT * u:UNIT * (u + 1)]).astype(BF16)
                                    for u in range(n_units)], axis=0)
        t["mla"] = _mla_stages(hb, (wcq_ref, wckv_ref, wkx_ref, wkpt_ref, qng_ref, kvng_ref, wuqn_ref,
                                    wuqr_ref, wuk_ref, wuvt_ref),
                               (q_out, k_out, v_out, lat_out, kpe_out), (cos_ref, sin_ref, cost_ref, sint_ref),
                               True)

    def raw_group(k):
        def run():
            raw_scr[:, HG_W * k:HG_W * (k + 1)] = _dot(t["hbp"], whg_ref[:, HG_W * k:HG_W * (k + 1)])
        return run

    chunks = [start] + [raw_group(k) for k in range(4)] + [lambda n=n: t["mla"][n]() for n in range(5)]

    def next_chunk(_level):
        if chunks:
            chunks.pop(0)()

    st = _hgrn_units_math(lambda u, k: raw_scr[UNIT * u:UNIT * (u + 1), HG_W * k:HG_W * (k + 1)], n_units, st,
                          lbl_ref, gain_ref, tin_ref, lv_ref, unperm_ref, hg_out, next_chunk)
    while chunks:
        chunks.pop(0)()
    for hd in range(HG_HEADS):
        st_scr[hd] = st[hd]
        st_out[0, hd] = st[hd]


def _unit_perm(n):
    i = np.arange(n)
    src = UNIT * (i // UNIT) + _unit_time()[i % UNIT]
    return jnp.asarray(src[:, None] == i[None, :], BF16)


def _inproj_args(x, mod, gmix, wts, rope, q_transposed):
    (win, wkx, wkpt, qng, kvng, wuqn, wuqr, wuk, wuv) = wts
    if q_transposed:
        wuqn, wuqr = wuqn.T, wuqr.T
    return [x, mod, gmix, win, wkx, wkpt, qng, kvng, wuqn, wuqr, wuk, wuv.T, *rope]


def _inproj_out_shapes(B, L, q_transposed):
    q_shape = (B, MLA_HEADS, QK_W, L) if q_transposed else (B, MLA_HEADS, L, QK_W)
    return [(q_shape, BF16), ((B, MLA_HEADS, L, QK_W), BF16), ((B, MLA_HEADS * MLA_V, L), BF16),
            ((B, L, KV_LORA), F32), ((B, MLA_ROPE, L), F32)]


def _inproj(x, mod, gmix, wts, rope, tm):
    B, L, D = x.shape
    args = _inproj_args(x, mod, gmix, wts, rope, False)
    row = lambda w: pl.BlockSpec((1, tm, w), lambda b, i: (b, i, 0))
    col = lambda h: pl.BlockSpec((1, h, tm), lambda b, i: (b, 0, i))
    in_specs = [row(D), pl.BlockSpec((1, 6, D), lambda b, i: (b, 0, 0))]
    in_specs += [_const_spec(w.shape) for w in args[2:12]]
    in_specs += [pl.BlockSpec((tm, 128), lambda b, i: (i, 0))] * 2
    in_specs += [pl.BlockSpec((ROPE_HALF, tm), lambda b, i: (0, i))] * 2
    shapes = [((B, L, 4 * HG_W), F32)] + _inproj_out_shapes(B, L, False)
    heads = pl.BlockSpec((1, MLA_HEADS, tm, QK_W), lambda b, i: (b, 0, i, 0))
    out_specs = [row(4 * HG_W), heads, heads, col(MLA_HEADS * MLA_V),
                 row(KV_LORA), col(MLA_ROPE)]
    return pl.pallas_call(
        _inproj_body,
        out_shape=[jax.ShapeDtypeStruct(s, dt) for s, dt in shapes],
        grid=(B, L // tm),
        in_specs=in_specs,
        out_specs=out_specs,
        compiler_params=pltpu.CompilerParams(dimension_semantics=("arbitrary", "arbitrary"),
                                             vmem_limit_bytes=VMEM_LIMIT),
        name="in_proj",
    )(*args)


def _inproj_hgrn(x, mod, gmix, wts, rope, lb_logits, gain, s0, tm):
    B, L, D = x.shape
    nt = L // tm
    n = B * nt
    args = _inproj_args(x, mod, gmix, wts, rope, True)
    tin, lv = _unit_tables()
    perm = _unit_perm(UNIT)
    args += [perm, lb_logits, gain, jnp.swapaxes(s0, 2, 3), tin, lv, perm.T]
    cur = lambda j: jnp.minimum(j, n - 1)
    lag = lambda j: jnp.maximum(j - 1, 0)
    row = lambda w, t: pl.BlockSpec((1, tm, w), lambda j: (t(j) // nt, t(j) % nt, 0))
    col = lambda h: pl.BlockSpec((1, h, tm), lambda j: (cur(j) // nt, 0, cur(j) % nt))
    state = pl.BlockSpec((1, HG_HEADS, HG_DV, HG_DK), lambda j: (lag(j) // nt, 0, 0, 0))
    in_specs = [row(D, cur), pl.BlockSpec((1, 6, D), lambda j: (cur(j) // nt, 0, 0))]
    in_specs += [_const_spec(w.shape) for w in args[2:12]]
    in_specs += [pl.BlockSpec((tm, 128), lambda j: (cur(j) % nt, 0))] * 2
    in_specs += [pl.BlockSpec((ROPE_HALF, tm), lambda j: (0, cur(j) % nt))] * 2
    in_specs += [_const_spec(perm.shape), _const_spec(lb_logits.shape), _const_spec(gain.shape), state,
                 _const_spec(tin.shape), _const_spec(lv.shape), _const_spec(perm.shape)]
    shapes = [((B, L, HG_W), BF16), ((B, HG_HEADS, HG_DV, HG_DK), F32)] + _inproj_out_shapes(B, L, True)
    heads = pl.BlockSpec((1, MLA_HEADS, tm, QK_W), lambda j: (cur(j) // nt, 0, cur(j) % nt, 0))
    heads_t = pl.BlockSpec((1, MLA_HEADS, QK_W, tm), lambda j: (cur(j) // nt, 0, 0, cur(j) % nt))
    out_specs = [row(HG_W, lag), state, heads_t, heads,
                 col(MLA_HEADS * MLA_V), row(KV_LORA, cur), col(MLA_ROPE)]
    ohg, st, *mla = pl.pallas_call(
        functools.partial(_inproj_hgrn_body, nt=nt),
        out_shape=[jax.ShapeDtypeStruct(s, dt) for s, dt in shapes],
        grid=(n + 1,),
        in_specs=in_specs,
        out_specs=out_specs,
        scratch_shapes=[pltpu.VMEM((tm, 4 * HG_W), F32), pltpu.VMEM((HG_HEADS, HG_DV, HG_DK), F32)],
        compiler_params=pltpu.CompilerParams(dimension_semantics=("arbitrary",),
                                             vmem_limit_bytes=VMEM_LIMIT),
        name="in_proj_hgrn",
    )(*args)
    return (ohg, jnp.swapaxes(st, 2, 3), *mla)


def _hgrn_body(raw_ref, lbl_ref, gain_ref, s0_ref, tin_ref, tup_ref, ones_ref,
               o_out, s_out,
               st_scr, b_scr, q_scr, kk_scr, v_scr, eb_scr, qt_scr, kt_scr, vb_scr, p_scr, o_scr, *, th):
    i = pl.program_id(1)

    @pl.when(i == 0)
    def _():
        for h in range(HG_HEADS):
            st_scr[h] = s0_ref[0, h].T

    lbl = lbl_ref[...]
    e = jnp.exp(lbl - jnp.max(lbl, axis=0, keepdims=True))
    lb = (e / jnp.sum(e, axis=0, keepdims=True))[0:1]

    q = raw_ref[0, :, 0:HG_W] * HG_DK ** -0.5
    f = lb + (1.0 - lb) * jax.nn.sigmoid(raw_ref[0, :, HG_W:2 * HG_W])
    kk = 1.0 - f
    v = raw_ref[0, :, 2 * HG_W:3 * HG_W]
    l1, l2, l3 = _split3(jnp.log(f))
    tin = tin_ref[...]
    tup = tup_ref[...]
    b = _dot(tin, l1) + _dot(tin, l2) + _dot(tin, l3)
    c = _dot(tup, l1) + _dot(tup, l2) + _dot(tup, l3)
    eb = jnp.exp(b)
    b_scr[...] = b
    q_scr[...] = q
    kk_scr[...] = kk
    v_scr[...] = v
    eb_scr[...] = eb
    qt_scr[...] = (q * eb).astype(BF16)
    kt_scr[...] = (kk * jnp.exp(c)).astype(BF16)
    vb_scr[...] = v.astype(BF16)

    rowid = lax.broadcasted_iota(jnp.int32, (SUB, HG_W), 0)

    def step(j, carry):
        r0 = pl.multiple_of(j * SUB, SUB)
        bj = b_scr[pl.ds(r0, SUB), :]
        qj = q_scr[pl.ds(r0, SUB), :]
        for s in range(SUB):
            bs = b_scr[pl.ds(r0 + s, 1), :]
            ks = kk_scr[pl.ds(r0 + s, 1), :]
            dec = jnp.exp(jnp.where(rowid >= s, bj - bs, -jnp.inf))
            p_scr[s * SUB:(s + 1) * SUB, :] = (qj * dec * ks).astype(BF16)
        r = _dot(p_scr[...], ones_ref[...])
        od = jnp.zeros((SUB, HG_W), F32)
        for s in range(SUB):
            od = od + r[s * SUB:(s + 1) * SUB, :] * v_scr[pl.ds(r0 + s, 1), :]
        qt = qt_scr[pl.ds(r0, SUB), :]
        kt = kt_scr[pl.ds(r0, SUB), :]
        vb = vb_scr[pl.ds(r0, SUB), :]
        dj = eb_scr[pl.ds(r0 + SUB - 1, 1), :]
        for h in range(HG_HEADS):
            hs = slice(h * HG_DK, (h + 1) * HG_DK)
            st = st_scr[h]
            oi = _dot_nt(qt[:, hs], st.astype(BF16))
            st_scr[h] = st * dj[:, hs] + _dot_tn(vb[:, hs], kt[:, hs])
            o_scr[pl.ds(r0, SUB), hs] = oi + od[:, hs]
        return carry

    lax.fori_loop(0, th // SUB, step, 0)

    gate = raw_ref[0, :, 3 * HG_W:4 * HG_W]
    gate = gate * jax.nn.sigmoid(gate)
    gain = gain_ref[...]
    for h in range(HG_HEADS):
        hs = slice(h * HG_DV, (h + 1) * HG_DV)
        o_out[0, :, hs] = (_rms(o_scr[:, hs], gain) * gate[:, hs]).astype(BF16)

    @pl.when(i == pl.num_programs(1) - 1)
    def _():
        for h in range(HG_HEADS):
            s_out[0, h] = st_scr[h].T


def _step_matrices(th):
    r = np.arange(th)
    same = (r[:, None] // SUB) == (r[None, :] // SUB)
    tin = (same & (r[None, :] <= r[:, None])).astype(np.float32)
    tup = (same & (r[None, :] > r[:, None])).astype(np.float32)
    hd = np.arange(HG_W) // HG_DK
    ones = (hd[:, None] == hd[None, :]).astype(np.float32)
    return jnp.asarray(tin, BF16), jnp.asarray(tup, BF16), jnp.asarray(ones, BF16)


def _hgrn(raw, lb_logits, gain, s0, th):
    B, L, _ = raw.shape
    tin, tup, ones = _step_matrices(th)
    f32s = lambda: pltpu.VMEM((th, HG_W), F32)
    b16s = lambda: pltpu.VMEM((th, HG_W), BF16)
    return pl.pallas_call(
        functools.partial(_hgrn_body, th=th),
        out_shape=[jax.ShapeDtypeStruct((B, L, HG_W), BF16),
                   jax.ShapeDtypeStruct((B, HG_HEADS, HG_DK, HG_DV), F32)],
        grid=(B, L // th),
        in_specs=[pl.BlockSpec((1, th, 4 * HG_W), lambda b, i: (b, i, 0)),
                  _const_spec(lb_logits.shape), _const_spec(gain.shape),
                  pl.BlockSpec((1, HG_HEADS, HG_DK, HG_DV), lambda b, i: (b, 0, 0, 0)),
                  _const_spec(tin.shape), _const_spec(tup.shape), _const_spec(ones.shape)],
        out_specs=[pl.BlockSpec((1, th, HG_W), lambda b, i: (b, i, 0)),
                   pl.BlockSpec((1, HG_HEADS, HG_DK, HG_DV), lambda b, i: (b, 0, 0, 0))],
        scratch_shapes=[pltpu.VMEM((HG_HEADS, HG_DV, HG_DK), F32),
                        f32s(), f32s(), f32s(), f32s(), f32s(),
                        b16s(), b16s(), b16s(),
                        pltpu.VMEM((SUB * SUB, HG_W), BF16), f32s()],
        compiler_params=pltpu.CompilerParams(dimension_semantics=("arbitrary", "arbitrary"),
                                             vmem_limit_bytes=VMEM_LIMIT),
        name="hgrn_scan",
    )(raw, lb_logits, gain, s0, tin, tup, ones)


def _unit_time():
    i = np.arange(UNIT)
    return SLABS * (i % SLAB_ROWS) + i // SLAB_ROWS


def _unit_tables():
    t = _unit_time()
    tin = (t[None, :] <= t[:, None]).astype(np.float32)
    rows = []
    for c in (1, 2, 4, 8):
        g = np.arange(SLAB_ROWS)
        t_bnd = SLABS * (2 * c * (g // (2 * c)) + c - 1) + SLABS - 1
        rows.append((t[None, :] <= t_bnd[:, None]).astype(np.float32))
    tin_ext = np.concatenate([tin] + rows, axis=0)
    x = t[:, None] ^ t[None, :]
    lv = np.where(x > 0, np.floor(np.log2(np.maximum(x, 1))), DIAG_LEVEL).astype(np.int32)
    lv = np.where(t[None, :] <= t[:, None], lv, -1)
    return jnp.asarray(tin_ext, BF16), jnp.asarray(np.concatenate([lv, lv], axis=1), jnp.int32)


def _pair_blockdiag(a, b):
    z = jnp.zeros_like(a)
    return jnp.concatenate([jnp.concatenate([a, z], axis=1), jnp.concatenate([z, b], axis=1)], axis=0)


def _hgrn_units_math(field, nu, st, lbl_ref, gain_ref, tin_ref, lv_ref, unperm_ref, o_out, after_level):
    lbl = lbl_ref[...]
    e = jnp.exp(lbl - jnp.max(lbl, axis=0, keepdims=True))
    lb = (e / jnp.sum(e, axis=0, keepdims=True))[0:1]
    tin = tin_ref[...]
    lv = lv_ref[...]
    gain = gain_ref[...]

    units = range(nu)
    pairs = range(HG_HEADS // 2)

    def pair_cols(x, hp):
        c0 = 2 * HG_DK * hp
        return x[:, c0:c0 + HG_DK], x[:, c0 + HG_DK:c0 + 2 * HG_DK]

    gates = [field(u, 3) for u in units]
    q = [field(u, 0) * HG_DK ** -0.5 for u in units]
    f = [lb + (1.0 - lb) * jax.nn.sigmoid(field(u, 1)) for u in units]
    kk = [1.0 - f[u] for u in units]
    lsplit = [_split2(jnp.log(f[u]) * np.float32(1.0 / np.log(2.0))) for u in units]
    bx = [_dot(tin, lsplit[u][0]) + _dot(tin, lsplit[u][1]) for u in units]
    b = [bx[u][0:UNIT] for u in units]
    bs = [[b[u][SLAB_ROWS * p:SLAB_ROWS * (p + 1)] for p in range(SLABS)] for u in units]
    vb = [field(u, 2).astype(BF16) for u in units]
    qb = [q[u].astype(BF16) for u in units]
    kkb_t = [kk[u].T.astype(BF16) for u in units]

    def slab_decay(u, level, p):
        if level < 3:
            half = 1 << level
            ref = p - p % (2 * half) + half - 1
            if p == ref:
                return jnp.ones_like(bs[u][p])
            return jnp.exp2(bs[u][p] - bs[u][ref] if p > ref else bs[u][ref] - bs[u][p])
        r = bx[u][UNIT + SLAB_ROWS * (level - 3):UNIT + SLAB_ROWS * (level - 2)]
        if level == N_LEVELS - 1:
            h = SLAB_ROWS // 2
            return jnp.concatenate([jnp.exp2(r[:h] - bs[u][p][:h]), jnp.exp2(bs[u][p][h:] - r[h:])], axis=0)
        return jnp.exp2(-jnp.abs(bs[u][p] - r))

    acc = [[jnp.zeros((UNIT, 2 * HG_DK), F32) for _ in pairs] for _ in units]
    for level in range(N_LEVELS + 1):
        here = lv == level
        for u in units:
            if level == DIAG_LEVEL:
                qt, kt_t = qb[u], kkb_t[u]
            else:
                dec = jnp.concatenate([slab_decay(u, level, p) for p in range(SLABS)], axis=0)
                qt, kt_t = qb[u] * dec.astype(BF16), kkb_t[u] * dec.T.astype(BF16)
            for hp in pairs:
                c0 = 2 * HG_DK * hp
                a2 = _dot(qt[:, c0:c0 + 2 * HG_DK], _pair_blockdiag(kt_t[c0:c0 + HG_DK], kt_t[c0 + HG_DK:c0 + 2 * HG_DK]))
                acc[u][hp] = jnp.where(here, a2, acc[u][hp])
        after_level(level)

    b_end = [b[u][UNIT - 1:UNIT] for u in units]
    qi = [(q[u] * jnp.exp2(b[u])).astype(BF16) for u in units]
    ks = [(kk[u] * jnp.exp2(b_end[u] - b[u])).astype(BF16) for u in units]
    d = [jnp.exp2(b_end[u]) for u in units]
    o_intra = [[_dot(acc[u][hp].astype(BF16), _pair_blockdiag(*pair_cols(vb[u], hp))) for hp in pairs]
               for u in units]
    ds = [[_dot_tn(vb[u][:, h * HG_DV:(h + 1) * HG_DV], ks[u][:, h * HG_DK:(h + 1) * HG_DK])
           for h in range(HG_HEADS)] for u in units]
    o_inter = []
    for u in units:
        o_inter.append([_dot_nt(qi[u][:, 2 * HG_DK * hp:2 * HG_DK * (hp + 1)],
                                _pair_blockdiag(st[2 * hp].astype(BF16), st[2 * hp + 1].astype(BF16)))
                        for hp in pairs])
        st = [st[h] * d[u][:, h * HG_DK:(h + 1) * HG_DK] + ds[u][h] for h in range(HG_HEADS)]
    res = []
    for u in units:
        o = jnp.concatenate([o_intra[u][hp] + o_inter[u][hp] for hp in pairs], axis=1)
        gate = gates[u] * jax.nn.sigmoid(gates[u])
        r = jnp.concatenate([_rms(o[:, h * HG_DV:(h + 1) * HG_DV], gain) for h in range(HG_HEADS)], axis=1)
        res.append((r * gate).astype(BF16))
    for u in units:
        o_out[0, UNIT * u:UNIT * (u + 1), :] = _dot(unperm_ref[...], res[u]).astype(BF16)
    return st


def _attn_body(qt_ref, k_ref, vt_ref, o_ref, acc_scr, *, tq, nq):
    for qi in range(nq):
        qt = qt_ref[0, 0, :, qi * tq:(qi + 1) * tq]
        nblk = qi + 1
        scores = lambda i: _dot(k_ref[0, 0, i * tq:(i + 1) * tq, :], qt)
        pending = [scores(i) for i in range(min(ATTN_LOOKAHEAD, nblk))]
        m = jnp.full((1, tq), -jnp.inf, F32)
        l = jnp.zeros((1, tq), F32)
        for ki in range(nblk):
            st = pending.pop(0)
            if ki + ATTN_LOOKAHEAD < nblk:
                pending.append(scores(ki + ATTN_LOOKAHEAD))
            if ki == nblk - 1:
                kc = lax.broadcasted_iota(jnp.int32, st.shape, 0) // CHUNK
                qc = lax.broadcasted_iota(jnp.int32, st.shape, 1) // CHUNK
                st = jnp.where(kc <= qc, st, -jnp.inf)
            m_new = jnp.maximum(m, jnp.max(st, axis=0, keepdims=True))
            p = jnp.exp2(st - m_new)
            alpha = jnp.exp2(m - m_new)
            l = alpha * l + jnp.sum(p, axis=0, keepdims=True)
            pv = _dot(vt_ref[0, :, ki * tq:(ki + 1) * tq], p.astype(BF16))
            if ki == 0:
                acc_scr[qi % 2] = pv
            else:
                acc_scr[qi % 2] = alpha * acc_scr[qi % 2] + pv
            m = m_new
        o_ref[0, 0, qi * tq:(qi + 1) * tq, :] = (acc_scr[qi % 2] / l).T.astype(BF16)


def _attn(qt, k, vt, tq):
    B, _, L, _ = k.shape
    per_head = lambda w: pl.BlockSpec((1, 1, L, w), lambda b, h: (b, h, 0, 0))
    return pl.pallas_call(
        functools.partial(_attn_body, tq=tq, nq=L // tq),
        out_shape=jax.ShapeDtypeStruct((B, MLA_HEADS, L, MLA_V), BF16),
        grid=(B, MLA_HEADS),
        in_specs=[pl.BlockSpec((1, 1, QK_W, L), lambda b, h: (b, h, 0, 0)), per_head(QK_W),
                  pl.BlockSpec((1, MLA_V, L), lambda b, h: (b, h, 0))],
        out_specs=per_head(MLA_V),
        scratch_shapes=[pltpu.VMEM((2, MLA_V, tq), F32)],
        compiler_params=pltpu.CompilerParams(dimension_semantics=("arbitrary",) * 2,
                                             vmem_limit_bytes=VMEM_LIMIT),
        name="mla_attn_prompt",
    )(qt, k, vt)


def _attn_cache_body(q_ref, kn_ref, vn_ref, clat_ref, ckpe_ref, wuk_ref, wuv_ref, place_ref, o_ref):
    n = q_ref.shape[2]
    latc = clat_ref[0].astype(BF16)
    kpec = ckpe_ref[0].astype(BF16)
    qlat, qrope, s_new = [], [], []
    for h in range(MLA_HEADS):
        qh = q_ref[0, h]
        ws = slice(MLA_NOPE * h, MLA_NOPE * (h + 1))
        qlat.append(_dot_nt(qh[:, 0:MLA_NOPE], wuk_ref[:, ws]).astype(BF16))
        qrope.append(_dot_nt(qh[:, MLA_NOPE:QK_W], place_ref[:, ws]).astype(BF16))
        s_new.append(_dot_nt(qh, kn_ref[0, h]))
    s_new = jnp.concatenate(s_new, axis=0)
    s_c = (_dot_nt(jnp.concatenate(qlat, axis=0), latc)
           + _dot(jnp.concatenate(qrope, axis=0), kpec))
    m = jnp.maximum(jnp.max(s_c, axis=-1, keepdims=True), jnp.max(s_new, axis=-1, keepdims=True))
    p_c = jnp.exp2(s_c - m)
    p_new = jnp.exp2(s_new - m)
    l = jnp.sum(p_c, axis=-1, keepdims=True) + jnp.sum(p_new, axis=-1, keepdims=True)
    o_lat = _dot(p_c.astype(BF16), latc).astype(BF16)
    p_new = p_new.astype(BF16)
    for h in range(MLA_HEADS):
        ws = slice(MLA_V * h, MLA_V * (h + 1))
        rs = slice(n * h, n * (h + 1))
        acc = _dot(o_lat[rs], wuv_ref[:, ws]) + _dot_nt(p_new[rs], vn_ref[0, ws, :])
        o_ref[0, h] = (acc / l[rs]).astype(BF16)


def _rope_place():
    p = np.zeros((MLA_ROPE, MLA_HEADS * MLA_NOPE), np.float32)
    for h in range(MLA_HEADS):
        for i in range(ROPE_HALF):
            p[i, MLA_NOPE * h + ROPE_HALF * h + i] = 1.0
            p[ROPE_HALF + i, MLA_NOPE * h + ROPE_HALF * ((h + 1) % MLA_HEADS) + i] = 1.0
    return jnp.asarray(p, BF16)


def _attn_cache(q, kn, vn, clat, ckpe, wuk, wuv):
    B, _, L, _ = q.shape
    P = clat.shape[1]
    place = _rope_place()
    bspec = lambda n, w: pl.BlockSpec((1, n, w), lambda b: (b, 0, 0))
    heads = lambda w: pl.BlockSpec((1, MLA_HEADS, L, w), lambda b: (b, 0, 0, 0))
    return pl.pallas_call(
        _attn_cache_body,
        out_shape=jax.ShapeDtypeStruct((B, MLA_HEADS, L, MLA_V), BF16),
        grid=(B,),
        in_specs=[heads(QK_W), heads(QK_W), bspec(MLA_HEADS * MLA_V, L),
                  bspec(P, KV_LORA), bspec(MLA_ROPE, P),
                  _const_spec(wuk.shape), _const_spec(wuv.shape), _const_spec(place.shape)],
        out_specs=heads(MLA_V),
        compiler_params=pltpu.CompilerParams(dimension_semantics=("arbitrary",),
                                             vmem_limit_bytes=VMEM_LIMIT),
        name="mla_attn_cache",
    )(q, kn, vn, clat, ckpe, wuk, wuv, place)


def _ffn_body(x_ref, ohg_ref, omla_ref, mod_ref, conv0_ref, wout_ref, gffn_ref, wup_ref, cw_ref, cb_ref,
              wdn_ref, gfin_ref, y_ref, conv_out, carry_scr, a_scr, u_scr, *, tm, ft):
    @pl.when(pl.program_id(1) == 0)
    def _():
        carry_scr[...] = conv0_ref[0]

    mod = mod_ref[0]
    g1 = mod[2:3]
    sh2 = mod[3:4]
    sc2 = mod[4:5]
    g2 = mod[5:6]
    omla = jnp.concatenate([omla_ref[0, h] for h in range(MLA_HEADS)], axis=1)
    o = _dot(ohg_ref[0], wout_ref[0:HG_W, :]) + _dot(omla, wout_ref[HG_W:2 * HG_W, :])
    x1 = x_ref[0] + g1 * o
    h2 = (_rms(x1, gffn_ref[...]) * (1.0 + sc2) + sh2).astype(BF16)
    for j in range(D_FF // ft):
        cs = slice(j * ft, (j + 1) * ft)
        a = _dot(h2, wup_ref[:, cs])
        v = _dot(h2, wup_ref[:, D_FF + j * ft:D_FF + (j + 1) * ft])
        a_scr[0:CARRY_ROWS, :] = carry_scr[:, cs]
        a_scr[CARRY_ROWS:CARRY_ROWS + tm, :] = a
        carry_scr[:, cs] = a[tm - CARRY_ROWS:tm]
        cw = cw_ref[:, cs]
        conv = (cb_ref[:, cs] + cw[0:1] * a_scr[CARRY_ROWS - 2:CARRY_ROWS - 2 + tm, :]
                + cw[1:2] * a_scr[CARRY_ROWS - 1:CARRY_ROWS - 1 + tm, :] + cw[2:3] * a)
        g = 0.5 * conv * (1.0 + lax.erf(conv * np.float32(np.sqrt(0.5))))
        u_scr[:, cs] = (g * v).astype(BF16)
        conv_out[0, :, cs] = a[tm - (CONV_W - 1):tm]

    x2 = x1 + g2 * _dot(u_scr[...], wdn_ref[...])
    y_ref[0] = _rms(x2, gfin_ref[...])


def _ffn(x, ohg, omla, mod, conv0, wout, gffn, wup, cw, cb, wdn, gfin, tm, ft):
    B, L, D = x.shape
    row = lambda w: pl.BlockSpec((1, tm, w), lambda b, i: (b, i, 0))
    once = lambda a: pl.BlockSpec(a.shape, lambda b, i: (0,) * a.ndim, pipeline_mode=pl.Buffered(1))
    return pl.pallas_call(
        functools.partial(_ffn_body, tm=tm, ft=ft),
        out_shape=[jax.ShapeDtypeStruct((B, L, D), F32),
                   jax.ShapeDtypeStruct((B, CONV_W - 1, D_FF), F32)],
        grid=(B, L // tm),
        in_specs=[row(D), row(HG_W), pl.BlockSpec((1, MLA_HEADS, tm, MLA_V), lambda b, i: (b, 0, i, 0)),
                  pl.BlockSpec((1, 6, D), lambda b, i: (b, 0, 0)),
                  pl.BlockSpec((1, CARRY_ROWS, D_FF), lambda b, i: (b, 0, 0)),
                  once(wout), once(gffn), once(wup), once(cw), once(cb), once(wdn), once(gfin)],
        out_specs=[row(D), pl.BlockSpec((1, CONV_W - 1, D_FF), lambda b, i: (b, 0, 0))],
        scratch_shapes=[pltpu.VMEM((CARRY_ROWS, D_FF), F32), pltpu.VMEM((CARRY_ROWS + tm, ft), F32),
                        pltpu.VMEM((tm, D_FF), BF16)],
        compiler_params=pltpu.CompilerParams(dimension_semantics=("arbitrary", "arbitrary"),
                                             vmem_limit_bytes=VMEM_LIMIT),
        name="out_ffn",
    )(x, ohg, omla, mod, conv0, wout, gffn, wup, cw, cb, wdn, gfin)


def _prep_weights(w_in, mla_q_norm_gain, mla_kv_norm_gain, w_uq, w_uk, w_uv):
    win = w_in.astype(BF16)
    kp = win[:, 4 * HG_W + Q_LORA + KV_LORA:]
    wkx = jnp.concatenate([jnp.tile(kp[:, :ROPE_HALF], (1, MLA_HEADS)),
                           jnp.tile(kp[:, ROPE_HALF:], (1, MLA_HEADS))], axis=1)
    uq = w_uq.reshape(Q_LORA, MLA_HEADS, MLA_NOPE + MLA_ROPE)
    wuqn = uq[:, :, :MLA_NOPE].reshape(Q_LORA, MLA_HEADS * MLA_NOPE).astype(BF16)
    x1a = uq[:, :, MLA_NOPE:MLA_NOPE + ROPE_HALF].reshape(Q_LORA, MLA_HEADS * ROPE_HALF)
    x2a = uq[:, :, MLA_NOPE + ROPE_HALF:].reshape(Q_LORA, MLA_HEADS * ROPE_HALF)
    wuqr = jnp.concatenate([x1a, x2a, jnp.roll(x1a, ROPE_HALF, axis=1), jnp.roll(x2a, ROPE_HALF, axis=1)],
                           axis=1).astype(BF16)
    wuk = w_uk.reshape(KV_LORA, MLA_HEADS * MLA_NOPE).astype(BF16)
    wuv = w_uv.reshape(KV_LORA, MLA_HEADS * MLA_V).astype(BF16)
    return (win, wkx, kp.T, mla_q_norm_gain.reshape(1, -1), mla_kv_norm_gain.reshape(1, -1), wuqn, wuqr, wuk, wuv)


def _rope_tables(start, n):
    inv_freq = ROPE_THETA ** (-np.arange(ROPE_HALF, dtype=np.float64) / ROPE_HALF)
    ang = (start + np.arange(n, dtype=np.float64))[:, None] * inv_freq[None, :]
    cos, sin = np.cos(ang).astype(np.float32), np.sin(ang).astype(np.float32)
    return tuple(jnp.asarray(a) for a in (np.tile(cos, (1, MLA_HEADS)), np.tile(sin, (1, MLA_HEADS)), cos.T, sin.T))


def _tile(n, pref):
    return pref if n % pref == 0 else n


def kernel(x_prompt, x_sample, c_prompt, c_sample, cache_kv_latent, cache_k_rope, state_hgrn, state_ffn_conv, w_ada, b_ada, norm_mix_gain, w_in, hg_lb_logits, hg_norm_gain, mla_q_norm_gain, mla_kv_norm_gain, w_uq, w_uk, w_uv, w_out, norm_ffn_gain, w_up, conv_w, conv_b, w_down, final_norm_gain):
    assert w_ada.shape[0] == 1, "single-layer trunk"
    B, L, D = x_prompt.shape
    Bs, Ls, _ = x_sample.shape
    past = cache_kv_latent.shape[2]

    mod = _ada(jnp.concatenate([c_prompt, c_sample], axis=0), w_ada[0], b_ada)
    mod = mod.reshape(B + Bs, 6, D)
    wts = _prep_weights(w_in[0], mla_q_norm_gain[0], mla_kv_norm_gain[0], w_uq[0], w_uk[0], w_uv[0])
    gmix = norm_mix_gain
    wout = w_out[0].astype(BF16)
    wup = w_up[0].astype(BF16)
    wdn = w_down[0].astype(BF16)
    gfin = final_norm_gain.reshape(1, D)
    ffn_w = (wout, norm_ffn_gain, wup, conv_w[0], conv_b, wdn, gfin)

    def layer(x, mod_x, pos, s0, conv0, cache):
        n, l, _ = x.shape
        rope = _rope_tables(pos, l)
        tm = _tile(l, 512)
        if tm % UNIT == 0:
            ohg, s_new, q, k, v, lat, kpe = _inproj_hgrn(x, mod_x, gmix, wts, rope, hg_lb_logits, hg_norm_gain,
                                                         s0, tm)
        else:
            raw, q, k, v, lat, kpe = _inproj(x, mod_x, gmix, wts, rope, tm)
            ohg, s_new = _hgrn(raw, hg_lb_logits, hg_norm_gain, s0, _tile(l, 256))
        if cache is None:
            omla = _attn(q, k, v, _tile(l, 512))
        else:
            omla = _attn_cache(q, k, v, cache[0], cache[1], wts[7], wts[8])
        y, conv_new = _ffn(x, ohg, omla, mod_x, conv0, *ffn_w, _tile(l, 512), FF_TILE)
        return y, lat[None], jnp.swapaxes(kpe, 1, 2)[None], s_new[None], conv_new[None]

    zeros_state = jnp.zeros((B, HG_HEADS, HG_DK, HG_DV), F32)
    zeros_conv = jnp.zeros((B, CARRY_ROWS, D_FF), F32)
    conv0_s = jnp.pad(state_ffn_conv[0], ((0, 0), (CARRY_ROWS - (CONV_W - 1), 0), (0, 0)))
    yp, latp, kpep, hgp, cvp = layer(x_prompt, mod[:B], 0, zeros_state, zeros_conv, None)
    ys, lats, kpes, hgs, cvs = layer(x_sample, mod[B:], past, state_hgrn[0], conv0_s,
                                     (cache_kv_latent[0], jnp.swapaxes(cache_k_rope[0], 1, 2)))
    return (yp, ys, latp, kpep, hgp, cvp, lats, kpes, hgs, cvs)
```

```python
import functools

import numpy as np
import jax
import jax.numpy as jnp
from jax import lax
from jax.experimental import pallas as pl
from jax.experimental.pallas import tpu as pltpu

F32 = jnp.float32
BF16 = jnp.bfloat16

D_MODEL = 1024
CHUNK = 64
HG_HEADS = 4
HG_DK = 128
HG_DV = 128
HG_W = HG_HEADS * HG_DK
MLA_HEADS = 4
MLA_NOPE = 128
MLA_ROPE = 64
ROPE_HALF = MLA_ROPE // 2
MLA_V = 128
Q_LORA = 384
KV_LORA = 256
ROPE_THETA = 10000.0
MLA_SCALE = (MLA_NOPE + MLA_ROPE) ** -0.5
Q_SCALE = MLA_SCALE * float(np.log2(np.e))
QK_W = 2 * MLA_NOPE
D_FF = 2816
CONV_W = 3
EPS = 1e-6

SUB = 16
FF_TILE = D_FF
ATTN_LOOKAHEAD = 4
CARRY_ROWS = 8
VMEM_LIMIT = 56 * 1024 * 1024
UNIT = 128
SLABS = 8
SLAB_ROWS = UNIT // SLABS
N_LEVELS = 7
DIAG_LEVEL = N_LEVELS


def _dot(a, b):
    return jnp.dot(a, b, preferred_element_type=F32)


def _dot_nt(a, b):
    return lax.dot_general(a, b, (((1,), (1,)), ((), ())), preferred_element_type=F32)


def _dot_tn(a, b):
    return lax.dot_general(a, b, (((0,), (0,)), ((), ())), preferred_element_type=F32)


def _split3(x):
    x1 = x.astype(BF16)
    r1 = x - x1.astype(F32)
    x2 = r1.astype(BF16)
    r2 = r1 - x2.astype(F32)
    return x1, x2, r2.astype(BF16)


def _split2(x):
    x1 = x.astype(BF16)
    return x1, (x - x1.astype(F32)).astype(BF16)


def _rms(x, g):
    return x * lax.rsqrt(jnp.mean(x * x, axis=-1, keepdims=True) + EPS) * g


def _const_spec(shape):
    n = len(shape)
    return pl.BlockSpec(shape, lambda *_: (0,) * n)


def _ada_body(c_ref, w_ref, b_ref, o_ref):
    c = c_ref[...]
    s = c * jax.nn.sigmoid(c)
    s1, s2, _ = _split3(s)
    w = w_ref[...]
    w1 = w.astype(BF16)
    w2 = (w - w1.astype(F32)).astype(BF16)
    o_ref[...] = _dot(s1, w1) + _dot(s1, w2) + _dot(s2, w1) + b_ref[...]


def _ada(c, w_ada, b_ada):
    n, d = c.shape
    nout = w_ada.shape[1]
    tn = 2048
    return pl.pallas_call(
        _ada_body,
        out_shape=jax.ShapeDtypeStruct((n, nout), F32),
        grid=(nout // tn,),
        in_specs=[pl.BlockSpec((n, d), lambda j: (0, 0)),
                  pl.BlockSpec((d, tn), lambda j: (0, j)),
                  pl.BlockSpec((1, tn), lambda j: (0, j))],
        out_specs=pl.BlockSpec((n, tn), lambda j: (0, j)),
        compiler_params=pltpu.CompilerParams(dimension_semantics=("arbitrary",)),
        name="adaln_mod",
    )(c, w_ada, b_ada)


def _rope_slots(r1, r2, nope, out_ref, slot):
    for h in range(MLA_HEADS):
        xh = jnp.where(slot == h, r1, jnp.where(slot == (h + 1) % MLA_HEADS, r2, 0.0))
        out_ref[0, h, :, 0:MLA_NOPE] = nope[:, MLA_NOPE * h:MLA_NOPE * (h + 1)].astype(BF16)
        out_ref[0, h, :, MLA_NOPE:QK_W] = xh.astype(BF16)


def _mla_stages(hb, refs, outs, rope_refs, q_transposed):
    (wcq_ref, wckv_ref, wkx_ref, wkpt_ref, qng_ref, kvng_ref, wuqn_ref, wuqr_ref, wuk_ref, wuvt_ref) = refs
    q_out, k_out, v_out, lat_out, kpe_out = outs
    cos_ref, sin_ref, cost_ref, sint_ref = rope_refs
    t = {}

    def project():
        t["cq"] = _dot(hb, wcq_ref[...])
        t["ckv"] = _dot(hb, wckv_ref[...])
        t["kx"] = _dot(hb, wkx_ref[...])
        if not q_transposed:
            t["kxt"] = _dot_nt(wkpt_ref[...], hb)

    def q_up():
        cq = _rms(t["cq"], qng_ref[...])
        if q_transposed:
            cq = cq.T.astype(BF16)
            t["qn"] = _dot(wuqn_ref[...], cq) * Q_SCALE
            t["qr"] = _dot(wuqr_ref[...], cq) * Q_SCALE
        else:
            cq = cq.astype(BF16)
            t["qn"] = _dot(cq, wuqn_ref[...]) * Q_SCALE
            t["qr"] = _dot(cq, wuqr_ref[...]) * Q_SCALE

    def kv_up():
        lat = _rms(t["ckv"], kvng_ref[...])
        lat_out[0] = lat
        latb = lat.astype(BF16)
        t["kn"] = _dot(latb, wuk_ref[...])
        if q_transposed:
            v_out[0] = _dot(wuvt_ref[...], lat.T.astype(BF16)).astype(BF16)
        else:
            v_out[0] = _dot_nt(wuvt_ref[...], latb).astype(BF16)

    def slot_of(cos):
        return lax.broadcasted_iota(jnp.int32, cos.shape, 1) // ROPE_HALF

    def q_rope():
        qr, qn = t["qr"], t["qn"]
        if q_transposed:
            cos = jnp.concatenate([cost_ref[...]] * MLA_HEADS, axis=0)
            sin = jnp.concatenate([sint_ref[...]] * MLA_HEADS, axis=0)
            qr1 = qr[0:128] * cos - qr[128:256] * sin
            qr2 = qr[384:512] * cos + qr[256:384] * sin
            slot = lax.broadcasted_iota(jnp.int32, cos.shape, 0) // ROPE_HALF
            for h in range(MLA_HEADS):
                xh = jnp.where(slot == h, qr1, jnp.where(slot == (h + 1) % MLA_HEADS, qr2, 0.0))
                q_out[0, h, 0:MLA_NOPE, :] = qn[MLA_NOPE * h:MLA_NOPE * (h + 1)].astype(BF16)
                q_out[0, h, MLA_NOPE:QK_W, :] = xh.astype(BF16)
            return
        cos, sin = cos_ref[...], sin_ref[...]
        qr1 = qr[:, 0:128] * cos - qr[:, 128:256] * sin
        qr2 = qr[:, 384:512] * cos + qr[:, 256:384] * sin
        _rope_slots(qr1, qr2, qn, q_out, slot_of(cos))

    def k_rope():
        cos, sin, kx = cos_ref[...], sin_ref[...], t["kx"]
        kr1 = kx[:, 0:128] * cos - kx[:, 128:256] * sin
        kr2 = kx[:, 128:256] * cos + kx[:, 0:128] * sin
        _rope_slots(kr1, kr2, t["kn"], k_out, slot_of(cos))
        if q_transposed:
            lane = lax.broadcasted_iota(jnp.int32, cos.shape, 1)
            kpe_out[0] = jnp.where(lane < ROPE_HALF, kr1, kr2).T[0:MLA_ROPE]
            return
        kxt = t["kxt"]
        x1t, x2t = kxt[0:ROPE_HALF], kxt[ROPE_HALF:MLA_ROPE]
        cost, sint = cost_ref[...], sint_ref[...]
        kpe_out[0] = jnp.concatenate([x1t * cost - x2t * sint, x2t * cost + x1t * sint], axis=0)

    return [project, q_up, kv_up, q_rope, k_rope]


def _normed_input(x_ref, mod_ref, gmix_ref):
    mod = mod_ref[0]
    return (_rms(x_ref[0], gmix_ref[...]) * (1.0 + mod[1:2]) + mod[0:1]).astype(BF16)


def _win_views(win_ref):
    hgw = 4 * HG_W
    return (win_ref.at[:, 0:hgw], win_ref.at[:, hgw:hgw + Q_LORA],
            win_ref.at[:, hgw + Q_LORA:hgw + Q_LORA + KV_LORA])


def _inproj_body(x_ref, mod_ref, gmix_ref, win_ref, wkx_ref, wkpt_ref, qng_ref, kvng_ref,
                 wuqn_ref, wuqr_ref, wuk_ref, wuvt_ref, cos_ref, sin_ref, cost_ref, sint_ref,
                 hg_out, q_out, k_out, v_out, lat_out, kpe_out):
    whg_ref, wcq_ref, wckv_ref = _win_views(win_ref)
    hb = _normed_input(x_ref, mod_ref, gmix_ref)
    hg_out[0] = _dot(hb, whg_ref[...])
    for stage in _mla_stages(hb, (wcq_ref, wckv_ref, wkx_ref, wkpt_ref, qng_ref, kvng_ref, wuqn_ref, wuqr_ref,
                                  wuk_ref, wuvt_ref),
                             (q_out, k_out, v_out, lat_out, kpe_out), (cos_ref, sin_ref, cost_ref, sint_ref),
                             False):
        stage()


def _inproj_hgrn_body(x_ref, mod_ref, gmix_ref, win_ref, wkx_ref, wkpt_ref, qng_ref, kvng_ref,
                      wuqn_ref, wuqr_ref, wuk_ref, wuvt_ref, cos_ref, sin_ref, cost_ref, sint_ref,
                      perm_ref, lbl_ref, gain_ref, s0t_ref, tin_ref, lv_ref, unperm_ref,
                      hg_out, st_out, q_out, k_out, v_out, lat_out, kpe_out, raw_scr, st_scr, *, nt):
    j = pl.program_id(0)

    @pl.when(j == 0)
    def _():
        raw_scr[...] = jnp.zeros(raw_scr.shape, F32)

    first_of_batch = lax.rem(jnp.maximum(j - 1, 0), nt) == 0
    st = [jnp.where(first_of_batch, s0t_ref[0, hd], st_scr[hd]) for hd in range(HG_HEADS)]

    whg_ref, wcq_ref, wckv_ref = _win_views(win_ref)
    n_units = x_ref.shape[1] // UNIT
    t = {}

    def start():
        hb = _normed_input(x_ref, mod_ref, gmix_ref)
        t["hbp"] = jnp.concatenate([_dot(perm_ref[...], hb[UNIT * u:UNIT * (u + 1)]).astype(BF16)
                                    for u in range(n_units)], axis=0)
        t["mla"] = _mla_stages(hb, (wcq_ref, wckv_ref, wkx_ref, wkpt_ref, qng_ref, kvng_ref, wuqn_ref,
                                    wuqr_ref, wuk_ref, wuvt_ref),
                               (q_out, k_out, v_out, lat_out, kpe_out), (cos_ref, sin_ref, cost_ref, sint_ref),
                               True)

    def raw_group(k):
        def run():
            raw_scr[:, HG_W * k:HG_W * (k + 1)] = _dot(t["hbp"], whg_ref[:, HG_W * k:HG_W * (k + 1)])
        return run

    chunks = [start] + [raw_group(k) for k in range(4)] + [lambda n=n: t["mla"][n]() for n in range(5)]

    def next_chunk(_level):
        if chunks:
            chunks.pop(0)()

    st = _hgrn_units_math(lambda u, k: raw_scr[UNIT * u:UNIT * (u + 1), HG_W * k:HG_W * (k + 1)], n_units, st,
                          lbl_ref, gain_ref, tin_ref, lv_ref, unperm_ref, hg_out, next_chunk)
    while chunks:
        chunks.pop(0)()
    for hd in range(HG_HEADS):
        st_scr[hd] = st[hd]
        st_out[0, hd] = st[hd]


def _unit_perm(n):
    i = np.arange(n)
    src = UNIT * (i // UNIT) + _unit_time()[i % UNIT]
    return jnp.asarray(src[:, None] == i[None, :], BF16)


def _inproj_args(x, mod, gmix, wts, rope, q_transposed):
    (win, wkx, wkpt, qng, kvng, wuqn, wuqr, wuk, wuv) = wts
    if q_transposed:
        wuqn, wuqr = wuqn.T, wuqr.T
    return [x, mod, gmix, win, wkx, wkpt, qng, kvng, wuqn, wuqr, wuk, wuv.T, *rope]


def _inproj_out_shapes(B, L, q_transposed):
    q_shape = (B, MLA_HEADS, QK_W, L) if q_transposed else (B, MLA_HEADS, L, QK_W)
    return [(q_shape, BF16), ((B, MLA_HEADS, L, QK_W), BF16), ((B, MLA_HEADS * MLA_V, L), BF16),
            ((B, L, KV_LORA), F32), ((B, MLA_ROPE, L), F32)]


def _inproj(x, mod, gmix, wts, rope, tm):
    B, L, D = x.shape
    args = _inproj_args(x, mod, gmix, wts, rope, False)
    row = lambda w: pl.BlockSpec((1, tm, w), lambda b, i: (b, i, 0))
    col = lambda h: pl.BlockSpec((1, h, tm), lambda b, i: (b, 0, i))
    in_specs = [row(D), pl.BlockSpec((1, 6, D), lambda b, i: (b, 0, 0))]
    in_specs += [_const_spec(w.shape) for w in args[2:12]]
    in_specs += [pl.BlockSpec((tm, 128), lambda b, i: (i, 0))] * 2
    in_specs += [pl.BlockSpec((ROPE_HALF, tm), lambda b, i: (0, i))] * 2
    shapes = [((B, L, 4 * HG_W), F32)] + _inproj_out_shapes(B, L, False)
    heads = pl.BlockSpec((1, MLA_HEADS, tm, QK_W), lambda b, i: (b, 0, i, 0))
    out_specs = [row(4 * HG_W), heads, heads, col(MLA_HEADS * MLA_V),
                 row(KV_LORA), col(MLA_ROPE)]
    return pl.pallas_call(
        _inproj_body,
        out_shape=[jax.ShapeDtypeStruct(s, dt) for s, dt in shapes],
        grid=(B, L // tm),
        in_specs=in_specs,
        out_specs=out_specs,
        compiler_params=pltpu.CompilerParams(dimension_semantics=("arbitrary", "arbitrary"),
                                             vmem_limit_bytes=VMEM_LIMIT),
        name="in_proj",
    )(*args)


def _inproj_hgrn(x, mod, gmix, wts, rope, lb_logits, gain, s0, tm):
    B, L, D = x.shape
    nt = L // tm
    n = B * nt
    args = _inproj_args(x, mod, gmix, wts, rope, True)
    tin, lv = _unit_tables()
    perm = _unit_perm(UNIT)
    args += [perm, lb_logits, gain, jnp.swapaxes(s0, 2, 3), tin, lv, perm.T]
    cur = lambda j: jnp.minimum(j, n - 1)
    lag = lambda j: jnp.maximum(j - 1, 0)
    row = lambda w, t: pl.BlockSpec((1, tm, w), lambda j: (t(j) // nt, t(j) % nt, 0))
    col = lambda h: pl.BlockSpec((1, h, tm), lambda j: (cur(j) // nt, 0, cur(j) % nt))
    state = pl.BlockSpec((1, HG_HEADS, HG_DV, HG_DK), lambda j: (lag(j) // nt, 0, 0, 0))
    in_specs = [row(D, cur), pl.BlockSpec((1, 6, D), lambda j: (cur(j) // nt, 0, 0))]
    in_specs += [_const_spec(w.shape) for w in args[2:12]]
    in_specs += [pl.BlockSpec((tm, 128), lambda j: (cur(j) % nt, 0))] * 2
    in_specs += [pl.BlockSpec((ROPE_HALF, tm), lambda j: (0, cur(j) % nt))] * 2
    in_specs += [_const_spec(perm.shape), _const_spec(lb_logits.shape), _const_spec(gain.shape), state,
                 _const_spec(tin.shape), _const_spec(lv.shape), _const_spec(perm.shape)]
    shapes = [((B, L, HG_W), BF16), ((B, HG_HEADS, HG_DV, HG_DK), F32)] + _inproj_out_shapes(B, L, True)
    heads = pl.BlockSpec((1, MLA_HEADS, tm, QK_W), lambda j: (cur(j) // nt, 0, cur(j) % nt, 0))
    heads_t = pl.BlockSpec((1, MLA_HEADS, QK_W, tm), lambda j: (cur(j) // nt, 0, 0, cur(j) % nt))
    out_specs = [row(HG_W, lag), state, heads_t, heads,
                 col(MLA_HEADS * MLA_V), row(KV_LORA, cur), col(MLA_ROPE)]
    ohg, st, *mla = pl.pallas_call(
        functools.partial(_inproj_hgrn_body, nt=nt),
        out_shape=[jax.ShapeDtypeStruct(s, dt) for s, dt in shapes],
        grid=(n + 1,),
        in_specs=in_specs,
        out_specs=out_specs,
        scratch_shapes=[pltpu.VMEM((tm, 4 * HG_W), F32), pltpu.VMEM((HG_HEADS, HG_DV, HG_DK), F32)],
        compiler_params=pltpu.CompilerParams(dimension_semantics=("arbitrary",),
                                             vmem_limit_bytes=VMEM_LIMIT),
        name="in_proj_hgrn",
    )(*args)
    return (ohg, jnp.swapaxes(st, 2, 3), *mla)


def _hgrn_body(raw_ref, lbl_ref, gain_ref, s0_ref, tin_ref, tup_ref, ones_ref,
               o_out, s_out,
               st_scr, b_scr, q_scr, kk_scr, v_scr, eb_scr, qt_scr, kt_scr, vb_scr, p_scr, o_scr, *, th):
    i = pl.program_id(1)

    @pl.when(i == 0)
    def _():
        for h in range(HG_HEADS):
            st_scr[h] = s0_ref[0, h].T

    lbl = lbl_ref[...]
    e = jnp.exp(lbl - jnp.max(lbl, axis=0, keepdims=True))
    lb = (e / jnp.sum(e, axis=0, keepdims=True))[0:1]

    q = raw_ref[0, :, 0:HG_W] * HG_DK ** -0.5
    f = lb + (1.0 - lb) * jax.nn.sigmoid(raw_ref[0, :, HG_W:2 * HG_W])
    kk = 1.0 - f
    v = raw_ref[0, :, 2 * HG_W:3 * HG_W]
    l1, l2, l3 = _split3(jnp.log(f))
    tin = tin_ref[...]
    tup = tup_ref[...]
    b = _dot(tin, l1) + _dot(tin, l2) + _dot(tin, l3)
    c = _dot(tup, l1) + _dot(tup, l2) + _dot(tup, l3)
    eb = jnp.exp(b)
    b_scr[...] = b
    q_scr[...] = q
    kk_scr[...] = kk
    v_scr[...] = v
    eb_scr[...] = eb
    qt_scr[...] = (q * eb).astype(BF16)
    kt_scr[...] = (kk * jnp.exp(c)).astype(BF16)
    vb_scr[...] = v.astype(BF16)

    rowid = lax.broadcasted_iota(jnp.int32, (SUB, HG_W), 0)

    def step(j, carry):
        r0 = pl.multiple_of(j * SUB, SUB)
        bj = b_scr[pl.ds(r0, SUB), :]
        qj = q_scr[pl.ds(r0, SUB), :]
        for s in range(SUB):
            bs = b_scr[pl.ds(r0 + s, 1), :]
            ks = kk_scr[pl.ds(r0 + s, 1), :]
            dec = jnp.exp(jnp.where(rowid >= s, bj - bs, -jnp.inf))
            p_scr[s * SUB:(s + 1) * SUB, :] = (qj * dec * ks).astype(BF16)
        r = _dot(p_scr[...], ones_ref[...])
        od = jnp.zeros((SUB, HG_W), F32)
        for s in range(SUB):
            od = od + r[s * SUB:(s + 1) * SUB, :] * v_scr[pl.ds(r0 + s, 1), :]
        qt = qt_scr[pl.ds(r0, SUB), :]
        kt = kt_scr[pl.ds(r0, SUB), :]
        vb = vb_scr[pl.ds(r0, SUB), :]
        dj = eb_scr[pl.ds(r0 + SUB - 1, 1), :]
        for h in range(HG_HEADS):
            hs = slice(h * HG_DK, (h + 1) * HG_DK)
            st = st_scr[h]
            oi = _dot_nt(qt[:, hs], st.astype(BF16))
            st_scr[h] = st * dj[:, hs] + _dot_tn(vb[:, hs], kt[:, hs])
            o_scr[pl.ds(r0, SUB), hs] = oi + od[:, hs]
        return carry

    lax.fori_loop(0, th // SUB, step, 0)

    gate = raw_ref[0, :, 3 * HG_W:4 * HG_W]
    gate = gate * jax.nn.sigmoid(gate)
    gain = gain_ref[...]
    for h in range(HG_HEADS):
        hs = slice(h * HG_DV, (h + 1) * HG_DV)
        o_out[0, :, hs] = (_rms(o_scr[:, hs], gain) * gate[:, hs]).astype(BF16)

    @pl.when(i == pl.num_programs(1) - 1)
    def _():
        for h in range(HG_HEADS):
            s_out[0, h] = st_scr[h].T


def _step_matrices(th):
    r = np.arange(th)
    same = (r[:, None] // SUB) == (r[None, :] // SUB)
    tin = (same & (r[None, :] <= r[:, None])).astype(np.float32)
    tup = (same & (r[None, :] > r[:, None])).astype(np.float32)
    hd = np.arange(HG_W) // HG_DK
    ones = (hd[:, None] == hd[None, :]).astype(np.float32)
    return jnp.asarray(tin, BF16), jnp.asarray(tup, BF16), jnp.asarray(ones, BF16)


def _hgrn(raw, lb_logits, gain, s0, th):
    B, L, _ = raw.shape
    tin, tup, ones = _step_matrices(th)
    f32s = lambda: pltpu.VMEM((th, HG_W), F32)
    b16s = lambda: pltpu.VMEM((th, HG_W), BF16)
    return pl.pallas_call(
        functools.partial(_hgrn_body, th=th),
        out_shape=[jax.ShapeDtypeStruct((B, L, HG_W), BF16),
                   jax.ShapeDtypeStruct((B, HG_HEADS, HG_DK, HG_DV), F32)],
        grid=(B, L // th),
        in_specs=[pl.BlockSpec((1, th, 4 * HG_W), lambda b, i: (b, i, 0)),
                  _const_spec(lb_logits.shape), _const_spec(gain.shape),
                  pl.BlockSpec((1, HG_HEADS, HG_DK, HG_DV), lambda b, i: (b, 0, 0, 0)),
                  _const_spec(tin.shape), _const_spec(tup.shape), _const_spec(ones.shape)],
        out_specs=[pl.BlockSpec((1, th, HG_W), lambda b, i: (b, i, 0)),
                   pl.BlockSpec((1, HG_HEADS, HG_DK, HG_DV), lambda b, i: (b, 0, 0, 0))],
        scratch_shapes=[pltpu.VMEM((HG_HEADS, HG_DV, HG_DK), F32),
                        f32s(), f32s(), f32s(), f32s(), f32s(),
                        b16s(), b16s(), b16s(),
                        pltpu.VMEM((SUB * SUB, HG_W), BF16), f32s()],
        compiler_params=pltpu.CompilerParams(dimension_semantics=("arbitrary", "arbitrary"),
                                             vmem_limit_bytes=VMEM_LIMIT),
        name="hgrn_scan",
    )(raw, lb_logits, gain, s0, tin, tup, ones)


def _unit_time():
    i = np.arange(UNIT)
    return SLABS * (i % SLAB_ROWS) + i // SLAB_ROWS


def _unit_tables():
    t = _unit_time()
    tin = (t[None, :] <= t[:, None]).astype(np.float32)
    rows = []
    for c in (1, 2, 4, 8):
        g = np.arange(SLAB_ROWS)
        t_bnd = SLABS * (2 * c * (g // (2 * c)) + c - 1) + SLABS - 1
        rows.append((t[None, :] <= t_bnd[:, None]).astype(np.float32))
    tin_ext = np.concatenate([tin] + rows, axis=0)
    x = t[:, None] ^ t[None, :]
    lv = np.where(x > 0, np.floor(np.log2(np.maximum(x, 1))), DIAG_LEVEL).astype(np.int32)
    lv = np.where(t[None, :] <= t[:, None], lv, -1)
    return jnp.asarray(tin_ext, BF16), jnp.asarray(np.concatenate([lv, lv], axis=1), jnp.int32)


def _pair_blockdiag(a, b):
    z = jnp.zeros_like(a)
    return jnp.concatenate([jnp.concatenate([a, z], axis=1), jnp.concatenate([z, b], axis=1)], axis=0)


def _hgrn_units_math(field, nu, st, lbl_ref, gain_ref, tin_ref, lv_ref, unperm_ref, o_out, after_level):
    lbl = lbl_ref[...]
    e = jnp.exp(lbl - jnp.max(lbl, axis=0, keepdims=True))
    lb = (e / jnp.sum(e, axis=0, keepdims=True))[0:1]
    tin = tin_ref[...]
    lv = lv_ref[...]
    gain = gain_ref[...]

    units = range(nu)
    pairs = range(HG_HEADS // 2)

    def pair_cols(x, hp):
        c0 = 2 * HG_DK * hp
        return x[:, c0:c0 + HG_DK], x[:, c0 + HG_DK:c0 + 2 * HG_DK]

    gates = [field(u, 3) for u in units]
    q = [field(u, 0) * HG_DK ** -0.5 for u in units]
    f = [lb + (1.0 - lb) * jax.nn.sigmoid(field(u, 1)) for u in units]
    kk = [1.0 - f[u] for u in units]
    lsplit = [_split2(jnp.log(f[u]) * np.float32(1.0 / np.log(2.0))) for u in units]
    bx = [_dot(tin, lsplit[u][0]) + _dot(tin, lsplit[u][1]) for u in units]
    b = [bx[u][0:UNIT] for u in units]
    bs = [[b[u][SLAB_ROWS * p:SLAB_ROWS * (p + 1)] for p in range(SLABS)] for u in units]
    vb = [field(u, 2).astype(BF16) for u in units]
    qb = [q[u].astype(BF16) for u in units]
    kkb_t = [kk[u].T.astype(BF16) for u in units]

    def slab_decay(u, level, p):
        if level < 3:
            half = 1 << level
            ref = p - p % (2 * half) + half - 1
            if p == ref:
                return jnp.ones_like(bs[u][p])
            return jnp.exp2(bs[u][p] - bs[u][ref] if p > ref else bs[u][ref] - bs[u][p])
        r = bx[u][UNIT + SLAB_ROWS * (level - 3):UNIT + SLAB_ROWS * (level - 2)]
        if level == N_LEVELS - 1:
            h = SLAB_ROWS // 2
            return jnp.concatenate([jnp.exp2(r[:h] - bs[u][p][:h]), jnp.exp2(bs[u][p][h:] - r[h:])], axis=0)
        return jnp.exp2(-jnp.abs(bs[u][p] - r))

    acc = [[jnp.zeros((UNIT, 2 * HG_DK), F32) for _ in pairs] for _ in units]
    for level in range(N_LEVELS + 1):
        here = lv == level
        for u in units:
            if level == DIAG_LEVEL:
                qt, kt_t = qb[u], kkb_t[u]
            else:
                dec = jnp.concatenate([slab_decay(u, level, p) for p in range(SLABS)], axis=0)
                qt, kt_t = qb[u] * dec.astype(BF16), kkb_t[u] * dec.T.astype(BF16)
            for hp in pairs:
                c0 = 2 * HG_DK * hp
                a2 = _dot(qt[:, c0:c0 + 2 * HG_DK], _pair_blockdiag(kt_t[c0:c0 + HG_DK], kt_t[c0 + HG_DK:c0 + 2 * HG_DK]))
                acc[u][hp] = jnp.where(here, a2, acc[u][hp])
        after_level(level)

    b_end = [b[u][UNIT - 1:UNIT] for u in units]
    qi = [(q[u] * jnp.exp2(b[u])).astype(BF16) for u in units]
    ks = [(kk[u] * jnp.exp2(b_end[u] - b[u])).astype(BF16) for u in units]
    d = [jnp.exp2(b_end[u]) for u in units]
    o_intra = [[_dot(acc[u][hp].astype(BF16), _pair_blockdiag(*pair_cols(vb[u], hp))) for hp in pairs]
               for u in units]
    ds = [[_dot_tn(vb[u][:, h * HG_DV:(h + 1) * HG_DV], ks[u][:, h * HG_DK:(h + 1) * HG_DK])
           for h in range(HG_HEADS)] for u in units]
    o_inter = []
    for u in units:
        o_inter.append([_dot_nt(qi[u][:, 2 * HG_DK * hp:2 * HG_DK * (hp + 1)],
                                _pair_blockdiag(st[2 * hp].astype(BF16), st[2 * hp + 1].astype(BF16)))
                        for hp in pairs])
        st = [st[h] * d[u][:, h * HG_DK:(h + 1) * HG_DK] + ds[u][h] for h in range(HG_HEADS)]
    res = []
    for u in units:
        o = jnp.concatenate([o_intra[u][hp] + o_inter[u][hp] for hp in pairs], axis=1)
        gate = gates[u] * jax.nn.sigmoid(gates[u])
        r = jnp.concatenate([_rms(o[:, h * HG_DV:(h + 1) * HG_DV], gain) for h in range(HG_HEADS)], axis=1)
        res.append((r * gate).astype(BF16))
    for u in units:
        o_out[0, UNIT * u:UNIT * (u + 1), :] = _dot(unperm_ref[...], res[u]).astype(BF16)
    return st


def _attn_body(qt_ref, k_ref, vt_ref, o_ref, acc_scr, *, tq, nq):
    for qi in range(nq):
        qt = qt_ref[0, 0, :, qi * tq:(qi + 1) * tq]
        nblk = qi + 1
        scores = lambda i: _dot(k_ref[0, 0, i * tq:(i + 1) * tq, :], qt)
        pending = [scores(i) for i in range(min(ATTN_LOOKAHEAD, nblk))]
        m = jnp.full((1, tq), -jnp.inf, F32)
        l = jnp.zeros((1, tq), F32)
        for ki in range(nblk):
            st = pending.pop(0)
            if ki + ATTN_LOOKAHEAD < nblk:
                pending.append(scores(ki + ATTN_LOOKAHEAD))
            if ki == nblk - 1:
                kc = lax.broadcasted_iota(jnp.int32, st.shape, 0) // CHUNK
                qc = lax.broadcasted_iota(jnp.int32, st.shape, 1) // CHUNK
                st = jnp.where(kc <= qc, st, -jnp.inf)
            m_new = jnp.maximum(m, jnp.max(st, axis=0, keepdims=True))
            p = jnp.exp2(st - m_new)
            alpha = jnp.exp2(m - m_new)
            l = alpha * l + jnp.sum(p, axis=0, keepdims=True)
            pv = _dot(vt_ref[0, :, ki * tq:(ki + 1) * tq], p.astype(BF16))
            if ki == 0:
                acc_scr[qi % 2] = pv
            else:
                acc_scr[qi % 2] = alpha * acc_scr[qi % 2] + pv
            m = m_new
        o_ref[0, 0, qi * tq:(qi + 1) * tq, :] = (acc_scr[qi % 2] / l).T.astype(BF16)


def _attn(qt, k, vt, tq):
    B, _, L, _ = k.shape
    per_head = lambda w: pl.BlockSpec((1, 1, L, w), lambda b, h: (b, h, 0, 0))
    return pl.pallas_call(
        functools.partial(_attn_body, tq=tq, nq=L // tq),
        out_shape=jax.ShapeDtypeStruct((B, MLA_HEADS, L, MLA_V), BF16),
        grid=(B, MLA_HEADS),
        in_specs=[pl.BlockSpec((1, 1, QK_W, L), lambda b, h: (b, h, 0, 0)), per_head(QK_W),
                  pl.BlockSpec((1, MLA_V, L), lambda b, h: (b, h, 0))],
        out_specs=per_head(MLA_V),
        scratch_shapes=[pltpu.VMEM((2, MLA_V, tq), F32)],
        compiler_params=pltpu.CompilerParams(dimension_semantics=("arbitrary",) * 2,
                                             vmem_limit_bytes=VMEM_LIMIT),
        name="mla_attn_prompt",
    )(qt, k, vt)


def _attn_cache_body(q_ref, kn_ref, vn_ref, clat_ref, ckpe_ref, wuk_ref, wuv_ref, place_ref, o_ref):
    n = q_ref.shape[2]
    latc = clat_ref[0].astype(BF16)
    kpec = ckpe_ref[0].astype(BF16)
    qlat, qrope, s_new = [], [], []
    for h in range(MLA_HEADS):
        qh = q_ref[0, h]
        ws = slice(MLA_NOPE * h, MLA_NOPE * (h + 1))
        qlat.append(_dot_nt(qh[:, 0:MLA_NOPE], wuk_ref[:, ws]).astype(BF16))
        qrope.append(_dot_nt(qh[:, MLA_NOPE:QK_W], place_ref[:, ws]).astype(BF16))
        s_new.append(_dot_nt(qh, kn_ref[0, h]))
    s_new = jnp.concatenate(s_new, axis=0)
    s_c = (_dot_nt(jnp.concatenate(qlat, axis=0), latc)
           + _dot(jnp.concatenate(qrope, axis=0), kpec))
    m = jnp.maximum(jnp.max(s_c, axis=-1, keepdims=True), jnp.max(s_new, axis=-1, keepdims=True))
    p_c = jnp.exp2(s_c - m)
    p_new = jnp.exp2(s_new - m)
    l = jnp.sum(p_c, axis=-1, keepdims=True) + jnp.sum(p_new, axis=-1, keepdims=True)
    o_lat = _dot(p_c.astype(BF16), latc).astype(BF16)
    p_new = p_new.astype(BF16)
    for h in range(MLA_HEADS):
        ws = slice(MLA_V * h, MLA_V * (h + 1))
        rs = slice(n * h, n * (h + 1))
        acc = _dot(o_lat[rs], wuv_ref[:, ws]) + _dot_nt(p_new[rs], vn_ref[0, ws, :])
        o_ref[0, h] = (acc / l[rs]).astype(BF16)


def _rope_place():
    p = np.zeros((MLA_ROPE, MLA_HEADS * MLA_NOPE), np.float32)
    for h in range(MLA_HEADS):
        for i in range(ROPE_HALF):
            p[i, MLA_NOPE * h + ROPE_HALF * h + i] = 1.0
            p[ROPE_HALF + i, MLA_NOPE * h + ROPE_HALF * ((h + 1) % MLA_HEADS) + i] = 1.0
    return jnp.asarray(p, BF16)


def _attn_cache(q, kn, vn, clat, ckpe, wuk, wuv):
    B, _, L, _ = q.shape
    P = clat.shape[1]
    place = _rope_place()
    bspec = lambda n, w: pl.BlockSpec((1, n, w), lambda b: (b, 0, 0))
    heads = lambda w: pl.BlockSpec((1, MLA_HEADS, L, w), lambda b: (b, 0, 0, 0))
    return pl.pallas_call(
        _attn_cache_body,
        out_shape=jax.ShapeDtypeStruct((B, MLA_HEADS, L, MLA_V), BF16),
        grid=(B,),
        in_specs=[heads(QK_W), heads(QK_W), bspec(MLA_HEADS * MLA_V, L),
                  bspec(P, KV_LORA), bspec(MLA_ROPE, P),
                  _const_spec(wuk.shape), _const_spec(wuv.shape), _const_spec(place.shape)],
        out_specs=heads(MLA_V),
        compiler_params=pltpu.CompilerParams(dimension_semantics=("arbitrary",),
                                             vmem_limit_bytes=VMEM_LIMIT),
        name="mla_attn_cache",
    )(q, kn, vn, clat, ckpe, wuk, wuv, place)


def _ffn_body(x_ref, ohg_ref, omla_ref, mod_ref, conv0_ref, wout_ref, gffn_ref, wup_ref, cw_ref, cb_ref,
              wdn_ref, gfin_ref, y_ref, conv_out, carry_scr, a_scr, u_scr, *, tm, ft):
    @pl.when(pl.program_id(1) == 0)
    def _():
        carry_scr[...] = conv0_ref[0]

    mod = mod_ref[0]
    g1 = mod[2:3]
    sh2 = mod[3:4]
    sc2 = mod[4:5]
    g2 = mod[5:6]
    omla = jnp.concatenate([omla_ref[0, h] for h in range(MLA_HEADS)], axis=1)
    o = _dot(ohg_ref[0], wout_ref[0:HG_W, :]) + _dot(omla, wout_ref[HG_W:2 * HG_W, :])
    x1 = x_ref[0] + g1 * o
    h2 = (_rms(x1, gffn_ref[...]) * (1.0 + sc2) + sh2).astype(BF16)
    for j in range(D_FF // ft):
        cs = slice(j * ft, (j + 1) * ft)
        a = _dot(h2, wup_ref[:, cs])
        v = _dot(h2, wup_ref[:, D_FF + j * ft:D_FF + (j + 1) * ft])
        a_scr[0:CARRY_ROWS, :] = carry_scr[:, cs]
        a_scr[CARRY_ROWS:CARRY_ROWS + tm, :] = a
        carry_scr[:, cs] = a[tm - CARRY_ROWS:tm]
        cw = cw_ref[:, cs]
        conv = (cb_ref[:, cs] + cw[0:1] * a_scr[CARRY_ROWS - 2:CARRY_ROWS - 2 + tm, :]
                + cw[1:2] * a_scr[CARRY_ROWS - 1:CARRY_ROWS - 1 + tm, :] + cw[2:3] * a)
        g = 0.5 * conv * (1.0 + lax.erf(conv * np.float32(np.sqrt(0.5))))
        u_scr[:, cs] = (g * v).astype(BF16)
        conv_out[0, :, cs] = a[tm - (CONV_W - 1):tm]

    x2 = x1 + g2 * _dot(u_scr[...], wdn_ref[...])
    y_ref[0] = _rms(x2, gfin_ref[...])


def _ffn(x, ohg, omla, mod, conv0, wout, gffn, wup, cw, cb, wdn, gfin, tm, ft):
    B, L, D = x.shape
    row = lambda w: pl.BlockSpec((1, tm, w), lambda b, i: (b, i, 0))
    once = lambda a: pl.BlockSpec(a.shape, lambda b, i: (0,) * a.ndim, pipeline_mode=pl.Buffered(1))
    return pl.pallas_call(
        functools.partial(_ffn_body, tm=tm, ft=ft),
        out_shape=[jax.ShapeDtypeStruct((B, L, D), F32),
                   jax.ShapeDtypeStruct((B, CONV_W - 1, D_FF), F32)],
        grid=(B, L // tm),
        in_specs=[row(D), row(HG_W), pl.BlockSpec((1, MLA_HEADS, tm, MLA_V), lambda b, i: (b, 0, i, 0)),
                  pl.BlockSpec((1, 6, D), lambda b, i: (b, 0, 0)),
                  pl.BlockSpec((1, CARRY_ROWS, D_FF), lambda b, i: (b, 0, 0)),
                  once(wout), once(gffn), once(wup), once(cw), once(cb), once(wdn), once(gfin)],
        out_specs=[row(D), pl.BlockSpec((1, CONV_W - 1, D_FF), lambda b, i: (b, 0, 0))],
        scratch_shapes=[pltpu.VMEM((CARRY_ROWS, D_FF), F32), pltpu.VMEM((CARRY_ROWS + tm, ft), F32),
                        pltpu.VMEM((tm, D_FF), BF16)],
        compiler_params=pltpu.CompilerParams(dimension_semantics=("arbitrary", "arbitrary"),
                                             vmem_limit_bytes=VMEM_LIMIT),
        name="out_ffn",
    )(x, ohg, omla, mod, conv0, wout, gffn, wup, cw, cb, wdn, gfin)


def _ffn_rows_body(x_ref, ohg_ref, omla_ref, mod_ref, conv0_ref, wout_ref, gffn_ref, wa_ref, wv_ref, cw_ref, cb_ref,
                   wdn_ref, gfin_ref, y_ref, conv_out, x1_scr, h2_scr, a_scr, acc_scr, *, nb, n):
    t = pl.program_id(0)
    seq = lambda b: slice(n * b, n * (b + 1))

    @pl.when(t == 0)
    def _():
        o = _dot(ohg_ref[...], wout_ref[0:HG_W, :]) + _dot(omla_ref[...], wout_ref[HG_W:2 * HG_W, :])
        for b in range(nb):
            mod = mod_ref[b]
            x1 = x_ref[seq(b), :] + mod[2:3] * o[seq(b)]
            x1_scr[seq(b), :] = x1
            h2_scr[seq(b), :] = (_rms(x1, gffn_ref[...]) * (1.0 + mod[4:5]) + mod[3:4]).astype(BF16)
        acc_scr[...] = jnp.zeros(acc_scr.shape, F32)

    h2 = h2_scr[...]
    a = _dot(h2, wa_ref[...])
    v = _dot(h2, wv_ref[...])
    cw = cw_ref[...]
    convs = []
    for b in range(nb):
        rows = slice((CARRY_ROWS + n) * b, (CARRY_ROWS + n) * (b + 1))
        a_scr[rows, :] = jnp.concatenate([conv0_ref[b], a[seq(b)]], axis=0)
        r0 = (CARRY_ROWS + n) * b + CARRY_ROWS
        convs.append(cb_ref[...] + cw[0:1] * a_scr[r0 - 2:r0 - 2 + n, :] + cw[1:2] * a_scr[r0 - 1:r0 - 1 + n, :]
                     + cw[2:3] * a[seq(b)])
        conv_out[b] = a[n * (b + 1) - (CONV_W - 1):n * (b + 1)]
    conv = jnp.concatenate(convs, axis=0)
    g = 0.5 * conv * (1.0 + lax.erf(conv * np.float32(np.sqrt(0.5))))
    acc_scr[...] += _dot((g * v).astype(BF16), wdn_ref[...])

    @pl.when(t == pl.num_programs(0) - 1)
    def _():
        for b in range(nb):
            x2 = x1_scr[seq(b), :] + mod_ref[b][5:6] * acc_scr[seq(b), :]
            y_ref[seq(b), :] = _rms(x2, gfin_ref[...])


def _ffn_rows(x, ohg, omla, mod, conv0, wout, gffn, wup, cw, cb, wdn, gfin):
    nb, n, D = x.shape
    r = nb * n
    ft = 256
    nf = D_FF // ft
    omla_rows = jnp.swapaxes(omla, 1, 2).reshape(r, MLA_HEADS * MLA_V)
    whole = lambda a: pl.BlockSpec(a.shape, lambda t: (0,) * a.ndim)
    y, conv_new = pl.pallas_call(
        functools.partial(_ffn_rows_body, nb=nb, n=n),
        out_shape=[jax.ShapeDtypeStruct((r, D), F32), jax.ShapeDtypeStruct((nb, CONV_W - 1, D_FF), F32)],
        grid=(nf,),
        in_specs=[pl.BlockSpec((r, D), lambda t: (0, 0)), pl.BlockSpec((r, HG_W), lambda t: (0, 0)),
                  pl.BlockSpec((r, MLA_HEADS * MLA_V), lambda t: (0, 0)), whole(mod),
                  pl.BlockSpec((nb, CARRY_ROWS, ft), lambda t: (0, 0, t)),
                  whole(wout), whole(gffn),
                  pl.BlockSpec((D, ft), lambda t: (0, t)), pl.BlockSpec((D, ft), lambda t: (0, nf + t)),
                  pl.BlockSpec((CONV_W, ft), lambda t: (0, t)), pl.BlockSpec((1, ft), lambda t: (0, t)),
                  pl.BlockSpec((ft, D), lambda t: (t, 0)), whole(gfin)],
        out_specs=[pl.BlockSpec((r, D), lambda t: (0, 0)),
                   pl.BlockSpec((nb, CONV_W - 1, ft), lambda t: (0, 0, t))],
        scratch_shapes=[pltpu.VMEM((r, D), F32), pltpu.VMEM((r, D), BF16),
                        pltpu.VMEM((nb * (CARRY_ROWS + n), ft), F32), pltpu.VMEM((r, D), F32)],
        compiler_params=pltpu.CompilerParams(dimension_semantics=("arbitrary",), vmem_limit_bytes=VMEM_LIMIT),
        name="out_ffn_rows",
    )(x.reshape(r, D), ohg.reshape(r, HG_W), omla_rows, mod, conv0, wout, gffn, wup, wup, cw, cb, wdn, gfin)
    return y.reshape(nb, n, D), conv_new


def _prep_weights(w_in, mla_q_norm_gain, mla_kv_norm_gain, w_uq, w_uk, w_uv):
    win = w_in.astype(BF16)
    kp = win[:, 4 * HG_W + Q_LORA + KV_LORA:]
    wkx = jnp.concatenate([jnp.tile(kp[:, :ROPE_HALF], (1, MLA_HEADS)),
                           jnp.tile(kp[:, ROPE_HALF:], (1, MLA_HEADS))], axis=1)
    uq = w_uq.reshape(Q_LORA, MLA_HEADS, MLA_NOPE + MLA_ROPE)
    wuqn = uq[:, :, :MLA_NOPE].reshape(Q_LORA, MLA_HEADS * MLA_NOPE).astype(BF16)
    x1a = uq[:, :, MLA_NOPE:MLA_NOPE + ROPE_HALF].reshape(Q_LORA, MLA_HEADS * ROPE_HALF)
    x2a = uq[:, :, MLA_NOPE + ROPE_HALF:].reshape(Q_LORA, MLA_HEADS * ROPE_HALF)
    wuqr = jnp.concatenate([x1a, x2a, jnp.roll(x1a, ROPE_HALF, axis=1), jnp.roll(x2a, ROPE_HALF, axis=1)],
                           axis=1).astype(BF16)
    wuk = w_uk.reshape(KV_LORA, MLA_HEADS * MLA_NOPE).astype(BF16)
    wuv = w_uv.reshape(KV_LORA, MLA_HEADS * MLA_V).astype(BF16)
    return (win, wkx, kp.T, mla_q_norm_gain.reshape(1, -1), mla_kv_norm_gain.reshape(1, -1), wuqn, wuqr, wuk, wuv)


def _rope_tables(start, n):
    inv_freq = ROPE_THETA ** (-np.arange(ROPE_HALF, dtype=np.float64) / ROPE_HALF)
    ang = (start + np.arange(n, dtype=np.float64))[:, None] * inv_freq[None, :]
    cos, sin = np.cos(ang).astype(np.float32), np.sin(ang).astype(np.float32)
    return tuple(jnp.asarray(a) for a in (np.tile(cos, (1, MLA_HEADS)), np.tile(sin, (1, MLA_HEADS)), cos.T, sin.T))


def _tile(n, pref):
    return pref if n % pref == 0 else n


def kernel(x_prompt, x_sample, c_prompt, c_sample, cache_kv_latent, cache_k_rope, state_hgrn, state_ffn_conv, w_ada, b_ada, norm_mix_gain, w_in, hg_lb_logits, hg_norm_gain, mla_q_norm_gain, mla_kv_norm_gain, w_uq, w_uk, w_uv, w_out, norm_ffn_gain, w_up, conv_w, conv_b, w_down, final_norm_gain):
    assert w_ada.shape[0] == 1, "single-layer trunk"
    B, L, D = x_prompt.shape
    Bs, Ls, _ = x_sample.shape
    past = cache_kv_latent.shape[2]

    mod = _ada(jnp.concatenate([c_prompt, c_sample], axis=0), w_ada[0], b_ada)
    mod = mod.reshape(B + Bs, 6, D)
    wts = _prep_weights(w_in[0], mla_q_norm_gain[0], mla_kv_norm_gain[0], w_uq[0], w_uk[0], w_uv[0])
    gmix = norm_mix_gain
    wout = w_out[0].astype(BF16)
    wup = w_up[0].astype(BF16)
    wdn = w_down[0].astype(BF16)
    gfin = final_norm_gain.reshape(1, D)
    ffn_w = (wout, norm_ffn_gain, wup, conv_w[0], conv_b, wdn, gfin)

    def layer(x, mod_x, pos, s0, conv0, cache):
        n, l, _ = x.shape
        rope = _rope_tables(pos, l)
        tm = _tile(l, 512)
        if tm % UNIT == 0:
            ohg, s_new, q, k, v, lat, kpe = _inproj_hgrn(x, mod_x, gmix, wts, rope, hg_lb_logits, hg_norm_gain,
                                                         s0, tm)
        else:
            raw, q, k, v, lat, kpe = _inproj(x, mod_x, gmix, wts, rope, tm)
            ohg, s_new = _hgrn(raw, hg_lb_logits, hg_norm_gain, s0, _tile(l, 256))
        if cache is None:
            omla = _attn(q, k, v, _tile(l, 512))
        else:
            omla = _attn_cache(q, k, v, cache[0], cache[1], wts[7], wts[8])
        if l % 512 == 0:
            y, conv_new = _ffn(x, ohg, omla, mod_x, conv0, *ffn_w, 512, FF_TILE)
        elif l % CARRY_ROWS == 0 and (n * l) % 16 == 0 and n * l <= 512:
            y, conv_new = _ffn_rows(x, ohg, omla, mod_x, conv0, *ffn_w)
        else:
            y, conv_new = _ffn(x, ohg, omla, mod_x, conv0, *ffn_w, l, FF_TILE)
        return y, lat[None], jnp.swapaxes(kpe, 1, 2)[None], s_new[None], conv_new[None]

    zeros_state = jnp.zeros((B, HG_HEADS, HG_DK, HG_DV), F32)
    zeros_conv = jnp.zeros((B, CARRY_ROWS, D_FF), F32)
    conv0_s = jnp.pad(state_ffn_conv[0], ((0, 0), (CARRY_ROWS - (CONV_W - 1), 0), (0, 0)))
    yp, latp, kpep, hgp, cvp = layer(x_prompt, mod[:B], 0, zeros_state, zeros_conv, None)
    ys, lats, kpes, hgs, cvs = layer(x_sample, mod[B:], past, state_hgrn[0], conv0_s,
                                     (cache_kv_latent[0], jnp.swapaxes(cache_k_rope[0], 1, 2)))
    return (yp, ys, latp, kpep, hgp, cvp, lats, kpes, hgs, cvs)
```

```python
import functools

import numpy as np
import jax
import jax.numpy as jnp
from jax import lax
from jax.experimental import pallas as pl
from jax.experimental.pallas import tpu as pltpu

F32 = jnp.float32
BF16 = jnp.bfloat16

D_MODEL = 1024
CHUNK = 64
HG_HEADS = 4
HG_DK = 128
HG_DV = 128
HG_W = HG_HEADS * HG_DK
MLA_HEADS = 4
MLA_NOPE = 128
MLA_ROPE = 64
ROPE_HALF = MLA_ROPE // 2
MLA_V = 128
Q_LORA = 384
KV_LORA = 256
ROPE_THETA = 10000.0
MLA_SCALE = (MLA_NOPE + MLA_ROPE) ** -0.5
Q_SCALE = MLA_SCALE * float(np.log2(np.e))
QK_W = 2 * MLA_NOPE
D_FF = 2816
CONV_W = 3
EPS = 1e-6

SUB = 16
FF_TILE = D_FF
ATTN_LOOKAHEAD = 4
CARRY_ROWS = 8
VMEM_LIMIT = 56 * 1024 * 1024
UNIT = 128
SLABS = 8
SLAB_ROWS = UNIT // SLABS
N_LEVELS = 7
DIAG_LEVEL = N_LEVELS


def _dot(a, b):
    return jnp.dot(a, b, preferred_element_type=F32)


def _dot_nt(a, b):
    return lax.dot_general(a, b, (((1,), (1,)), ((), ())), preferred_element_type=F32)


def _dot_tn(a, b):
    return lax.dot_general(a, b, (((0,), (0,)), ((), ())), preferred_element_type=F32)


def _split3(x):
    x1 = x.astype(BF16)
    r1 = x - x1.astype(F32)
    x2 = r1.astype(BF16)
    r2 = r1 - x2.astype(F32)
    return x1, x2, r2.astype(BF16)


def _split2(x):
    x1 = x.astype(BF16)
    return x1, (x - x1.astype(F32)).astype(BF16)


def _rms(x, g):
    return x * lax.rsqrt(jnp.mean(x * x, axis=-1, keepdims=True) + EPS) * g


def _const_spec(shape):
    n = len(shape)
    return pl.BlockSpec(shape, lambda *_: (0,) * n)


def _ada_body(c_ref, w_ref, b_ref, o_ref):
    c = c_ref[...]
    s = c * jax.nn.sigmoid(c)
    s1, s2, _ = _split3(s)
    w = w_ref[...]
    w1 = w.astype(BF16)
    w2 = (w - w1.astype(F32)).astype(BF16)
    o_ref[...] = _dot(s1, w1) + _dot(s1, w2) + _dot(s2, w1) + b_ref[...]


def _ada(c, w_ada, b_ada):
    n, d = c.shape
    nout = w_ada.shape[1]
    tn = 2048
    return pl.pallas_call(
        _ada_body,
        out_shape=jax.ShapeDtypeStruct((n, nout), F32),
        grid=(nout // tn,),
        in_specs=[pl.BlockSpec((n, d), lambda j: (0, 0)),
                  pl.BlockSpec((d, tn), lambda j: (0, j)),
                  pl.BlockSpec((1, tn), lambda j: (0, j))],
        out_specs=pl.BlockSpec((n, tn), lambda j: (0, j)),
        compiler_params=pltpu.CompilerParams(dimension_semantics=("arbitrary",)),
        name="adaln_mod",
    )(c, w_ada, b_ada)


def _rope_slots(r1, r2, nope, out_ref, slot):
    for h in range(MLA_HEADS):
        xh = jnp.where(slot == h, r1, jnp.where(slot == (h + 1) % MLA_HEADS, r2, 0.0))
        out_ref[0, h, :, 0:MLA_NOPE] = nope[:, MLA_NOPE * h:MLA_NOPE * (h + 1)].astype(BF16)
        out_ref[0, h, :, MLA_NOPE:QK_W] = xh.astype(BF16)


def _mla_stages(hb, refs, outs, rope_refs, q_transposed):
    (wcq_ref, wckv_ref, wkx_ref, wkpt_ref, qng_ref, kvng_ref, wuqn_ref, wuqr_ref, wuk_ref, wuvt_ref) = refs
    q_out, k_out, v_out, lat_out, kpe_out = outs
    cos_ref, sin_ref, cost_ref, sint_ref = rope_refs
    t = {}

    def project():
        t["cq"] = _dot(hb, wcq_ref[...])
        t["ckv"] = _dot(hb, wckv_ref[...])
        t["kx"] = _dot(hb, wkx_ref[...])
        if not q_transposed:
            t["kxt"] = _dot_nt(wkpt_ref[...], hb)

    def q_up():
        cq = _rms(t["cq"], qng_ref[...])
        if q_transposed:
            cq = cq.T.astype(BF16)
            t["qn"] = _dot(wuqn_ref[...], cq) * Q_SCALE
            t["qr"] = _dot(wuqr_ref[...], cq) * Q_SCALE
        else:
            cq = cq.astype(BF16)
            t["qn"] = _dot(cq, wuqn_ref[...]) * Q_SCALE
            t["qr"] = _dot(cq, wuqr_ref[...]) * Q_SCALE

    def kv_up():
        lat = _rms(t["ckv"], kvng_ref[...])
        lat_out[0] = lat
        latb = lat.astype(BF16)
        t["kn"] = _dot(latb, wuk_ref[...])
        if q_transposed:
            v_out[0] = _dot(wuvt_ref[...], lat.T.astype(BF16)).astype(BF16)
        else:
            v_out[0] = _dot_nt(wuvt_ref[...], latb).astype(BF16)

    def slot_of(cos):
        return lax.broadcasted_iota(jnp.int32, cos.shape, 1) // ROPE_HALF

    def q_rope():
        qr, qn = t["qr"], t["qn"]
        if q_transposed:
            cos = jnp.concatenate([cost_ref[...]] * MLA_HEADS, axis=0)
            sin = jnp.concatenate([sint_ref[...]] * MLA_HEADS, axis=0)
            qr1 = qr[0:128] * cos - qr[128:256] * sin
            qr2 = qr[384:512] * cos + qr[256:384] * sin
            slot = lax.broadcasted_iota(jnp.int32, cos.shape, 0) // ROPE_HALF
            for h in range(MLA_HEADS):
                xh = jnp.where(slot == h, qr1, jnp.where(slot == (h + 1) % MLA_HEADS, qr2, 0.0))
                q_out[0, h, 0:MLA_NOPE, :] = qn[MLA_NOPE * h:MLA_NOPE * (h + 1)].astype(BF16)
                q_out[0, h, MLA_NOPE:QK_W, :] = xh.astype(BF16)
            return
        cos, sin = cos_ref[...], sin_ref[...]
        qr1 = qr[:, 0:128] * cos - qr[:, 128:256] * sin
        qr2 = qr[:, 384:512] * cos + qr[:, 256:384] * sin
        _rope_slots(qr1, qr2, qn, q_out, slot_of(cos))

    def k_rope():
        cos, sin, kx = cos_ref[...], sin_ref[...], t["kx"]
        kr1 = kx[:, 0:128] * cos - kx[:, 128:256] * sin
        kr2 = kx[:, 128:256] * cos + kx[:, 0:128] * sin
        _rope_slots(kr1, kr2, t["kn"], k_out, slot_of(cos))
        if q_transposed:
            lane = lax.broadcasted_iota(jnp.int32, cos.shape, 1)
            kpe_out[0] = jnp.where(lane < ROPE_HALF, kr1, kr2).T[0:MLA_ROPE]
            return
        kxt = t["kxt"]
        x1t, x2t = kxt[0:ROPE_HALF], kxt[ROPE_HALF:MLA_ROPE]
        cost, sint = cost_ref[...], sint_ref[...]
        kpe_out[0] = jnp.concatenate([x1t * cost - x2t * sint, x2t * cost + x1t * sint], axis=0)

    return [project, q_up, kv_up, q_rope, k_rope]


def _normed_input(x_ref, mod_ref, gmix_ref):
    mod = mod_ref[0]
    return (_rms(x_ref[0], gmix_ref[...]) * (1.0 + mod[1:2]) + mod[0:1]).astype(BF16)


def _win_views(win_ref):
    hgw = 4 * HG_W
    return (win_ref.at[:, 0:hgw], win_ref.at[:, hgw:hgw + Q_LORA],
            win_ref.at[:, hgw + Q_LORA:hgw + Q_LORA + KV_LORA])


def _inproj_body(x_ref, mod_ref, gmix_ref, win_ref, wkx_ref, wkpt_ref, qng_ref, kvng_ref,
                 wuqn_ref, wuqr_ref, wuk_ref, wuvt_ref, cos_ref, sin_ref, cost_ref, sint_ref,
                 hg_out, q_out, k_out, v_out, lat_out, kpe_out):
    whg_ref, wcq_ref, wckv_ref = _win_views(win_ref)
    hb = _normed_input(x_ref, mod_ref, gmix_ref)
    hg_out[0] = _dot(hb, whg_ref[...])
    for stage in _mla_stages(hb, (wcq_ref, wckv_ref, wkx_ref, wkpt_ref, qng_ref, kvng_ref, wuqn_ref, wuqr_ref,
                                  wuk_ref, wuvt_ref),
                             (q_out, k_out, v_out, lat_out, kpe_out), (cos_ref, sin_ref, cost_ref, sint_ref),
                             False):
        stage()


def _inproj_hgrn_body(x_ref, mod_ref, gmix_ref, win_ref, wkx_ref, wkpt_ref, qng_ref, kvng_ref,
                      wuqn_ref, wuqr_ref, wuk_ref, wuvt_ref, cos_ref, sin_ref, cost_ref, sint_ref,
                      perm_ref, lbl_ref, gain_ref, s0t_ref, tin_ref, lv_ref, unperm_ref,
                      hg_out, st_out, q_out, k_out, v_out, lat_out, kpe_out, raw_scr, st_scr, *, nt):
    j = pl.program_id(0)

    @pl.when(j == 0)
    def _():
        raw_scr[...] = jnp.zeros(raw_scr.shape, F32)

    first_of_batch = lax.rem(jnp.maximum(j - 1, 0), nt) == 0
    st = [jnp.where(first_of_batch, s0t_ref[0, hd], st_scr[hd]) for hd in range(HG_HEADS)]

    whg_ref, wcq_ref, wckv_ref = _win_views(win_ref)
    n_units = x_ref.shape[1] // UNIT
    t = {}

    def start():
        hb = _normed_input(x_ref, mod_ref, gmix_ref)
        t["hbp"] = jnp.concatenate([_dot(perm_ref[...], hb[UNIT * u:UNIT * (u + 1)]).astype(BF16)
                                    for u in range(n_units)], axis=0)
        t["mla"] = _mla_stages(hb, (wcq_ref, wckv_ref, wkx_ref, wkpt_ref, qng_ref, kvng_ref, wuqn_ref,
                                    wuqr_ref, wuk_ref, wuvt_ref),
                               (q_out, k_out, v_out, lat_out, kpe_out), (cos_ref, sin_ref, cost_ref, sint_ref),
                               True)

    def raw_group(k):
        def run():
            raw_scr[:, HG_W * k:HG_W * (k + 1)] = _dot(t["hbp"], whg_ref[:, HG_W * k:HG_W * (k + 1)])
        return run

    chunks = [start] + [raw_group(k) for k in range(4)] + [lambda n=n: t["mla"][n]() for n in range(5)]

    def next_chunk(_level):
        if chunks:
            chunks.pop(0)()

    st = _hgrn_units_math(lambda u, k: raw_scr[UNIT * u:UNIT * (u + 1), HG_W * k:HG_W * (k + 1)], n_units, st,
                          lbl_ref, gain_ref, tin_ref, lv_ref, unperm_ref, hg_out, next_chunk)
    while chunks:
        chunks.pop(0)()
    for hd in range(HG_HEADS):
        st_scr[hd] = st[hd]
        st_out[0, hd] = st[hd]


def _unit_perm(n):
    i = np.arange(n)
    src = UNIT * (i // UNIT) + _unit_time()[i % UNIT]
    return jnp.asarray(src[:, None] == i[None, :], BF16)


def _inproj_args(x, mod, gmix, wts, rope, q_transposed):
    (win, wkx, wkpt, qng, kvng, wuqn, wuqr, wuk, wuv) = wts
    if q_transposed:
        wuqn, wuqr = wuqn.T, wuqr.T
    return [x, mod, gmix, win, wkx, wkpt, qng, kvng, wuqn, wuqr, wuk, wuv.T, *rope]


def _inproj_out_shapes(B, L, q_transposed):
    q_shape = (B, MLA_HEADS, QK_W, L) if q_transposed else (B, MLA_HEADS, L, QK_W)
    return [(q_shape, BF16), ((B, MLA_HEADS, L, QK_W), BF16), ((B, MLA_HEADS * MLA_V, L), BF16),
            ((B, L, KV_LORA), F32), ((B, MLA_ROPE, L), F32)]


def _inproj(x, mod, gmix, wts, rope, tm):
    B, L, D = x.shape
    args = _inproj_args(x, mod, gmix, wts, rope, False)
    row = lambda w: pl.BlockSpec((1, tm, w), lambda b, i: (b, i, 0))
    col = lambda h: pl.BlockSpec((1, h, tm), lambda b, i: (b, 0, i))
    in_specs = [row(D), pl.BlockSpec((1, 6, D), lambda b, i: (b, 0, 0))]
    in_specs += [_const_spec(w.shape) for w in args[2:12]]
    in_specs += [pl.BlockSpec((tm, 128), lambda b, i: (i, 0))] * 2
    in_specs += [pl.BlockSpec((ROPE_HALF, tm), lambda b, i: (0, i))] * 2
    shapes = [((B, L, 4 * HG_W), F32)] + _inproj_out_shapes(B, L, False)
    heads = pl.BlockSpec((1, MLA_HEADS, tm, QK_W), lambda b, i: (b, 0, i, 0))
    out_specs = [row(4 * HG_W), heads, heads, col(MLA_HEADS * MLA_V),
                 row(KV_LORA), col(MLA_ROPE)]
    return pl.pallas_call(
        _inproj_body,
        out_shape=[jax.ShapeDtypeStruct(s, dt) for s, dt in shapes],
        grid=(B, L // tm),
        in_specs=in_specs,
        out_specs=out_specs,
        compiler_params=pltpu.CompilerParams(dimension_semantics=("arbitrary", "arbitrary"),
                                             vmem_limit_bytes=VMEM_LIMIT),
        name="in_proj",
    )(*args)


def _inproj_hgrn(x, mod, gmix, wts, rope, lb_logits, gain, s0, tm):
    B, L, D = x.shape
    nt = L // tm
    n = B * nt
    args = _inproj_args(x, mod, gmix, wts, rope, True)
    tin, lv = _unit_tables()
    perm = _unit_perm(UNIT)
    args += [perm, lb_logits, gain, jnp.swapaxes(s0, 2, 3), tin, lv, perm.T]
    cur = lambda j: jnp.minimum(j, n - 1)
    lag = lambda j: jnp.maximum(j - 1, 0)
    row = lambda w, t: pl.BlockSpec((1, tm, w), lambda j: (t(j) // nt, t(j) % nt, 0))
    col = lambda h: pl.BlockSpec((1, h, tm), lambda j: (cur(j) // nt, 0, cur(j) % nt))
    state = pl.BlockSpec((1, HG_HEADS, HG_DV, HG_DK), lambda j: (lag(j) // nt, 0, 0, 0))
    in_specs = [row(D, cur), pl.BlockSpec((1, 6, D), lambda j: (cur(j) // nt, 0, 0))]
    in_specs += [_const_spec(w.shape) for w in args[2:12]]
    in_specs += [pl.BlockSpec((tm, 128), lambda j: (cur(j) % nt, 0))] * 2
    in_specs += [pl.BlockSpec((ROPE_HALF, tm), lambda j: (0, cur(j) % nt))] * 2
    in_specs += [_const_spec(perm.shape), _const_spec(lb_logits.shape), _const_spec(gain.shape), state,
                 _const_spec(tin.shape), _const_spec(lv.shape), _const_spec(perm.shape)]
    shapes = [((B, L, HG_W), BF16), ((B, HG_HEADS, HG_DV, HG_DK), F32)] + _inproj_out_shapes(B, L, True)
    heads = pl.BlockSpec((1, MLA_HEADS, tm, QK_W), lambda j: (cur(j) // nt, 0, cur(j) % nt, 0))
    heads_t = pl.BlockSpec((1, MLA_HEADS, QK_W, tm), lambda j: (cur(j) // nt, 0, 0, cur(j) % nt))
    out_specs = [row(HG_W, lag), state, heads_t, heads,
                 col(MLA_HEADS * MLA_V), row(KV_LORA, cur), col(MLA_ROPE)]
    ohg, st, *mla = pl.pallas_call(
        functools.partial(_inproj_hgrn_body, nt=nt),
        out_shape=[jax.ShapeDtypeStruct(s, dt) for s, dt in shapes],
        grid=(n + 1,),
        in_specs=in_specs,
        out_specs=out_specs,
        scratch_shapes=[pltpu.VMEM((tm, 4 * HG_W), F32), pltpu.VMEM((HG_HEADS, HG_DV, HG_DK), F32)],
        compiler_params=pltpu.CompilerParams(dimension_semantics=("arbitrary",),
                                             vmem_limit_bytes=VMEM_LIMIT),
        name="in_proj_hgrn",
    )(*args)
    return (ohg, jnp.swapaxes(st, 2, 3), *mla)


def _hgrn_body(raw_ref, lbl_ref, gain_ref, s0_ref, tin_ref, tup_ref, ones_ref,
               o_out, s_out,
               st_scr, b_scr, q_scr, kk_scr, v_scr, eb_scr, qt_scr, kt_scr, vb_scr, p_scr, o_scr, *, th):
    i = pl.program_id(1)

    @pl.when(i == 0)
    def _():
        for h in range(HG_HEADS):
            st_scr[h] = s0_ref[0, h].T

    lbl = lbl_ref[...]
    e = jnp.exp(lbl - jnp.max(lbl, axis=0, keepdims=True))
    lb = (e / jnp.sum(e, axis=0, keepdims=True))[0:1]

    q = raw_ref[0, :, 0:HG_W] * HG_DK ** -0.5
    f = lb + (1.0 - lb) * jax.nn.sigmoid(raw_ref[0, :, HG_W:2 * HG_W])
    kk = 1.0 - f
    v = raw_ref[0, :, 2 * HG_W:3 * HG_W]
    l1, l2, l3 = _split3(jnp.log(f))
    tin = tin_ref[...]
    tup = tup_ref[...]
    b = _dot(tin, l1) + _dot(tin, l2) + _dot(tin, l3)
    c = _dot(tup, l1) + _dot(tup, l2) + _dot(tup, l3)
    eb = jnp.exp(b)
    b_scr[...] = b
    q_scr[...] = q
    kk_scr[...] = kk
    v_scr[...] = v
    eb_scr[...] = eb
    qt_scr[...] = (q * eb).astype(BF16)
    kt_scr[...] = (kk * jnp.exp(c)).astype(BF16)
    vb_scr[...] = v.astype(BF16)

    rowid = lax.broadcasted_iota(jnp.int32, (SUB, HG_W), 0)

    def step(j, carry):
        r0 = pl.multiple_of(j * SUB, SUB)
        bj = b_scr[pl.ds(r0, SUB), :]
        qj = q_scr[pl.ds(r0, SUB), :]
        for s in range(SUB):
            bs = b_scr[pl.ds(r0 + s, 1), :]
            ks = kk_scr[pl.ds(r0 + s, 1), :]
            dec = jnp.exp(jnp.where(rowid >= s, bj - bs, -jnp.inf))
            p_scr[s * SUB:(s + 1) * SUB, :] = (qj * dec * ks).astype(BF16)
        r = _dot(p_scr[...], ones_ref[...])
        od = jnp.zeros((SUB, HG_W), F32)
        for s in range(SUB):
            od = od + r[s * SUB:(s + 1) * SUB, :] * v_scr[pl.ds(r0 + s, 1), :]
        qt = qt_scr[pl.ds(r0, SUB), :]
        kt = kt_scr[pl.ds(r0, SUB), :]
        vb = vb_scr[pl.ds(r0, SUB), :]
        dj = eb_scr[pl.ds(r0 + SUB - 1, 1), :]
        for h in range(HG_HEADS):
            hs = slice(h * HG_DK, (h + 1) * HG_DK)
            st = st_scr[h]
            oi = _dot_nt(qt[:, hs], st.astype(BF16))
            st_scr[h] = st * dj[:, hs] + _dot_tn(vb[:, hs], kt[:, hs])
            o_scr[pl.ds(r0, SUB), hs] = oi + od[:, hs]
        return carry

    lax.fori_loop(0, th // SUB, step, 0)

    gate = raw_ref[0, :, 3 * HG_W:4 * HG_W]
    gate = gate * jax.nn.sigmoid(gate)
    gain = gain_ref[...]
    for h in range(HG_HEADS):
        hs = slice(h * HG_DV, (h + 1) * HG_DV)
        o_out[0, :, hs] = (_rms(o_scr[:, hs], gain) * gate[:, hs]).astype(BF16)

    @pl.when(i == pl.num_programs(1) - 1)
    def _():
        for h in range(HG_HEADS):
            s_out[0, h] = st_scr[h].T


def _step_matrices(th):
    r = np.arange(th)
    same = (r[:, None] // SUB) == (r[None, :] // SUB)
    tin = (same & (r[None, :] <= r[:, None])).astype(np.float32)
    tup = (same & (r[None, :] > r[:, None])).astype(np.float32)
    hd = np.arange(HG_W) // HG_DK
    ones = (hd[:, None] == hd[None, :]).astype(np.float32)
    return jnp.asarray(tin, BF16), jnp.asarray(tup, BF16), jnp.asarray(ones, BF16)


def _hgrn(raw, lb_logits, gain, s0, th):
    B, L, _ = raw.shape
    tin, tup, ones = _step_matrices(th)
    f32s = lambda: pltpu.VMEM((th, HG_W), F32)
    b16s = lambda: pltpu.VMEM((th, HG_W), BF16)
    return pl.pallas_call(
        functools.partial(_hgrn_body, th=th),
        out_shape=[jax.ShapeDtypeStruct((B, L, HG_W), BF16),
                   jax.ShapeDtypeStruct((B, HG_HEADS, HG_DK, HG_DV), F32)],
        grid=(B, L // th),
        in_specs=[pl.BlockSpec((1, th, 4 * HG_W), lambda b, i: (b, i, 0)),
                  _const_spec(lb_logits.shape), _const_spec(gain.shape),
                  pl.BlockSpec((1, HG_HEADS, HG_DK, HG_DV), lambda b, i: (b, 0, 0, 0)),
                  _const_spec(tin.shape), _const_spec(tup.shape), _const_spec(ones.shape)],
        out_specs=[pl.BlockSpec((1, th, HG_W), lambda b, i: (b, i, 0)),
                   pl.BlockSpec((1, HG_HEADS, HG_DK, HG_DV), lambda b, i: (b, 0, 0, 0))],
        scratch_shapes=[pltpu.VMEM((HG_HEADS, HG_DV, HG_DK), F32),
                        f32s(), f32s(), f32s(), f32s(), f32s(),
                        b16s(), b16s(), b16s(),
                        pltpu.VMEM((SUB * SUB, HG_W), BF16), f32s()],
        compiler_params=pltpu.CompilerParams(dimension_semantics=("arbitrary", "arbitrary"),
                                             vmem_limit_bytes=VMEM_LIMIT),
        name="hgrn_scan",
    )(raw, lb_logits, gain, s0, tin, tup, ones)


def _unit_time():
    i = np.arange(UNIT)
    return SLABS * (i % SLAB_ROWS) + i // SLAB_ROWS


def _unit_tables():
    t = _unit_time()
    tin = (t[None, :] <= t[:, None]).astype(np.float32)
    rows = []
    for c in (1, 2, 4, 8):
        g = np.arange(SLAB_ROWS)
        t_bnd = SLABS * (2 * c * (g // (2 * c)) + c - 1) + SLABS - 1
        rows.append((t[None, :] <= t_bnd[:, None]).astype(np.float32))
    tin_ext = np.concatenate([tin] + rows, axis=0)
    x = t[:, None] ^ t[None, :]
    lv = np.where(x > 0, np.floor(np.log2(np.maximum(x, 1))), DIAG_LEVEL).astype(np.int32)
    lv = np.where(t[None, :] <= t[:, None], lv, -1)
    return jnp.asarray(tin_ext, BF16), jnp.asarray(np.concatenate([lv, lv], axis=1), jnp.int32)


def _pair_blockdiag(a, b):
    z = jnp.zeros_like(a)
    return jnp.concatenate([jnp.concatenate([a, z], axis=1), jnp.concatenate([z, b], axis=1)], axis=0)


def _hgrn_units_math(field, nu, st, lbl_ref, gain_ref, tin_ref, lv_ref, unperm_ref, o_out, after_level):
    lbl = lbl_ref[...]
    e = jnp.exp(lbl - jnp.max(lbl, axis=0, keepdims=True))
    lb = (e / jnp.sum(e, axis=0, keepdims=True))[0:1]
    tin = tin_ref[...]
    lv = lv_ref[...]
    gain = gain_ref[...]

    units = range(nu)
    pairs = range(HG_HEADS // 2)

    def pair_cols(x, hp):
        c0 = 2 * HG_DK * hp
        return x[:, c0:c0 + HG_DK], x[:, c0 + HG_DK:c0 + 2 * HG_DK]

    gates = [field(u, 3) for u in units]
    q = [field(u, 0) * HG_DK ** -0.5 for u in units]
    f = [lb + (1.0 - lb) * jax.nn.sigmoid(field(u, 1)) for u in units]
    kk = [1.0 - f[u] for u in units]
    lsplit = [_split2(jnp.log(f[u]) * np.float32(1.0 / np.log(2.0))) for u in units]
    bx = [_dot(tin, lsplit[u][0]) + _dot(tin, lsplit[u][1]) for u in units]
    b = [bx[u][0:UNIT] for u in units]
    bs = [[b[u][SLAB_ROWS * p:SLAB_ROWS * (p + 1)] for p in range(SLABS)] for u in units]
    vb = [field(u, 2).astype(BF16) for u in units]
    qb = [q[u].astype(BF16) for u in units]
    kkb_t = [kk[u].T.astype(BF16) for u in units]

    def slab_decay(u, level, p):
        if level < 3:
            half = 1 << level
            ref = p - p % (2 * half) + half - 1
            if p == ref:
                return jnp.ones_like(bs[u][p])
            return jnp.exp2(bs[u][p] - bs[u][ref] if p > ref else bs[u][ref] - bs[u][p])
        r = bx[u][UNIT + SLAB_ROWS * (level - 3):UNIT + SLAB_ROWS * (level - 2)]
        if level == N_LEVELS - 1:
            h = SLAB_ROWS // 2
            return jnp.concatenate([jnp.exp2(r[:h] - bs[u][p][:h]), jnp.exp2(bs[u][p][h:] - r[h:])], axis=0)
        return jnp.exp2(-jnp.abs(bs[u][p] - r))

    acc = [[jnp.zeros((UNIT, 2 * HG_DK), F32) for _ in pairs] for _ in units]
    for level in range(N_LEVELS + 1):
        here = lv == level
        for u in units:
            if level == DIAG_LEVEL:
                qt, kt_t = qb[u], kkb_t[u]
            else:
                dec = jnp.concatenate([slab_decay(u, level, p) for p in range(SLABS)], axis=0)
                qt, kt_t = qb[u] * dec.astype(BF16), kkb_t[u] * dec.T.astype(BF16)
            for hp in pairs:
                c0 = 2 * HG_DK * hp
                a2 = _dot(qt[:, c0:c0 + 2 * HG_DK], _pair_blockdiag(kt_t[c0:c0 + HG_DK], kt_t[c0 + HG_DK:c0 + 2 * HG_DK]))
                acc[u][hp] = jnp.where(here, a2, acc[u][hp])
        after_level(level)

    b_end = [b[u][UNIT - 1:UNIT] for u in units]
    qi = [(q[u] * jnp.exp2(b[u])).astype(BF16) for u in units]
    ks = [(kk[u] * jnp.exp2(b_end[u] - b[u])).astype(BF16) for u in units]
    d = [jnp.exp2(b_end[u]) for u in units]
    o_intra = [[_dot(acc[u][hp].astype(BF16), _pair_blockdiag(*pair_cols(vb[u], hp))) for hp in pairs]
               for u in units]
    ds = [[_dot_tn(vb[u][:, h * HG_DV:(h + 1) * HG_DV], ks[u][:, h * HG_DK:(h + 1) * HG_DK])
           for h in range(HG_HEADS)] for u in units]
    o_inter = []
    for u in units:
        o_inter.append([_dot_nt(qi[u][:, 2 * HG_DK * hp:2 * HG_DK * (hp + 1)],
                                _pair_blockdiag(st[2 * hp].astype(BF16), st[2 * hp + 1].astype(BF16)))
                        for hp in pairs])
        st = [st[h] * d[u][:, h * HG_DK:(h + 1) * HG_DK] + ds[u][h] for h in range(HG_HEADS)]
    res = []
    for u in units:
        o = jnp.concatenate([o_intra[u][hp] + o_inter[u][hp] for hp in pairs], axis=1)
        gate = gates[u] * jax.nn.sigmoid(gates[u])
        r = jnp.concatenate([_rms(o[:, h * HG_DV:(h + 1) * HG_DV], gain) for h in range(HG_HEADS)], axis=1)
        res.append((r * gate).astype(BF16))
    for u in units:
        o_out[0, UNIT * u:UNIT * (u + 1), :] = _dot(unperm_ref[...], res[u]).astype(BF16)
    return st


def _attn_body(qt_ref, k_ref, vt_ref, o_ref, acc_scr, *, tq, nq):
    for qi in range(nq):
        qt = qt_ref[0, 0, :, qi * tq:(qi + 1) * tq]
        nblk = qi + 1
        scores = lambda i: _dot(k_ref[0, 0, i * tq:(i + 1) * tq, :], qt)
        pending = [scores(i) for i in range(min(ATTN_LOOKAHEAD, nblk))]
        m = jnp.full((1, tq), -jnp.inf, F32)
        l = jnp.zeros((1, tq), F32)
        for ki in range(nblk):
            st = pending.pop(0)
            if ki + ATTN_LOOKAHEAD < nblk:
                pending.append(scores(ki + ATTN_LOOKAHEAD))
            if ki == nblk - 1:
                kc = lax.broadcasted_iota(jnp.int32, st.shape, 0) // CHUNK
                qc = lax.broadcasted_iota(jnp.int32, st.shape, 1) // CHUNK
                st = jnp.where(kc <= qc, st, -jnp.inf)
            m_new = jnp.maximum(m, jnp.max(st, axis=0, keepdims=True))
            p = jnp.exp2(st - m_new)
            alpha = jnp.exp2(m - m_new)
            l = alpha * l + jnp.sum(p, axis=0, keepdims=True)
            pv = _dot(vt_ref[0, :, ki * tq:(ki + 1) * tq], p.astype(BF16))
            if ki == 0:
                acc_scr[qi % 2] = pv
            else:
                acc_scr[qi % 2] = alpha * acc_scr[qi % 2] + pv
            m = m_new
        o_ref[0, 0, qi * tq:(qi + 1) * tq, :] = (acc_scr[qi % 2] / l).T.astype(BF16)


def _attn(qt, k, vt, tq):
    B, _, L, _ = k.shape
    per_head = lambda w: pl.BlockSpec((1, 1, L, w), lambda b, h: (b, h, 0, 0))
    return pl.pallas_call(
        functools.partial(_attn_body, tq=tq, nq=L // tq),
        out_shape=jax.ShapeDtypeStruct((B, MLA_HEADS, L, MLA_V), BF16),
        grid=(B, MLA_HEADS),
        in_specs=[pl.BlockSpec((1, 1, QK_W, L), lambda b, h: (b, h, 0, 0)), per_head(QK_W),
                  pl.BlockSpec((1, MLA_V, L), lambda b, h: (b, h, 0))],
        out_specs=per_head(MLA_V),
        scratch_shapes=[pltpu.VMEM((2, MLA_V, tq), F32)],
        compiler_params=pltpu.CompilerParams(dimension_semantics=("arbitrary",) * 2,
                                             vmem_limit_bytes=VMEM_LIMIT),
        name="mla_attn_prompt",
    )(qt, k, vt)


def _attn_cache_body(q_ref, kn_ref, vn_ref, clat_ref, ckpe_ref, wuk_ref, wuv_ref, place_ref, o_ref):
    n = q_ref.shape[2]
    latc = clat_ref[0].astype(BF16)
    kpec = ckpe_ref[0].astype(BF16)
    qlat, qrope, s_new = [], [], []
    for h in range(MLA_HEADS):
        qh = q_ref[0, h]
        ws = slice(MLA_NOPE * h, MLA_NOPE * (h + 1))
        qlat.append(_dot_nt(qh[:, 0:MLA_NOPE], wuk_ref[:, ws]).astype(BF16))
        qrope.append(_dot_nt(qh[:, MLA_NOPE:QK_W], place_ref[:, ws]).astype(BF16))
        s_new.append(_dot_nt(qh, kn_ref[0, h]))
    s_new = jnp.concatenate(s_new, axis=0)
    s_c = (_dot_nt(jnp.concatenate(qlat, axis=0), latc)
           + _dot(jnp.concatenate(qrope, axis=0), kpec))
    m = jnp.maximum(jnp.max(s_c, axis=-1, keepdims=True), jnp.max(s_new, axis=-1, keepdims=True))
    p_c = jnp.exp2(s_c - m)
    p_new = jnp.exp2(s_new - m)
    l = jnp.sum(p_c, axis=-1, keepdims=True) + jnp.sum(p_new, axis=-1, keepdims=True)
    o_lat = _dot(p_c.astype(BF16), latc).astype(BF16)
    p_new = p_new.astype(BF16)
    for h in range(MLA_HEADS):
        ws = slice(MLA_V * h, MLA_V * (h + 1))
        rs = slice(n * h, n * (h + 1))
        acc = _dot(o_lat[rs], wuv_ref[:, ws]) + _dot_nt(p_new[rs], vn_ref[0, ws, :])
        o_ref[0, h] = (acc / l[rs]).astype(BF16)


def _rope_place():
    p = np.zeros((MLA_ROPE, MLA_HEADS * MLA_NOPE), np.float32)
    for h in range(MLA_HEADS):
        for i in range(ROPE_HALF):
            p[i, MLA_NOPE * h + ROPE_HALF * h + i] = 1.0
            p[ROPE_HALF + i, MLA_NOPE * h + ROPE_HALF * ((h + 1) % MLA_HEADS) + i] = 1.0
    return jnp.asarray(p, BF16)


def _attn_cache(q, kn, vn, clat, ckpe, wuk, wuv):
    B, _, L, _ = q.shape
    P = clat.shape[1]
    place = _rope_place()
    bspec = lambda n, w: pl.BlockSpec((1, n, w), lambda b: (b, 0, 0))
    heads = lambda w: pl.BlockSpec((1, MLA_HEADS, L, w), lambda b: (b, 0, 0, 0))
    return pl.pallas_call(
        _attn_cache_body,
        out_shape=jax.ShapeDtypeStruct((B, MLA_HEADS, L, MLA_V), BF16),
        grid=(B,),
        in_specs=[heads(QK_W), heads(QK_W), bspec(MLA_HEADS * MLA_V, L),
                  bspec(P, KV_LORA), bspec(MLA_ROPE, P),
                  _const_spec(wuk.shape), _const_spec(wuv.shape), _const_spec(place.shape)],
        out_specs=heads(MLA_V),
        compiler_params=pltpu.CompilerParams(dimension_semantics=("arbitrary",),
                                             vmem_limit_bytes=VMEM_LIMIT),
        name="mla_attn_cache",
    )(q, kn, vn, clat, ckpe, wuk, wuv, place)


def _ffn_body(x_ref, ohg_ref, omla_ref, mod_ref, conv0_ref, wout_ref, gffn_ref, wa_ref, wv_ref, cw_ref, cb_ref,
              wdn_ref, gfin_ref, y_ref, conv_out, carry_scr, a_scr, u_scr, *, tm, ft):
    @pl.when(pl.program_id(1) == 0)
    def _():
        carry_scr[...] = conv0_ref[0]

    mod = mod_ref[0]
    g1 = mod[2:3]
    sh2 = mod[3:4]
    sc2 = mod[4:5]
    g2 = mod[5:6]
    omla = jnp.concatenate([omla_ref[0, h] for h in range(MLA_HEADS)], axis=1)
    o = _dot(ohg_ref[0], wout_ref[0:HG_W, :]) + _dot(omla, wout_ref[HG_W:2 * HG_W, :])
    x1 = x_ref[0] + g1 * o
    h2 = (_rms(x1, gffn_ref[...]) * (1.0 + sc2) + sh2).astype(BF16)
    for j in range(D_FF // ft):
        cs = slice(j * ft, (j + 1) * ft)
        a = _dot(h2, wa_ref[:, cs])
        v = _dot(h2, wv_ref[:, cs])
        a_scr[0:CARRY_ROWS, :] = carry_scr[:, cs]
        a_scr[CARRY_ROWS:CARRY_ROWS + tm, :] = a
        carry_scr[:, cs] = a[tm - CARRY_ROWS:tm]
        cw = cw_ref[:, cs]
        conv = (cb_ref[:, cs] + cw[0:1] * a_scr[CARRY_ROWS - 2:CARRY_ROWS - 2 + tm, :]
                + cw[1:2] * a_scr[CARRY_ROWS - 1:CARRY_ROWS - 1 + tm, :] + cw[2:3] * a)
        g = 0.5 * conv * (1.0 + lax.erf(conv * np.float32(np.sqrt(0.5))))
        u_scr[:, cs] = (g * v).astype(BF16)
        conv_out[0, :, cs] = a[tm - (CONV_W - 1):tm]

    x2 = x1 + g2 * _dot(u_scr[...], wdn_ref[...])
    y_ref[0] = _rms(x2, gfin_ref[...])


def _ffn(x, ohg, omla, mod, conv0, wout, gffn, wa, wv, cw, cb, wdn, gfin, tm, ft):
    B, L, D = x.shape
    row = lambda w: pl.BlockSpec((1, tm, w), lambda b, i: (b, i, 0))
    once = lambda a: pl.BlockSpec(a.shape, lambda b, i: (0,) * a.ndim, pipeline_mode=pl.Buffered(1))
    return pl.pallas_call(
        functools.partial(_ffn_body, tm=tm, ft=ft),
        out_shape=[jax.ShapeDtypeStruct((B, L, D), F32),
                   jax.ShapeDtypeStruct((B, CONV_W - 1, D_FF), F32)],
        grid=(B, L // tm),
        in_specs=[row(D), row(HG_W), pl.BlockSpec((1, MLA_HEADS, tm, MLA_V), lambda b, i: (b, 0, i, 0)),
                  pl.BlockSpec((1, 6, D), lambda b, i: (b, 0, 0)),
                  pl.BlockSpec((1, CARRY_ROWS, D_FF), lambda b, i: (b, 0, 0)),
                  once(wout), once(gffn), once(wa), once(wv), once(cw), once(cb), once(wdn), once(gfin)],
        out_specs=[row(D), pl.BlockSpec((1, CONV_W - 1, D_FF), lambda b, i: (b, 0, 0))],
        scratch_shapes=[pltpu.VMEM((CARRY_ROWS, D_FF), F32), pltpu.VMEM((CARRY_ROWS + tm, ft), F32),
                        pltpu.VMEM((tm, D_FF), BF16)],
        compiler_params=pltpu.CompilerParams(dimension_semantics=("arbitrary", "arbitrary"),
                                             vmem_limit_bytes=VMEM_LIMIT),
        name="out_ffn",
    )(x, ohg, omla, mod, conv0, wout, gffn, wa, wv, cw, cb, wdn, gfin)


def _ffn_rows_body(x_ref, ohg_ref, omla_ref, mod_ref, conv0_ref, wout_ref, gffn_ref, wa_ref, wv_ref, cw_ref, cb_ref,
                   wdn_ref, gfin_ref, y_ref, conv_out, wa_out, wv_out, wdn_out, x1_scr, h2_scr, a_scr, acc_scr,
                   *, nb, n):
    t = pl.program_id(0)
    seq = lambda b: slice(n * b, n * (b + 1))

    @pl.when(t == 0)
    def _():
        o = _dot(ohg_ref[...], wout_ref[0:HG_W, :]) + _dot(omla_ref[...], wout_ref[HG_W:2 * HG_W, :])
        for b in range(nb):
            mod = mod_ref[b]
            x1 = x_ref[seq(b), :] + mod[2:3] * o[seq(b)]
            x1_scr[seq(b), :] = x1
            h2_scr[seq(b), :] = (_rms(x1, gffn_ref[...]) * (1.0 + mod[4:5]) + mod[3:4]).astype(BF16)
        acc_scr[...] = jnp.zeros(acc_scr.shape, F32)

    h2 = h2_scr[...]
    wa_out[...] = wa_ref[...].astype(BF16)
    wv_out[...] = wv_ref[...].astype(BF16)
    wdn_out[...] = wdn_ref[...].astype(BF16)
    a = _dot(h2, wa_out[...])
    v = _dot(h2, wv_out[...])
    cw = cw_ref[...]
    convs = []
    for b in range(nb):
        rows = slice((CARRY_ROWS + n) * b, (CARRY_ROWS + n) * (b + 1))
        a_scr[rows, :] = jnp.concatenate([conv0_ref[b], a[seq(b)]], axis=0)
        r0 = (CARRY_ROWS + n) * b + CARRY_ROWS
        convs.append(cb_ref[...] + cw[0:1] * a_scr[r0 - 2:r0 - 2 + n, :] + cw[1:2] * a_scr[r0 - 1:r0 - 1 + n, :]
                     + cw[2:3] * a[seq(b)])
        conv_out[b] = a[n * (b + 1) - (CONV_W - 1):n * (b + 1)]
    conv = jnp.concatenate(convs, axis=0)
    g = 0.5 * conv * (1.0 + lax.erf(conv * np.float32(np.sqrt(0.5))))
    acc_scr[...] += _dot((g * v).astype(BF16), wdn_out[...])

    @pl.when(t == pl.num_programs(0) - 1)
    def _():
        for b in range(nb):
            x2 = x1_scr[seq(b), :] + mod_ref[b][5:6] * acc_scr[seq(b), :]
            y_ref[seq(b), :] = _rms(x2, gfin_ref[...])


def _ffn_rows(x, ohg, omla, mod, conv0, wout, gffn, wup, cw, cb, wdn, gfin):
    nb, n, D = x.shape
    r = nb * n
    ft = 256
    nf = D_FF // ft
    omla_rows = jnp.swapaxes(omla, 1, 2).reshape(r, MLA_HEADS * MLA_V)
    whole = lambda a: pl.BlockSpec(a.shape, lambda t: (0,) * a.ndim)
    y, conv_new, wa, wv, wdn_b = pl.pallas_call(
        functools.partial(_ffn_rows_body, nb=nb, n=n),
        out_shape=[jax.ShapeDtypeStruct((r, D), F32), jax.ShapeDtypeStruct((nb, CONV_W - 1, D_FF), F32),
                   jax.ShapeDtypeStruct((D, D_FF), BF16), jax.ShapeDtypeStruct((D, D_FF), BF16),
                   jax.ShapeDtypeStruct((D_FF, D), BF16)],
        grid=(nf,),
        in_specs=[pl.BlockSpec((r, D), lambda t: (0, 0)), pl.BlockSpec((r, HG_W), lambda t: (0, 0)),
                  pl.BlockSpec((r, MLA_HEADS * MLA_V), lambda t: (0, 0)), whole(mod),
                  pl.BlockSpec((nb, CARRY_ROWS, ft), lambda t: (0, 0, t)),
                  whole(wout), whole(gffn),
                  pl.BlockSpec((D, ft), lambda t: (0, t)), pl.BlockSpec((D, ft), lambda t: (0, nf + t)),
                  pl.BlockSpec((CONV_W, ft), lambda t: (0, t)), pl.BlockSpec((1, ft), lambda t: (0, t)),
                  pl.BlockSpec((ft, D), lambda t: (t, 0)), whole(gfin)],
        out_specs=[pl.BlockSpec((r, D), lambda t: (0, 0)),
                   pl.BlockSpec((nb, CONV_W - 1, ft), lambda t: (0, 0, t)),
                   pl.BlockSpec((D, ft), lambda t: (0, t)), pl.BlockSpec((D, ft), lambda t: (0, t)),
                   pl.BlockSpec((ft, D), lambda t: (t, 0))],
        scratch_shapes=[pltpu.VMEM((r, D), F32), pltpu.VMEM((r, D), BF16),
                        pltpu.VMEM((nb * (CARRY_ROWS + n), ft), F32), pltpu.VMEM((r, D), F32)],
        compiler_params=pltpu.CompilerParams(dimension_semantics=("arbitrary",), vmem_limit_bytes=VMEM_LIMIT),
        name="out_ffn_rows",
    )(x.reshape(r, D), ohg.reshape(r, HG_W), omla_rows, mod, conv0, wout, gffn, wup, wup, cw, cb, wdn, gfin)
    return y.reshape(nb, n, D), conv_new, wa, wv, wdn_b


def _prep_weights(w_in, mla_q_norm_gain, mla_kv_norm_gain, w_uq, w_uk, w_uv):
    win = w_in.astype(BF16)
    kp = win[:, 4 * HG_W + Q_LORA + KV_LORA:]
    wkx = jnp.concatenate([jnp.tile(kp[:, :ROPE_HALF], (1, MLA_HEADS)),
                           jnp.tile(kp[:, ROPE_HALF:], (1, MLA_HEADS))], axis=1)
    uq = w_uq.reshape(Q_LORA, MLA_HEADS, MLA_NOPE + MLA_ROPE)
    wuqn = uq[:, :, :MLA_NOPE].reshape(Q_LORA, MLA_HEADS * MLA_NOPE).astype(BF16)
    x1a = uq[:, :, MLA_NOPE:MLA_NOPE + ROPE_HALF].reshape(Q_LORA, MLA_HEADS * ROPE_HALF)
    x2a = uq[:, :, MLA_NOPE + ROPE_HALF:].reshape(Q_LORA, MLA_HEADS * ROPE_HALF)
    wuqr = jnp.concatenate([x1a, x2a, jnp.roll(x1a, ROPE_HALF, axis=1), jnp.roll(x2a, ROPE_HALF, axis=1)],
                           axis=1).astype(BF16)
    wuk = w_uk.reshape(KV_LORA, MLA_HEADS * MLA_NOPE).astype(BF16)
    wuv = w_uv.reshape(KV_LORA, MLA_HEADS * MLA_V).astype(BF16)
    return (win, wkx, kp.T, mla_q_norm_gain.reshape(1, -1), mla_kv_norm_gain.reshape(1, -1), wuqn, wuqr, wuk, wuv)


def _rope_tables(start, n):
    inv_freq = ROPE_THETA ** (-np.arange(ROPE_HALF, dtype=np.float64) / ROPE_HALF)
    ang = (start + np.arange(n, dtype=np.float64))[:, None] * inv_freq[None, :]
    cos, sin = np.cos(ang).astype(np.float32), np.sin(ang).astype(np.float32)
    return tuple(jnp.asarray(a) for a in (np.tile(cos, (1, MLA_HEADS)), np.tile(sin, (1, MLA_HEADS)), cos.T, sin.T))


def _tile(n, pref):
    return pref if n % pref == 0 else n


def kernel(x_prompt, x_sample, c_prompt, c_sample, cache_kv_latent, cache_k_rope, state_hgrn, state_ffn_conv, w_ada, b_ada, norm_mix_gain, w_in, hg_lb_logits, hg_norm_gain, mla_q_norm_gain, mla_kv_norm_gain, w_uq, w_uk, w_uv, w_out, norm_ffn_gain, w_up, conv_w, conv_b, w_down, final_norm_gain):
    assert w_ada.shape[0] == 1, "single-layer trunk"
    B, L, D = x_prompt.shape
    Bs, Ls, _ = x_sample.shape
    past = cache_kv_latent.shape[2]

    mod = _ada(jnp.concatenate([c_prompt, c_sample], axis=0), w_ada[0], b_ada)
    mod = mod.reshape(B + Bs, 6, D)
    wts = _prep_weights(w_in[0], mla_q_norm_gain[0], mla_kv_norm_gain[0], w_uq[0], w_uk[0], w_uv[0])
    gmix = norm_mix_gain
    wout = w_out[0].astype(BF16)
    gfin = final_norm_gain.reshape(1, D)
    ffn_bf16 = {}

    def ffn_weights():
        if not ffn_bf16:
            ffn_bf16["w"] = (w_up[0][:, :D_FF].astype(BF16), w_up[0][:, D_FF:].astype(BF16), w_down[0].astype(BF16))
        wa, wv, wdn = ffn_bf16["w"]
        return (wout, norm_ffn_gain, wa, wv, conv_w[0], conv_b, wdn, gfin)

    def layer(x, mod_x, pos, s0, conv0, cache):
        n, l, _ = x.shape
        rope = _rope_tables(pos, l)
        tm = _tile(l, 512)
        if tm % UNIT == 0:
            ohg, s_new, q, k, v, lat, kpe = _inproj_hgrn(x, mod_x, gmix, wts, rope, hg_lb_logits, hg_norm_gain,
                                                         s0, tm)
        else:
            raw, q, k, v, lat, kpe = _inproj(x, mod_x, gmix, wts, rope, tm)
            ohg, s_new = _hgrn(raw, hg_lb_logits, hg_norm_gain, s0, _tile(l, 256))
        if cache is None:
            omla = _attn(q, k, v, _tile(l, 512))
        else:
            omla = _attn_cache(q, k, v, cache[0], cache[1], wts[7], wts[8])
        if l % 512 != 0 and l % CARRY_ROWS == 0 and (n * l) % 16 == 0 and n * l <= 512 and not ffn_bf16:
            y, conv_new, *w = _ffn_rows(x, ohg, omla, mod_x, conv0, wout, norm_ffn_gain, w_up[0], conv_w[0], conv_b,
                                        w_down[0], gfin)
            ffn_bf16["w"] = tuple(w)
        else:
            y, conv_new = _ffn(x, ohg, omla, mod_x, conv0, *ffn_weights(), _tile(l, 512), FF_TILE)
        return y, lat[None], jnp.swapaxes(kpe, 1, 2)[None], s_new[None], conv_new[None]

    zeros_state = jnp.zeros((B, HG_HEADS, HG_DK, HG_DV), F32)
    zeros_conv = jnp.zeros((B, CARRY_ROWS, D_FF), F32)
    conv0_s = jnp.pad(state_ffn_conv[0], ((0, 0), (CARRY_ROWS - (CONV_W - 1), 0), (0, 0)))
    ys, lats, kpes, hgs, cvs = layer(x_sample, mod[B:], past, state_hgrn[0], conv0_s,
                                     (cache_kv_latent[0], jnp.swapaxes(cache_k_rope[0], 1, 2)))
    yp, latp, kpep, hgp, cvp = layer(x_prompt, mod[:B], 0, zeros_state, zeros_conv, None)
    return (yp, ys, latp, kpep, hgp, cvp, lats, kpes, hgs, cvs)
```
